```python
import jax, jax.numpy as jnp
from jax import lax
import numpy as np

D_MODEL = 1024
BATCH = 2
SEQ = 16384
DEPTH = 4

GRID_W = 64
CTX_LEN = 256
EPS = 1e-6
MLA_HEADS = 8
MLA_NOPE = 64
MLA_ROPE = 32
MLA_V = 64
MLA_Q_RANK = 384
MLA_KV_RANK = 256
MLA_SCALE = (MLA_NOPE + MLA_ROPE) ** -0.5
ROPE_BASE = 10000.0
Q_BLOCK = 128
CM_CHUNK = 128
CM_GROUPS = 4
CM_GROUP_DIM = 128
CM_WIDTH = CM_GROUPS * CM_GROUP_DIM
GLA_HEADS = 4
GLA_DK = 128
GLA_DV = 256
GLA_GATE_RANK = 16
GLA_TAU = 16.0
GLA_CHUNK = 64
D_FF = 4 * D_MODEL

E_Q = MLA_Q_RANK
E_KV = E_Q + MLA_KV_RANK
E_R = E_KV + MLA_ROPE
E_U = E_R + CM_WIDTH
EVEN_IN = E_U + CM_WIDTH
EVEN_MIX = MLA_HEADS * MLA_V + CM_WIDTH
O_K = GLA_HEADS * GLA_DK
O_V = O_K + GLA_HEADS * GLA_DV
O_ZF = O_V + GLA_GATE_RANK
O_ZB = O_ZF + GLA_GATE_RANK
O_Q = O_ZB + GLA_HEADS * GLA_DK
ODD_MIX = GLA_HEADS * GLA_DV
ODD_IN = O_Q + ODD_MIX
N_EVEN = (DEPTH + 1) // 2
N_ODD = DEPTH // 2

kernel_name = "hybrid_mla_chunkmlp_gla_dit"

F32 = jnp.float32


def rmsnorm(x, g):
    xf = x.astype(F32)
    y = xf * lax.rsqrt(jnp.mean(xf * xf, axis=-1, keepdims=True) + EPS)
    return (y * g.astype(F32)).astype(x.dtype)


def modulate(h, shift, scale):
    return h * (1.0 + scale) + shift


def axial_rope_tables(length):
    rows = length // GRID_W
    r = jnp.repeat(jnp.arange(rows, dtype=F32), GRID_W)
    col = jnp.tile(jnp.arange(GRID_W, dtype=F32), rows)
    half = MLA_ROPE // 2
    inv = ROPE_BASE ** (-jnp.arange(0, half, 2, dtype=F32) / half)
    ang_r = r[:, None] * inv
    ang_c = col[:, None] * inv
    ang = jnp.concatenate([ang_r, ang_r, ang_c, ang_c], axis=-1)
    return jnp.cos(ang), jnp.sin(ang)


def apply_axial_rope(x, cos, sin):
    xs = x.reshape(x.shape[:-1] + (2, 2, MLA_ROPE // 4))
    rot = jnp.stack([-xs[..., 1, :], xs[..., 0, :]], axis=-2).reshape(x.shape)
    return (x.astype(F32) * cos + rot.astype(F32) * sin).astype(x.dtype)


def mla_kv(ckv_raw, kr_raw, kv_norm, w_ukv, cos, sin):
    ckv = rmsnorm(ckv_raw, kv_norm)
    kv = (ckv @ w_ukv).reshape(ckv.shape[:-1] + (MLA_HEADS, MLA_NOPE + MLA_V))
    k_rope = kr_raw if cos is None else apply_axial_rope(kr_raw, cos, sin)
    return kv[..., :MLA_NOPE], k_rope, kv[..., MLA_NOPE:]


def mla_q(cq_raw, q_norm, w_uq, cos, sin):
    q = (rmsnorm(cq_raw, q_norm) @ w_uq).reshape(cq_raw.shape[:-1] + (MLA_HEADS, MLA_NOPE + MLA_ROPE))
    q_nope, q_rope = q[..., :MLA_NOPE], q[..., MLA_NOPE:]
    if cos is not None:
        q_rope = apply_axial_rope(q_rope, cos[:, None, :], sin[:, None, :])
    return q_nope, q_rope


def mla_attend(q_nope, q_rope, k_nope, k_rope, v):
    s = (jnp.einsum('bqhd,bkhd->bhqk', q_nope, k_nope, preferred_element_type=F32)
         + jnp.einsum('bqhd,bkd->bhqk', q_rope, k_rope, preferred_element_type=F32)) * MLA_SCALE
    p = jax.nn.softmax(s, axis=-1).astype(v.dtype)
    return jnp.einsum('bhqk,bkhd->bqhd', p, v)


def mla_attend_blocks(q_nope, q_rope, k_nope, k_rope, v):
    B, L = q_nope.shape[:2]
    nb = L // Q_BLOCK
    qn = q_nope.reshape(B, nb, Q_BLOCK, MLA_HEADS, MLA_NOPE).transpose(1, 0, 2, 3, 4)
    qr = q_rope.reshape(B, nb, Q_BLOCK, MLA_HEADS, MLA_ROPE).transpose(1, 0, 2, 3, 4)
    out = lax.map(lambda qs: mla_attend(qs[0], qs[1], k_nope, k_rope, v), (qn, qr))
    return out.transpose(1, 0, 2, 3, 4).reshape(B, L, MLA_HEADS, MLA_V)


def chunk_mlp(u_raw, v_raw, v_norm, ws, bs):
    B, L = u_raw.shape[:2]
    n = L // CM_CHUNK
    u = jax.nn.gelu(u_raw)
    v = jax.nn.gelu(v_raw).reshape(B, n, CM_CHUNK, CM_GROUPS, CM_GROUP_DIM)
    v = rmsnorm(v, v_norm)
    v = jnp.einsum('gts,bnsgc->bntgc', ws, v) + bs.T[:, :, None]
    return u * v.reshape(B, L, CM_WIDTH)


def even_mixer(hl, hc, w_in, q_norm, w_uq, kv_norm, w_ukv, cm_norm, cm_ws, cm_bs, w_out, cos, sin, need_ctx):
    B, L = hl.shape[:2]
    pl = hl @ w_in
    if need_ctx:
        pc = hc @ w_in
        kvc = pc[..., E_Q:E_R]
    else:
        kvc = hc @ w_in[:, E_Q:E_R]
    kc_n, kc_r, vc = mla_kv(kvc[..., :MLA_KV_RANK], kvc[..., MLA_KV_RANK:], kv_norm, w_ukv, None, None)
    kl_n, kl_r, vl = mla_kv(pl[..., E_Q:E_KV], pl[..., E_KV:E_R], kv_norm, w_ukv, cos, sin)
    ql_n, ql_r = mla_q(pl[..., :E_Q], q_norm, w_uq, cos, sin)
    k_n = jnp.concatenate([kl_n, kc_n], axis=1)
    k_r = jnp.concatenate([kl_r, kc_r], axis=1)
    v_all = jnp.concatenate([vl, vc], axis=1)
    al = mla_attend_blocks(ql_n, ql_r, k_n, k_r, v_all)
    ml = chunk_mlp(pl[..., E_R:E_U], pl[..., E_U:], cm_norm, cm_ws, cm_bs)
    yl = jnp.concatenate([al.reshape(B, L, MLA_HEADS * MLA_V), ml], axis=-1) @ w_out
    yc = None
    if need_ctx:
        qc_n, qc_r = mla_q(pc[..., :E_Q], q_norm, w_uq, None, None)
        ac = mla_attend(qc_n, qc_r, kc_n, kc_r, vc)
        mc = chunk_mlp(pc[..., E_R:E_U], pc[..., E_U:], cm_norm, cm_ws, cm_bs)
        Lc = hc.shape[1]
        yc = jnp.concatenate([ac.reshape(B, Lc, MLA_HEADS * MLA_V), mc], axis=-1) @ w_out
    return yl, yc


def gla_log_decay(z, w_up, b):
    g = (z @ w_up + b).astype(F32)
    return (jax.nn.log_sigmoid(g) / GLA_TAU).reshape(z.shape[:2] + (GLA_HEADS, GLA_DK))


def gla_chunked(q, k, v, logd, s0):
    B, L = q.shape[:2]
    n = L // GLA_CHUNK

    def blk(t):
        return t.reshape(B, n, GLA_CHUNK, GLA_HEADS, t.shape[-1]).transpose(1, 0, 3, 2, 4)

    qb, kb, vb, gb = blk(q), blk(k), blk(v), blk(logd)
    bcum = jnp.cumsum(gb, axis=3)
    blast = bcum[..., -1:, :]
    q_t = qb * jnp.exp(bcum)
    k_t = kb * jnp.exp(-bcum)
    k_end = kb * jnp.exp(blast - bcum)
    mask = jnp.tril(jnp.ones((GLA_CHUNK, GLA_CHUNK), dtype=bool))
    a = jnp.where(mask, jnp.einsum('nbhtd,nbhsd->nbhts', q_t, k_t), 0.0)
    o_intra = jnp.einsum('nbhts,nbhsv->nbhtv', a, vb)
    decay = jnp.exp(blast[..., 0, :])

    def step(s, xs):
        qt, ke, vv, dc = xs
        o = jnp.einsum('bhtd,bhdv->bhtv', qt, s)
        s = dc[..., None] * s + jnp.einsum('bhtd,bhtv->bhdv', ke, vv)
        return s, o

    s_fin, o_inter = lax.scan(step, s0, (q_t, k_end, vb, decay))
    o = (o_intra + o_inter).transpose(1, 0, 3, 2, 4).reshape(B, L, GLA_HEADS, GLA_DV)
    return o, s_fin


def gla_final_state(k, v, logd):
    bcum = jnp.cumsum(logd, axis=1)
    kw = k * jnp.exp(bcum[:, -1:] - bcum)
    return jnp.einsum('blhd,blhv->bhdv', kw, v)


def gla_output(o, r_raw, o_norm, w_out, dtype):
    B, L = o.shape[:2]
    o = rmsnorm(o, o_norm).astype(dtype)
    r = jax.nn.silu(r_raw).reshape(B, L, GLA_HEADS, GLA_DV)
    return (o * r).reshape(B, L, ODD_MIX) @ w_out


def odd_mixer(hl, hc, w_in, w_gf, b_gf, w_gb, b_gb, o_norm, w_out, need_ctx):
    flip = lambda t: jnp.flip(t, axis=1)

    def state_inputs(p):
        shp = p.shape[:2]
        k = p[..., :O_K].reshape(shp + (GLA_HEADS, GLA_DK)).astype(F32)
        v = p[..., O_K:O_V].reshape(shp + (GLA_HEADS, GLA_DV)).astype(F32)
        return k, v, gla_log_decay(p[..., O_V:O_ZF], w_gf, b_gf), gla_log_decay(p[..., O_ZF:O_ZB], w_gb, b_gb)

    def query(p):
        return p[..., O_ZB:O_Q].reshape(p.shape[:2] + (GLA_HEADS, GLA_DK)).astype(F32) * (GLA_DK ** -0.5)

    B = hl.shape[0]
    pl = hl @ w_in
    pc = hc @ (w_in if need_ctx else w_in[:, :O_ZB])
    kc, vc, fc, bc = state_inputs(pc)
    yc = None
    if need_ctx:
        qc = query(pc)
        s0 = jnp.zeros((B, GLA_HEADS, GLA_DK, GLA_DV), F32)
        oc_f, s_f = gla_chunked(qc, kc, vc, fc, s0)
        oc_b, s_b = gla_chunked(flip(qc), flip(kc), flip(vc), flip(bc), s0)
        yc = gla_output(oc_f + flip(oc_b), pc[..., O_Q:], o_norm, w_out, hc.dtype)
    else:
        s_f = gla_final_state(kc, vc, fc)
        s_b = gla_final_state(flip(kc), flip(vc), flip(bc))
    kl, vl, fl, bl = state_inputs(pl)
    ql = query(pl)
    ol_f, _ = gla_chunked(ql, kl, vl, fl, s_f)
    ol_b, _ = gla_chunked(flip(ql), flip(kl), flip(vl), flip(bl), s_b)
    yl = gla_output(ol_f + flip(ol_b), pl[..., O_Q:], o_norm, w_out, hl.dtype)
    return yl, yc


def sqrelu_mlp(h, w1, w2):
    return jnp.square(jax.nn.relu(h @ w1)) @ w2


def setup_inputs(seed: int = 0) -> dict:
    key = jax.random.key(seed)
    ks = iter(jax.random.split(key, 40))
    D = D_MODEL

    def nrm(shape, scale):
        return jax.random.normal(next(ks), shape, F32) * scale

    return {
        "x": nrm((BATCH, SEQ, D), 1.0),
        "c": nrm((BATCH, D), 1.0),
        "ctx": nrm((BATCH, CTX_LEN, D), 1.0),
        "c_ctx": nrm((D,), 1.0),
        "ada_w": nrm((DEPTH, D, 6 * D), 0.5 * D ** -0.5),
        "ada_b": nrm((DEPTH, 6 * D), 0.02),
        "norm1_g": 1.0 + nrm((DEPTH, D), 0.02),
        "norm2_g": 1.0 + nrm((DEPTH, D), 0.02),
        "mlp_w1": nrm((DEPTH, D, D_FF), D ** -0.5),
        "mlp_w2": nrm((DEPTH, D_FF, D), D_FF ** -0.5),
        "ev_w_in": nrm((N_EVEN, D, EVEN_IN), D ** -0.5),
        "ev_q_norm": 1.0 + nrm((N_EVEN, MLA_Q_RANK), 0.02),
        "ev_w_uq": nrm((N_EVEN, MLA_Q_RANK, MLA_HEADS * (MLA_NOPE + MLA_ROPE)), MLA_Q_RANK ** -0.5),
        "ev_kv_norm": 1.0 + nrm((N_EVEN, MLA_KV_RANK), 0.02),
        "ev_w_ukv": nrm((N_EVEN, MLA_KV_RANK, MLA_HEADS * (MLA_NOPE + MLA_V)), MLA_KV_RANK ** -0.5),
        "ev_cm_norm": 1.0 + nrm((N_EVEN, CM_GROUP_DIM), 0.02),
        "ev_cm_ws": nrm((N_EVEN, CM_GROUPS, CM_CHUNK, CM_CHUNK), CM_CHUNK ** -0.5),
        "ev_cm_bs": 1.0 + nrm((N_EVEN, CM_GROUPS, CM_CHUNK), 0.02),
        "ev_w_out": nrm((N_EVEN, EVEN_MIX, D), EVEN_MIX ** -0.5),
        "od_w_in": nrm((N_ODD, D, ODD_IN), D ** -0.5),
        "od_w_gf": nrm((N_ODD, GLA_GATE_RANK, GLA_HEADS * GLA_DK), GLA_GATE_RANK ** -0.5),
        "od_b_gf": nrm((N_ODD, GLA_HEADS * GLA_DK), 0.02),
        "od_w_gb": nrm((N_ODD, GLA_GATE_RANK, GLA_HEADS * GLA_DK), GLA_GATE_RANK ** -0.5),
        "od_b_gb": nrm((N_ODD, GLA_HEADS * GLA_DK), 0.02),
        "od_o_norm": 1.0 + nrm((N_ODD, GLA_DV), 0.02),
        "od_w_out": nrm((N_ODD, ODD_MIX, D), ODD_MIX ** -0.5),
        "final_g": 1.0 + nrm((D,), 0.02),
    }


def reference(x, c, ctx, c_ctx, ada_w, ada_b, norm1_g, norm2_g, mlp_w1, mlp_w2,
              ev_w_in, ev_q_norm, ev_w_uq, ev_kv_norm, ev_w_ukv, ev_cm_norm, ev_cm_ws, ev_cm_bs, ev_w_out,
              od_w_in, od_w_gf, od_b_gf, od_w_gb, od_b_gb, od_o_norm, od_w_out, final_g):
    L = x.shape[1]
    cos, sin = axial_rope_tables(L)
    sc = jax.nn.silu(c)
    scc = jax.nn.silu(c_ctx)
    xl, xc = x, ctx
    for i in range(DEPTH):
        need_ctx = i < DEPTH - 1
        ml = [m[:, None, :] for m in jnp.split(sc @ ada_w[i] + ada_b[i], 6, axis=-1)]
        mc = jnp.split(scc @ ada_w[i] + ada_b[i], 6, axis=-1)
        hl = modulate(rmsnorm(xl, norm1_g[i]), ml[0], ml[1])
        hc = modulate(rmsnorm(xc, norm1_g[i]), mc[0], mc[1])
        if i % 2 == 0:
            j = i // 2
            yl, yc = even_mixer(hl, hc, ev_w_in[j], ev_q_norm[j], ev_w_uq[j], ev_kv_norm[j], ev_w_ukv[j],
                                ev_cm_norm[j], ev_cm_ws[j], ev_cm_bs[j], ev_w_out[j], cos, sin, need_ctx)
        else:
            j = i // 2
            yl, yc = odd_mixer(hl, hc, od_w_in[j], od_w_gf[j], od_b_gf[j], od_w_gb[j], od_b_gb[j],
                               od_o_norm[j], od_w_out[j], need_ctx)
        xl = xl + ml[2] * yl
        xl = xl + ml[5] * sqrelu_mlp(modulate(rmsnorm(xl, norm2_g[i]), ml[3], ml[4]), mlp_w1[i], mlp_w2[i])
        if need_ctx:
            xc = xc + mc[2] * yc
            xc = xc + mc[5] * sqrelu_mlp(modulate(rmsnorm(xc, norm2_g[i]), mc[3], mc[4]), mlp_w1[i], mlp_w2[i])
    return rmsnorm(xl, final_g)
```

```python
import functools

import jax
import jax.numpy as jnp
from jax import lax
from jax.experimental import pallas as pl
from jax.experimental.pallas import tpu as pltpu

F32 = jnp.float32
BF16 = jnp.bfloat16

D_MODEL = 1024
DEPTH = 4
GRID_W = 64
EPS = 1e-6
MLA_HEADS = 8
MLA_NOPE = 64
MLA_ROPE = 32
MLA_V = 64
MLA_Q_RANK = 384
MLA_KV_RANK = 256
MLA_SCALE = (MLA_NOPE + MLA_ROPE) ** -0.5
ROPE_BASE = 10000.0
CM_CHUNK = 128
CM_GROUPS = 4
CM_GROUP_DIM = 128
CM_WIDTH = CM_GROUPS * CM_GROUP_DIM
GLA_HEADS = 4
GLA_DK = 128
GLA_DV = 256
GLA_GATE_RANK = 16
GLA_TAU = 16.0
GLA_CHUNK = 64
D_FF = 4 * D_MODEL

LANES = 128
HEAD_SLOT = LANES
VMEM_LIMIT = 56 * 1024 * 1024

EA_Q = 0
EA_KV = EA_Q + MLA_Q_RANK
EA_KR = EA_KV + MLA_KV_RANK
EA_U = EA_KR + LANES
EA_V = EA_U + CM_WIDTH
EA_END = EA_V + CM_WIDTH
OA_K = 0
OA_V = OA_K + GLA_HEADS * GLA_DK
OA_Q = OA_V + GLA_HEADS * GLA_DV
OA_R = OA_Q + GLA_HEADS * GLA_DK
OA_Z = OA_R + GLA_HEADS * GLA_DV
OA_END = OA_Z + LANES


def _cparams(sem):
    return pltpu.CompilerParams(dimension_semantics=sem, vmem_limit_bytes=VMEM_LIMIT)


def _rms(x, g):
    return x * lax.rsqrt(jnp.mean(x * x, axis=-1, keepdims=True) + EPS) * g


def _bdot(a, b):
    return jnp.dot(a.astype(BF16), b.astype(BF16), preferred_element_type=F32)


def _const_spec(shape):
    nd = len(shape)
    return pl.BlockSpec(shape, lambda *_: (0,) * nd, pipeline_mode=pl.Buffered(1))


def _ada_kernel(c_ref, w_ref, b_ref, o_ref):
    s = c_ref[...]
    s = s * jax.nn.sigmoid(s)
    o_ref[0] = _bdot(s, w_ref[0]) + b_ref[0]


def _ada_call(cvec, ada_w, ada_b):
    depth, d, n = ada_w.shape
    tn = 1536
    return pl.pallas_call(
        _ada_kernel,
        grid=(depth, n // tn),
        in_specs=[
            pl.BlockSpec((8, d), lambda l, j: (0, 0)),
            pl.BlockSpec((1, d, tn), lambda l, j: (l, 0, j)),
            pl.BlockSpec((1, 1, tn), lambda l, j: (l, 0, j)),
        ],
        out_specs=pl.BlockSpec((1, 8, tn), lambda l, j: (l, 0, j)),
        out_shape=jax.ShapeDtypeStruct((depth, 8, n), F32),
        compiler_params=_cparams(("arbitrary", "arbitrary")),
        name="ada_mod",
    )(cvec, ada_w, ada_b.reshape(depth, 1, n))


def _rope_slot(t, a, b):
    return t * a + pltpu.roll(t, HEAD_SLOT - MLA_ROPE, 1) * b


def _even_in_kernel(x_ref, g1_ref, sh_ref, sc_ref, ta_ref, tb_ref, wall_ref, qn_ref, wuq_ref, kvn_ref,
                    wuk_ref, wuv_ref, cmn_ref, ws_ref, bias_ref, q_ref, k_ref, v_ref, ml_ref, *, tm):
    h = _rms(x_ref[...], g1_ref[...]) * (1.0 + sc_ref[0]) + sh_ref[0]
    p = _bdot(h, wall_ref[...])
    ta = ta_ref[...]
    tb = tb_ref[...]

    cq = _rms(p[:, EA_Q:EA_KV], qn_ref[...])
    qf = _bdot(cq, wuq_ref[...])
    for hd in range(MLA_HEADS):
        sl = slice(hd * HEAD_SLOT, (hd + 1) * HEAD_SLOT)
        q_ref[:, sl] = (_rope_slot(qf[:, sl], ta, tb) * MLA_SCALE).astype(BF16)

    ckv = _rms(p[:, EA_KV:EA_KR], kvn_ref[...]).astype(BF16)
    kn = jnp.dot(ckv, wuk_ref[...], preferred_element_type=F32)
    kr = _rope_slot(pltpu.roll(p[:, EA_KR:EA_U], MLA_NOPE, 1), ta, tb)
    for hd in range(MLA_HEADS):
        sl = slice(hd * HEAD_SLOT, (hd + 1) * HEAD_SLOT)
        k_ref[:, sl] = (kn[:, sl] + kr).astype(BF16)
    v_ref[...] = jnp.dot(ckv, wuv_ref[...], preferred_element_type=F32).astype(BF16)

    u = jax.nn.gelu(p[:, EA_U:EA_V])
    vv = jax.nn.gelu(p[:, EA_V:EA_END])
    cmn = cmn_ref[...]
    for g in range(CM_GROUPS):
        gl = slice(g * CM_GROUP_DIM, (g + 1) * CM_GROUP_DIM)
        vn = _rms(vv[:, gl], cmn).astype(BF16)
        w = ws_ref[g]
        for c in range(tm // CM_CHUNK):
            rs = slice(c * CM_CHUNK, (c + 1) * CM_CHUNK)
            y = jnp.dot(w, vn[rs], preferred_element_type=F32) + bias_ref[:, gl]
            ml_ref[rs, gl] = (u[rs, gl] * y).astype(BF16)


def _even_in_call(x, g1, sh, sc, ta, tb, w, *, seq, tm):
    n, d = x.shape
    tpb = seq // tm
    tpt = ta.shape[0] // tm
    row = lambda i: (i, 0)
    bat = lambda i: (i // tpb, 0, 0)
    tab = lambda i: (i % tpt, 0)
    hq = MLA_HEADS * HEAD_SLOT
    hv = MLA_HEADS * MLA_V
    return pl.pallas_call(
        functools.partial(_even_in_kernel, tm=tm),
        grid=(n // tm,),
        in_specs=[
            pl.BlockSpec((tm, d), row),
            _const_spec((1, d)),
            pl.BlockSpec((1, 1, d), bat),
            pl.BlockSpec((1, 1, d), bat),
            pl.BlockSpec((tm, HEAD_SLOT), tab),
            pl.BlockSpec((tm, HEAD_SLOT), tab),
            _const_spec((d, EA_END)),
            _const_spec((1, MLA_Q_RANK)),
            _const_spec((MLA_Q_RANK, hq)),
            _const_spec((1, MLA_KV_RANK)),
            _const_spec((MLA_KV_RANK, hq)),
            _const_spec((MLA_KV_RANK, hv)),
            _const_spec((1, CM_GROUP_DIM)),
            _const_spec((CM_GROUPS, CM_CHUNK, CM_CHUNK)),
            _const_spec((CM_CHUNK, CM_WIDTH)),
        ],
        out_specs=[
            pl.BlockSpec((tm, hq), row),
            pl.BlockSpec((tm, hq), row),
            pl.BlockSpec((tm, hv), row),
            pl.BlockSpec((tm, CM_WIDTH), row),
        ],
        out_shape=[
            jax.ShapeDtypeStruct((n, hq), BF16),
            jax.ShapeDtypeStruct((n, hq), BF16),
            jax.ShapeDtypeStruct((n, hv), BF16),
            jax.ShapeDtypeStruct((n, CM_WIDTH), BF16),
        ],
        compiler_params=_cparams(("arbitrary",)),
        name="even_in",
    )(x, g1, sh, sc, ta, tb, w["wall"], w["qn"], w["wuq"], w["kvn"], w["wuk"], w["wuv"], w["cmn"], w["ws"],
      w["bias"])


def _attn_kernel(*refs, tq, segs):
    q_ref = refs[0]
    o_ref = refs[-1]
    outs = []
    for hh in range(2):
        q = q_ref[:, hh * HEAD_SLOT:(hh + 1) * HEAD_SLOT]
        carry = (jnp.full((tq, 1), -jnp.inf, F32), jnp.zeros((tq, 1), F32), jnp.zeros((tq, 2 * MLA_V), F32))
        for si, (t_len, tk) in enumerate(segs):
            k_ref = refs[1 + 2 * si]
            v_ref = refs[2 + 2 * si]

            def body(j, c, k_ref=k_ref, v_ref=v_ref, tk=tk):
                m, l, acc = c
                r0 = pl.multiple_of(j * tk, tk)
                ks = k_ref[pl.ds(r0, tk), hh * HEAD_SLOT:(hh + 1) * HEAD_SLOT]
                vs = v_ref[pl.ds(r0, tk), :]
                s = lax.dot_general(q, ks, (((1,), (1,)), ((), ())), preferred_element_type=F32)
                m_new = jnp.maximum(m, jnp.max(s, axis=1, keepdims=True))
                alpha = jnp.exp(m - m_new)
                p = jnp.exp(s - m_new)
                l = alpha * l + jnp.sum(p, axis=1, keepdims=True)
                acc = alpha * acc + jnp.dot(p.astype(BF16), vs, preferred_element_type=F32)
                return m_new, l, acc

            carry = lax.fori_loop(0, t_len // tk, body, carry)
        outs.append(carry[2] / carry[1])
    lane = lax.broadcasted_iota(jnp.int32, (tq, 2 * MLA_V), 1)
    o_ref[...] = jnp.where(lane < MLA_V, outs[0], outs[1]).astype(BF16)


def _attn_call(q, kvs, *, batch, tq, tks):
    n = q.shape[0]
    lq = n // batch
    nq = lq // tq
    segs = []
    in_specs = [pl.BlockSpec((tq, 2 * HEAD_SLOT), lambda b, hp, i: (b * nq + i, hp))]
    args = [q]
    for (k, v), tk in zip(kvs, tks):
        t_len = k.shape[0] // batch
        segs.append((t_len, tk))
        in_specs.append(pl.BlockSpec((t_len, 2 * HEAD_SLOT), lambda b, hp, i: (b, hp)))
        in_specs.append(pl.BlockSpec((t_len, 2 * MLA_V), lambda b, hp, i: (b, hp)))
        args += [k, v]
    return pl.pallas_call(
        functools.partial(_attn_kernel, tq=tq, segs=tuple(segs)),
        grid=(batch, MLA_HEADS // 2, nq),
        in_specs=in_specs,
        out_specs=pl.BlockSpec((tq, 2 * MLA_V), lambda b, hp, i: (b * nq + i, hp)),
        out_shape=jax.ShapeDtypeStruct((n, MLA_HEADS * MLA_V), BF16),
        compiler_params=_cparams(("arbitrary", "arbitrary", "arbitrary")),
        name="mla_attn",
    )(*args)


def _post_kernel(x_ref, a_ref, b_ref, gm_ref, g2_ref, sh_ref, sc_ref, gf_ref, wo_ref, w1_ref, w2_ref, fg_ref,
                 o_ref, *, final):
    half = wo_ref.shape[0] // 2
    y = (jnp.dot(a_ref[...], wo_ref[:half, :], preferred_element_type=F32)
         + jnp.dot(b_ref[...], wo_ref[half:, :], preferred_element_type=F32))
    x1 = x_ref[...] + gm_ref[0] * y
    h2 = (_rms(x1, g2_ref[...]) * (1.0 + sc_ref[0]) + sh_ref[0]).astype(BF16)
    acc = jnp.zeros_like(x1)
    fc = 1024
    for c in range(w1_ref.shape[1] // fc):
        hc = jnp.dot(h2, w1_ref[:, c * fc:(c + 1) * fc], preferred_element_type=F32)
        hc = jnp.square(jnp.maximum(hc, 0.0)).astype(BF16)
        acc = acc + jnp.dot(hc, w2_ref[c * fc:(c + 1) * fc, :], preferred_element_type=F32)
    x2 = x1 + gf_ref[0] * acc
    if final:
        x2 = _rms(x2, fg_ref[...])
    o_ref[...] = x2


def _post_call(x, mix_a, mix_b, col_a, col_b, gm, g2, sh, sc, gf, wo, w1, w2, fg, *, seq, tm, final):
    n, d = x.shape
    tpb = seq // tm
    row = lambda i: (i, 0)
    bat = lambda i: (i // tpb, 0, 0)
    half = wo.shape[0] // 2
    return pl.pallas_call(
        functools.partial(_post_kernel, final=final),
        grid=(n // tm,),
        in_specs=[
            pl.BlockSpec((tm, d), row),
            pl.BlockSpec((tm, half), lambda i: (i, col_a)),
            pl.BlockSpec((tm, half), lambda i: (i, col_b)),
            pl.BlockSpec((1, 1, d), bat),
            _const_spec((1, d)),
            pl.BlockSpec((1, 1, d), bat),
            pl.BlockSpec((1, 1, d), bat),
            pl.BlockSpec((1, 1, d), bat),
            _const_spec(wo.shape),
            _const_spec(w1.shape),
            _const_spec(w2.shape),
            _const_spec((1, d)),
        ],
        out_specs=pl.BlockSpec((tm, d), row),
        out_shape=jax.ShapeDtypeStruct((n, d), F32),
        compiler_params=_cparams(("arbitrary",)),
        name="post_mlp",
    )(x, mix_a, mix_b, gm, g2, sh, sc, gf, wo, w1, w2, fg)


def _log_decay(g):
    return (jnp.minimum(g, 0.0) - jnp.log1p(jnp.exp(-jnp.abs(g)))) * (1.0 / GLA_TAU)


def _odd_in_kernel(x_ref, g1_ref, sh_ref, sc_ref, wall_ref, wgf_ref, bgf_ref, wgb_ref, bgb_ref,
                   k_ref, v_ref, q_ref, r_ref, df_ref, db_ref):
    h = _rms(x_ref[...], g1_ref[...]) * (1.0 + sc_ref[0]) + sh_ref[0]
    p = _bdot(h, wall_ref[...])
    k_ref[...] = p[:, OA_K:OA_V]
    v_ref[...] = p[:, OA_V:OA_Q].astype(BF16)
    q_ref[...] = p[:, OA_Q:OA_R] * (GLA_DK ** -0.5)
    r = p[:, OA_R:OA_Z]
    r_ref[...] = r * jax.nn.sigmoid(r)
    z = p[:, OA_Z:OA_END].astype(BF16)
    df_ref[...] = _log_decay(jnp.dot(z, wgf_ref[...], preferred_element_type=F32) + bgf_ref[...])
    db_ref[...] = _log_decay(jnp.dot(z, wgb_ref[...], preferred_element_type=F32) + bgb_ref[...])


def _odd_in_call(x, g1, sh, sc, w, *, seq, tm):
    n, d = x.shape
    tpb = seq // tm
    row = lambda i: (i, 0)
    bat = lambda i: (i // tpb, 0, 0)
    kd = GLA_HEADS * GLA_DK
    vd = GLA_HEADS * GLA_DV
    return pl.pallas_call(
        _odd_in_kernel,
        grid=(n // tm,),
        in_specs=[
            pl.BlockSpec((tm, d), row),
            _const_spec((1, d)),
            pl.BlockSpec((1, 1, d), bat),
            pl.BlockSpec((1, 1, d), bat),
            _const_spec((d, OA_END)),
            _const_spec((LANES, kd)),
            _const_spec((1, kd)),
            _const_spec((LANES, kd)),
            _const_spec((1, kd)),
        ],
        out_specs=[
            pl.BlockSpec((tm, kd), row),
            pl.BlockSpec((tm, vd), row),
            pl.BlockSpec((tm, kd), row),
            pl.BlockSpec((tm, vd), row),
            pl.BlockSpec((tm, kd), row),
            pl.BlockSpec((tm, kd), row),
        ],
        out_shape=[
            jax.ShapeDtypeStruct((n, kd), F32),
            jax.ShapeDtypeStruct((n, vd), BF16),
            jax.ShapeDtypeStruct((n, kd), F32),
            jax.ShapeDtypeStruct((n, vd), F32),
            jax.ShapeDtypeStruct((n, kd), F32),
            jax.ShapeDtypeStruct((n, kd), F32),
        ],
        compiler_params=_cparams(("arbitrary",)),
        name="odd_in",
    )(x, g1, sh, sc, w["wall"], w["wgf"], w["bgf"], w["wgb"], w["bgb"])


def _gla_kernel(*refs, reverse, nchunk, combine):
    if combine:
        q_ref, k_ref, g_ref, v_ref, s0_ref, of_ref, r_ref, on_ref, o_ref, sfin_ref, s_scr = refs
    else:
        q_ref, k_ref, g_ref, v_ref, s0_ref, o_ref, sfin_ref, s_scr = refs
    i = pl.program_id(2)

    @pl.when(i == 0)
    def _():
        s_scr[...] = s0_ref[...]

    ch = GLA_CHUNK
    rr = lax.broadcasted_iota(jnp.int32, (ch, ch), 0)
    cc = lax.broadcasted_iota(jnp.int32, (ch, ch), 1)
    mask = (rr <= cc) if reverse else (rr >= cc)
    tri = mask.astype(F32)
    order = range(nchunk - 1, -1, -1) if reverse else range(nchunk)
    for c in order:
        sl = slice(c * ch, (c + 1) * ch)
        g = g_ref[sl, :]
        q = q_ref[sl, :]
        k = k_ref[sl, :]
        v = v_ref[sl, :]
        bc = jnp.dot(tri, g, preferred_element_type=F32, precision=lax.Precision.HIGHEST)
        tot = bc[0:1, :] if reverse else bc[ch - 1:ch, :]
        q_t = (q * jnp.exp(bc)).astype(BF16)
        k_t = (k * jnp.exp(-bc)).astype(BF16)
        k_end = k * jnp.exp(tot - bc)
        a = lax.dot_general(q_t, k_t, (((1,), (1,)), ((), ())), preferred_element_type=F32)
        a = jnp.where(mask, a, 0.0).astype(BF16)
        s = s_scr[...]
        o = (jnp.dot(a, v, preferred_element_type=F32)
             + jnp.dot(q_t, s.astype(BF16), preferred_element_type=F32))
        dec = jnp.exp(jnp.sum(g.T, axis=1, keepdims=True))
        s_scr[...] = dec * s + jnp.dot(k_end.T.astype(BF16), v, preferred_element_type=F32)
        if combine:
            o = _rms(o + of_ref[sl, :], on_ref[...]) * r_ref[sl, :]
            o_ref[sl, :] = o.astype(o_ref.dtype)
        else:
            o_ref[sl, :] = o

    @pl.when(i == pl.num_programs(2) - 1)
    def _():
        sfin_ref[...] = s_scr[...]


def _gla_call(q, k, g, v, s0, *, reverse, tb, o_fwd=None, r=None, o_norm=None):
    b, l, _ = q.shape
    nb = l // tb
    combine = o_fwd is not None
    blk = (lambda bi, h, i: (bi, nb - 1 - i, h)) if reverse else (lambda bi, h, i: (bi, i, h))
    st = lambda bi, h, i: (bi, h, 0, 0)
    kspec = pl.BlockSpec((None, tb, GLA_DK), blk)
    vspec = pl.BlockSpec((None, tb, GLA_DV), blk)
    sspec = pl.BlockSpec((None, None, GLA_DK, GLA_DV), st)
    in_specs = [kspec, kspec, kspec, vspec, sspec]
    args = [q, k, g, v, s0]
    if combine:
        in_specs += [vspec, vspec, pl.BlockSpec((1, GLA_DV), lambda bi, h, i: (0, 0))]
        args += [o_fwd, r, o_norm]
    return pl.pallas_call(
        functools.partial(_gla_kernel, reverse=reverse, nchunk=tb // GLA_CHUNK, combine=combine),
        grid=(b, GLA_HEADS, nb),
        in_specs=in_specs,
        out_specs=[vspec, sspec],
        out_shape=[
            jax.ShapeDtypeStruct((b, l, GLA_HEADS * GLA_DV), BF16 if combine else F32),
            jax.ShapeDtypeStruct((b, GLA_HEADS, GLA_DK, GLA_DV), F32),
        ],
        scratch_shapes=[pltpu.VMEM((GLA_DK, GLA_DV), F32)],
        compiler_params=_cparams(("arbitrary", "arbitrary", "arbitrary")),
        name="gla_bwd" if reverse else "gla_fwd",
    )(*args)


def _rot_cols(w):
    q = MLA_ROPE // 4
    return jnp.concatenate([-w[..., q:2 * q], w[..., 0:q], -w[..., 3 * q:4 * q], w[..., 2 * q:3 * q]], axis=-1)


def _prep_even(w_in, q_norm, w_uq, kv_norm, w_ukv, cm_norm, cm_ws, cm_bs):
    d = w_in.shape[0]
    e_q, e_kv = MLA_Q_RANK, MLA_Q_RANK + MLA_KV_RANK
    e_r = e_kv + MLA_ROPE
    e_u = e_r + CM_WIDTH
    wkr = w_in[:, e_kv:e_r]
    wall = jnp.concatenate([w_in[:, :e_kv], wkr, _rot_cols(wkr), jnp.zeros((d, LANES - 2 * MLA_ROPE), F32),
                            w_in[:, e_r:e_u], w_in[:, e_u:]], axis=1).astype(BF16)
    uq = w_uq.reshape(MLA_Q_RANK, MLA_HEADS, MLA_NOPE + MLA_ROPE)
    uq_r = uq[..., MLA_NOPE:]
    wuq = jnp.concatenate([uq, _rot_cols(uq_r)], axis=-1).reshape(MLA_Q_RANK, MLA_HEADS * HEAD_SLOT).astype(BF16)
    ukv = w_ukv.reshape(MLA_KV_RANK, MLA_HEADS, MLA_NOPE + MLA_V)
    wuk = jnp.concatenate([ukv[..., :MLA_NOPE], jnp.zeros((MLA_KV_RANK, MLA_HEADS, HEAD_SLOT - MLA_NOPE), F32)],
                          axis=-1).reshape(MLA_KV_RANK, MLA_HEADS * HEAD_SLOT).astype(BF16)
    wuv = ukv[..., MLA_NOPE:].reshape(MLA_KV_RANK, MLA_HEADS * MLA_V).astype(BF16)
    bias = jnp.repeat(cm_bs.T, CM_GROUP_DIM, axis=1)
    return dict(wall=wall, qn=q_norm[None], wuq=wuq, kvn=kv_norm[None], wuk=wuk, wuv=wuv, cmn=cm_norm[None],
                ws=cm_ws.astype(BF16), bias=bias)


def _prep_odd(w_in, w_gf, b_gf, w_gb, b_gb):
    d = w_in.shape[0]
    o_k = GLA_HEADS * GLA_DK
    o_v = o_k + GLA_HEADS * GLA_DV
    o_zb = o_v + 2 * GLA_GATE_RANK
    o_q = o_zb + GLA_HEADS * GLA_DK
    wall = jnp.concatenate([w_in[:, :o_v], w_in[:, o_zb:o_q], w_in[:, o_q:], w_in[:, o_v:o_zb],
                            jnp.zeros((d, LANES - 2 * GLA_GATE_RANK), F32)], axis=1).astype(BF16)
    zr = GLA_GATE_RANK
    wgf = jnp.zeros((LANES, o_k), F32).at[:zr].set(w_gf).astype(BF16)
    wgb = jnp.zeros((LANES, o_k), F32).at[zr:2 * zr].set(w_gb).astype(BF16)
    return dict(wall=wall, wgf=wgf, bgf=b_gf[None], wgb=wgb, bgb=b_gb[None])


def _rope_tables(length):
    rows = length // GRID_W
    r = jnp.repeat(jnp.arange(rows, dtype=F32), GRID_W)
    col = jnp.tile(jnp.arange(GRID_W, dtype=F32), rows)
    half = MLA_ROPE // 2
    inv = ROPE_BASE ** (-jnp.arange(0, half, 2, dtype=F32) / half)
    ang_r = r[:, None] * inv
    ang_c = col[:, None] * inv
    ang = jnp.concatenate([ang_r, ang_r, ang_c, ang_c], axis=-1)
    one = jnp.ones((length, MLA_NOPE), F32)
    pad = jnp.zeros((length, HEAD_SLOT - MLA_NOPE - MLA_ROPE), F32)
    ta = jnp.concatenate([one, jnp.cos(ang), pad], axis=1)
    tb = jnp.concatenate([0.0 * one, jnp.sin(ang), pad], axis=1)
    return ta, tb


def _flat_tables(length):
    ta = jnp.concatenate([jnp.ones((length, MLA_NOPE + MLA_ROPE), F32),
                          jnp.zeros((length, HEAD_SLOT - MLA_NOPE - MLA_ROPE), F32)], axis=1)
    return ta, jnp.zeros_like(ta)


def _row_tile(seq, want):
    t = min(seq, want)
    while seq % t:
        t //= 2
    return t


def kernel(x, c, ctx, c_ctx, ada_w, ada_b, norm1_g, norm2_g, mlp_w1, mlp_w2, ev_w_in, ev_q_norm, ev_w_uq, ev_kv_norm,
           ev_w_ukv, ev_cm_norm, ev_cm_ws, ev_cm_bs, ev_w_out, od_w_in, od_w_gf, od_b_gf, od_w_gb, od_b_gb, od_o_norm,
           od_w_out, final_g):
    batch, seq, d = x.shape
    lc = ctx.shape[1]
    depth = ada_w.shape[0]
    tm_l = _row_tile(seq, 512)
    tm_c = _row_tile(lc, 256)
    tq_l = _row_tile(seq, 256)
    tq_c = _row_tile(lc, 256)
    tk_l = _row_tile(seq, 512)
    tk_c = _row_tile(lc, 256)
    tb_l = _row_tile(seq, 256)
    tb_c = _row_tile(lc, 256)

    cvec = jnp.concatenate([c, c_ctx[None], jnp.zeros((8 - batch - 1, d), F32)], axis=0)
    mods = _ada_call(cvec, ada_w, ada_b).reshape(depth, 8, 6, d)

    ta_l, tb_l_ = _rope_tables(seq)
    ta_c, tb_c_ = _flat_tables(lc)

    xl = x.reshape(batch * seq, d)
    xc = ctx.reshape(batch * lc, d)
    for i in range(depth):
        need_ctx = i < depth - 1
        j = i // 2
        ml = [mods[i, :batch, t][:, None, :] for t in range(6)]
        mc = [mods[i, batch:batch + 1, t][:, None, :] for t in range(6)]
        g1 = norm1_g[i][None]
        g2 = norm2_g[i][None]
        w1 = mlp_w1[i].astype(BF16)
        w2 = mlp_w2[i].astype(BF16)
        final = i == depth - 1
        if i % 2 == 0:
            w = _prep_even(ev_w_in[j], ev_q_norm[j], ev_w_uq[j], ev_kv_norm[j], ev_w_ukv[j], ev_cm_norm[j],
                           ev_cm_ws[j], ev_cm_bs[j])
            wo = ev_w_out[j].astype(BF16)
            ql, kl, vl, mll = _even_in_call(xl, g1, ml[0], ml[1], ta_l, tb_l_, w, seq=seq, tm=tm_l)
            qc, kc, vc, mlc = _even_in_call(xc, g1, mc[0], mc[1], ta_c, tb_c_, w, seq=batch * lc, tm=tm_c)
            al = _attn_call(ql, [(kl, vl), (kc, vc)], batch=batch, tq=tq_l, tks=(tk_l, tk_c))
            xl = _post_call(xl, al, mll, 0, 0, ml[2], g2, ml[3], ml[4], ml[5], wo, w1, w2, final_g[None],
                            seq=seq, tm=tm_l, final=final)
            if need_ctx:
                ac = _attn_call(qc, [(kc, vc)], batch=batch, tq=tq_c, tks=(tk_c,))
                xc = _post_call(xc, ac, mlc, 0, 0, mc[2], g2, mc[3], mc[4], mc[5], wo, w1, w2, final_g[None],
                                seq=batch * lc, tm=tm_c, final=False)
        else:
            w = _prep_odd(od_w_in[j], od_w_gf[j], od_b_gf[j], od_w_gb[j], od_b_gb[j])
            wo = od_w_out[j].astype(BF16)
            on = od_o_norm[j][None]
            kc, vc, qc, rc, dfc, dbc = _odd_in_call(xc, g1, mc[0], mc[1], w, seq=batch * lc, tm=tm_c)
            kl, vl, ql, rl, dfl, dbl = _odd_in_call(xl, g1, ml[0], ml[1], w, seq=seq, tm=tm_l)
            r3 = lambda t, n: t.reshape(batch, n, t.shape[-1])
            s0 = jnp.zeros((batch, GLA_HEADS, GLA_DK, GLA_DV), F32)
            ocf, s_f = _gla_call(r3(qc, lc), r3(kc, lc), r3(dfc, lc), r3(vc, lc), s0, reverse=False, tb=tb_c)
            mixc, s_b = _gla_call(r3(qc, lc), r3(kc, lc), r3(dbc, lc), r3(vc, lc), s0, reverse=True, tb=tb_c,
                                  o_fwd=ocf, r=r3(rc, lc), o_norm=on)
            olf, _ = _gla_call(r3(ql, seq), r3(kl, seq), r3(dfl, seq), r3(vl, seq), s_f, reverse=False, tb=tb_l)
            mixl, _ = _gla_call(r3(ql, seq), r3(kl, seq), r3(dbl, seq), r3(vl, seq), s_b, reverse=True, tb=tb_l,
                                o_fwd=olf, r=r3(rl, seq), o_norm=on)
            mixl = mixl.reshape(batch * seq, -1)
            xl = _post_call(xl, mixl, mixl, 0, 1, ml[2], g2, ml[3], ml[4], ml[5], wo, w1, w2, final_g[None],
                            seq=seq, tm=tm_l, final=final)
            if need_ctx:
                mixc = mixc.reshape(batch * lc, -1)
                xc = _post_call(xc, mixc, mixc, 0, 1, mc[2], g2, mc[3], mc[4], mc[5], wo, w1, w2, final_g[None],
                                seq=batch * lc, tm=tm_c, final=False)
    return xl.reshape(batch, seq, d)
```

```python
import functools

import jax
import jax.numpy as jnp
from jax import lax
from jax.experimental import pallas as pl
from jax.experimental.pallas import tpu as pltpu

F32 = jnp.float32
BF16 = jnp.bfloat16

D_MODEL = 1024
DEPTH = 4
GRID_W = 64
EPS = 1e-6
MLA_HEADS = 8
MLA_NOPE = 64
MLA_ROPE = 32
MLA_V = 64
MLA_Q_RANK = 384
MLA_KV_RANK = 256
MLA_SCALE = (MLA_NOPE + MLA_ROPE) ** -0.5
ROPE_BASE = 10000.0
CM_CHUNK = 128
CM_GROUPS = 4
CM_GROUP_DIM = 128
CM_WIDTH = CM_GROUPS * CM_GROUP_DIM
GLA_HEADS = 4
GLA_DK = 128
GLA_DV = 256
GLA_GATE_RANK = 16
GLA_TAU = 16.0
GLA_CHUNK = 64
D_FF = 4 * D_MODEL

LANES = 128
HEAD_SLOT = LANES
VMEM_LIMIT = 56 * 1024 * 1024
MAX_KEY_TILE = 640
Q_PRESCALE = MLA_SCALE * 1.4426950408889634

EA_Q = 0
EA_KV = EA_Q + MLA_Q_RANK
EA_KR = EA_KV + MLA_KV_RANK
EA_U = EA_KR + LANES
EA_V = EA_U + CM_WIDTH
EA_END = EA_V + CM_WIDTH
OA_K = 0
OA_V = OA_K + GLA_HEADS * GLA_DK
OA_Q = OA_V + GLA_HEADS * GLA_DV
OA_R = OA_Q + GLA_HEADS * GLA_DK
OA_Z = OA_R + GLA_HEADS * GLA_DV
OA_END = OA_Z + LANES


def _cparams(sem):
    return pltpu.CompilerParams(dimension_semantics=sem, vmem_limit_bytes=VMEM_LIMIT)


def _rms(x, g):
    return x * lax.rsqrt(jnp.mean(x * x, axis=-1, keepdims=True) + EPS) * g


def _bdot(a, b):
    return jnp.dot(a.astype(BF16), b.astype(BF16), preferred_element_type=F32)


def _const_spec(shape):
    nd = len(shape)
    return pl.BlockSpec(shape, lambda *_: (0,) * nd, pipeline_mode=pl.Buffered(1))


def _ada_kernel(c_ref, w_ref, b_ref, o_ref):
    s = c_ref[...]
    s = s * jax.nn.sigmoid(s)
    o_ref[0] = _bdot(s, w_ref[0]) + b_ref[0]


def _ada_call(cvec, ada_w, ada_b):
    depth, d, n = ada_w.shape
    tn = 1536
    return pl.pallas_call(
        _ada_kernel,
        grid=(depth, n // tn),
        in_specs=[
            pl.BlockSpec((8, d), lambda l, j: (0, 0)),
            pl.BlockSpec((1, d, tn), lambda l, j: (l, 0, j)),
            pl.BlockSpec((1, 1, tn), lambda l, j: (l, 0, j)),
        ],
        out_specs=pl.BlockSpec((1, 8, tn), lambda l, j: (l, 0, j)),
        out_shape=jax.ShapeDtypeStruct((depth, 8, n), F32),
        compiler_params=_cparams(("arbitrary", "arbitrary")),
        name="ada_mod",
    )(cvec, ada_w, ada_b.reshape(depth, 1, n))


def _rope_slot(t, a, b):
    return t * a + pltpu.roll(t, HEAD_SLOT - MLA_ROPE, 1) * b


def _dot_nt(a, b):
    return lax.dot_general(a, b, (((1,), (1,)), ((), ())), preferred_element_type=F32)


def _even_in_kernel(x_ref, g1_ref, sh_ref, sc_ref, ta_ref, tb_ref, ct_ref, st_ref, wall_ref, qn_ref, wuqt_ref,
                    kvn_ref, wuk_ref, wuvt_ref, cmn_ref, ws_ref, bias_ref, qt_ref, k_ref, vt_ref, ml_ref, *, tm):
    h = _rms(x_ref[...], g1_ref[...]) * (1.0 + sc_ref[0]) + sh_ref[0]
    p = _bdot(h, wall_ref[...])

    cq = _rms(p[:, EA_Q:EA_KV], qn_ref[...]).astype(BF16)
    qt = _dot_nt(wuqt_ref[...], cq)
    cos_t = ct_ref[...]
    sin_t = st_ref[...]
    r0, r1 = MLA_NOPE, MLA_NOPE + MLA_ROPE
    for hd in range(MLA_HEADS):
        b = hd * HEAD_SLOT
        qt_ref[b:b + r0, :] = (qt[b:b + r0, :] * Q_PRESCALE).astype(BF16)
        roped = qt[b + r0:b + r1, :] * cos_t + qt[b + r1:b + HEAD_SLOT, :] * sin_t
        qt_ref[b + r0:b + r1, :] = (roped * Q_PRESCALE).astype(BF16)
        qt_ref[b + r1:b + HEAD_SLOT, :] = jnp.zeros((HEAD_SLOT - r1, tm), BF16)

    ckv = _rms(p[:, EA_KV:EA_KR], kvn_ref[...]).astype(BF16)
    kn = jnp.dot(ckv, wuk_ref[...], preferred_element_type=F32)
    kr = _rope_slot(pltpu.roll(p[:, EA_KR:EA_U], MLA_NOPE, 1), ta_ref[...], tb_ref[...])
    for hd in range(MLA_HEADS):
        sl = slice(hd * HEAD_SLOT, (hd + 1) * HEAD_SLOT)
        k_ref[:, sl] = (kn[:, sl] + kr).astype(BF16)
    vt = _dot_nt(wuvt_ref[...], ckv)
    for hd in range(MLA_HEADS):
        b = hd * HEAD_SLOT
        vt_ref[b:b + MLA_V, :] = vt[hd * MLA_V:(hd + 1) * MLA_V, :].astype(BF16)
        vt_ref[b + MLA_V:b + HEAD_SLOT, :] = jnp.ones((HEAD_SLOT - MLA_V, tm), BF16)

    u = jax.nn.gelu(p[:, EA_U:EA_V])
    vv = jax.nn.gelu(p[:, EA_V:EA_END])
    cmn = cmn_ref[...]
    for g in range(CM_GROUPS):
        gl = slice(g * CM_GROUP_DIM, (g + 1) * CM_GROUP_DIM)
        vn = _rms(vv[:, gl], cmn).astype(BF16)
        w = ws_ref[g]
        for c in range(tm // CM_CHUNK):
            rs = slice(c * CM_CHUNK, (c + 1) * CM_CHUNK)
            y = jnp.dot(w, vn[rs], preferred_element_type=F32) + bias_ref[:, gl]
            ml_ref[rs, gl] = (u[rs, gl] * y).astype(BF16)


def _even_in_call(x, g1, sh, sc, tabs, w, *, seq, tm):
    ta, tb, cos_t, sin_t = tabs
    n, d = x.shape
    tpb = seq // tm
    tpt = ta.shape[0] // tm
    row = lambda i: (i, 0)
    col = lambda i: (0, i)
    bat = lambda i: (i // tpb, 0, 0)
    tab = lambda i: (i % tpt, 0)
    tab_t = lambda i: (0, i % tpt)
    hq = MLA_HEADS * HEAD_SLOT
    hv = MLA_HEADS * MLA_V
    return pl.pallas_call(
        functools.partial(_even_in_kernel, tm=tm),
        grid=(n // tm,),
        in_specs=[
            pl.BlockSpec((tm, d), row),
            _const_spec((1, d)),
            pl.BlockSpec((1, 1, d), bat),
            pl.BlockSpec((1, 1, d), bat),
            pl.BlockSpec((tm, HEAD_SLOT), tab),
            pl.BlockSpec((tm, HEAD_SLOT), tab),
            pl.BlockSpec((MLA_ROPE, tm), tab_t),
            pl.BlockSpec((MLA_ROPE, tm), tab_t),
            _const_spec((d, EA_END)),
            _const_spec((1, MLA_Q_RANK)),
            _const_spec((hq, MLA_Q_RANK)),
            _const_spec((1, MLA_KV_RANK)),
            _const_spec((MLA_KV_RANK, hq)),
            _const_spec((hv, MLA_KV_RANK)),
            _const_spec((1, CM_GROUP_DIM)),
            _const_spec((CM_GROUPS, CM_CHUNK, CM_CHUNK)),
            _const_spec((CM_CHUNK, CM_WIDTH)),
        ],
        out_specs=[
            pl.BlockSpec((hq, tm), col),
            pl.BlockSpec((tm, hq), row),
            pl.BlockSpec((hq, tm), col),
            pl.BlockSpec((tm, CM_WIDTH), row),
        ],
        out_shape=[
            jax.ShapeDtypeStruct((hq, n), BF16),
            jax.ShapeDtypeStruct((n, hq), BF16),
            jax.ShapeDtypeStruct((hq, n), BF16),
            jax.ShapeDtypeStruct((n, CM_WIDTH), BF16),
        ],
        compiler_params=_cparams(("arbitrary",)),
        name="even_in",
    )(x, g1, sh, sc, ta, tb, cos_t, sin_t, w["wall"], w["qn"], w["wuqt"], w["kvn"], w["wuk"], w["wuvt"], w["cmn"],
      w["ws"], w["bias"])


def _attn_kernel(qt_ref, k_ref, vt_ref, o_ref, s_scr, p_scr, acc_scr, *, tq, tk, nk):
    def scores(t, h, slot):
        r0 = pl.multiple_of(t * tk, LANES)
        ks = k_ref[pl.ds(r0, tk), h * HEAD_SLOT:(h + 1) * HEAD_SLOT]
        s_scr[h, slot] = jnp.dot(ks, qt_ref[h * HEAD_SLOT:(h + 1) * HEAD_SLOT, :], preferred_element_type=F32)

    def softmax(h, slot, m):
        s = s_scr[h, slot]
        m_new = jnp.maximum(m, jnp.max(s, axis=0, keepdims=True))
        alpha = jnp.exp2(m - m_new)
        p_scr[h, slot] = jnp.exp2(s - m_new).astype(BF16)
        return m_new, alpha

    def weighted(t, h, slot, alpha):
        r0 = pl.multiple_of(t * tk, LANES)
        vs = vt_ref[h * HEAD_SLOT:(h + 1) * HEAD_SLOT, pl.ds(r0, tk)]
        acc_scr[h] = alpha * acc_scr[h] + jnp.dot(vs, p_scr[h, slot], preferred_element_type=F32)

    carry = []
    for h in range(2):
        acc_scr[h] = jnp.zeros((HEAD_SLOT, tq), F32)
        scores(0, h, 0)
        m, a = softmax(h, 0, jnp.full((1, tq), -jnp.inf, F32))
        scores(1, h, 1)
        carry.append((m, a))

    def body(jj, carry):
        t = 2 * jj
        out = []
        for h in range(2):
            m, a = carry[h]
            scores(t + 2, h, 0)
            m, a1 = softmax(h, 1, m)
            weighted(t, h, 0, a)
            scores(t + 3, h, 1)
            m, a2 = softmax(h, 0, m)
            weighted(t + 1, h, 1, a1)
            out.append((m, a2))
        return tuple(out)

    carry = lax.fori_loop(0, nk // 2 - 1, body, tuple(carry))
    outs = []
    for h in range(2):
        m, a = carry[h]
        m, a1 = softmax(h, 1, m)
        weighted(nk - 2, h, 0, a)
        weighted(nk - 1, h, 1, a1)
        acc = acc_scr[h]
        outs.append(acc[0:MLA_V, :] * (1.0 / acc[MLA_V:MLA_V + 1, :]))
    o_ref[...] = jnp.concatenate(outs, axis=0).T.astype(BF16)


def _key_tile(t_len):
    tk = (min(t_len // 2, MAX_KEY_TILE) // LANES) * LANES
    while t_len % tk or (t_len // tk) % 2:
        tk -= LANES
    return tk


def _attn_call(qt, k, vt, *, batch, tq):
    n = qt.shape[1]
    t_len = k.shape[0] // batch
    nq = n // batch // tq
    tk = _key_tile(t_len)
    return pl.pallas_call(
        functools.partial(_attn_kernel, tq=tq, tk=tk, nk=t_len // tk),
        grid=(batch, MLA_HEADS // 2, nq),
        in_specs=[
            pl.BlockSpec((2 * HEAD_SLOT, tq), lambda b, hp, i: (hp, b * nq + i)),
            pl.BlockSpec((t_len, 2 * HEAD_SLOT), lambda b, hp, i: (b, hp)),
            pl.BlockSpec((2 * HEAD_SLOT, t_len), lambda b, hp, i: (hp, b)),
        ],
        out_specs=pl.BlockSpec((tq, 2 * MLA_V), lambda b, hp, i: (b * nq + i, hp)),
        out_shape=jax.ShapeDtypeStruct((n, MLA_HEADS * MLA_V), BF16),
        scratch_shapes=[
            pltpu.VMEM((2, 2, tk, tq), F32),
            pltpu.VMEM((2, 2, tk, tq), BF16),
            pltpu.VMEM((2, HEAD_SLOT, tq), F32),
        ],
        compiler_params=_cparams(("arbitrary", "arbitrary", "arbitrary")),
        name="mla_attn",
    )(qt, k, vt)


def _post_kernel(x_ref, a_ref, b_ref, gm_ref, g2_ref, sh_ref, sc_ref, gf_ref, wo_ref, w1_ref, w2_ref, fg_ref,
                 o_ref, *, final):
    half = wo_ref.shape[0] // 2
    y = (jnp.dot(a_ref[...], wo_ref[:half, :], preferred_element_type=F32)
         + jnp.dot(b_ref[...], wo_ref[half:, :], preferred_element_type=F32))
    x1 = x_ref[...] + gm_ref[0] * y
    h2 = (_rms(x1, g2_ref[...]) * (1.0 + sc_ref[0]) + sh_ref[0]).astype(BF16)
    acc = jnp.zeros_like(x1)
    fc = 1024
    for c in range(w1_ref.shape[1] // fc):
        hc = jnp.dot(h2, w1_ref[:, c * fc:(c + 1) * fc], preferred_element_type=F32)
        hc = jnp.square(jnp.maximum(hc, 0.0)).astype(BF16)
        acc = acc + jnp.dot(hc, w2_ref[c * fc:(c + 1) * fc, :], preferred_element_type=F32)
    x2 = x1 + gf_ref[0] * acc
    if final:
        x2 = _rms(x2, fg_ref[...])
    o_ref[...] = x2


def _post_call(x, mix_a, mix_b, col_a, col_b, gm, g2, sh, sc, gf, wo, w1, w2, fg, *, seq, tm, final):
    n, d = x.shape
    tpb = seq // tm
    row = lambda i: (i, 0)
    bat = lambda i: (i // tpb, 0, 0)
    half = wo.shape[0] // 2
    return pl.pallas_call(
        functools.partial(_post_kernel, final=final),
        grid=(n // tm,),
        in_specs=[
            pl.BlockSpec((tm, d), row),
            pl.BlockSpec((tm, half), lambda i: (i, col_a)),
            pl.BlockSpec((tm, half), lambda i: (i, col_b)),
            pl.BlockSpec((1, 1, d), bat),
            _const_spec((1, d)),
            pl.BlockSpec((1, 1, d), bat),
            pl.BlockSpec((1, 1, d), bat),
            pl.BlockSpec((1, 1, d), bat),
            _const_spec(wo.shape),
            _const_spec(w1.shape),
            _const_spec(w2.shape),
            _const_spec((1, d)),
        ],
        out_specs=pl.BlockSpec((tm, d), row),
        out_shape=jax.ShapeDtypeStruct((n, d), F32),
        compiler_params=_cparams(("arbitrary",)),
        name="post_mlp",
    )(x, mix_a, mix_b, gm, g2, sh, sc, gf, wo, w1, w2, fg)


def _log_decay(g):
    return (jnp.minimum(g, 0.0) - jnp.log1p(jnp.exp(-jnp.abs(g)))) * (1.0 / GLA_TAU)


def _odd_in_kernel(x_ref, g1_ref, sh_ref, sc_ref, wall_ref, wgf_ref, bgf_ref, wgb_ref, bgb_ref,
                   k_ref, v_ref, q_ref, r_ref, df_ref, db_ref):
    h = _rms(x_ref[...], g1_ref[...]) * (1.0 + sc_ref[0]) + sh_ref[0]
    p = _bdot(h, wall_ref[...])
    k_ref[...] = p[:, OA_K:OA_V]
    v_ref[...] = p[:, OA_V:OA_Q].astype(BF16)
    q_ref[...] = p[:, OA_Q:OA_R] * (GLA_DK ** -0.5)
    r = p[:, OA_R:OA_Z]
    r_ref[...] = r * jax.nn.sigmoid(r)
    z = p[:, OA_Z:OA_END].astype(BF16)
    df_ref[...] = _log_decay(jnp.dot(z, wgf_ref[...], preferred_element_type=F32) + bgf_ref[...])
    db_ref[...] = _log_decay(jnp.dot(z, wgb_ref[...], preferred_element_type=F32) + bgb_ref[...])


def _odd_in_call(x, g1, sh, sc, w, *, seq, tm):
    n, d = x.shape
    tpb = seq // tm
    row = lambda i: (i, 0)
    bat = lambda i: (i // tpb, 0, 0)
    kd = GLA_HEADS * GLA_DK
    vd = GLA_HEADS * GLA_DV
    return pl.pallas_call(
        _odd_in_kernel,
        grid=(n // tm,),
        in_specs=[
            pl.BlockSpec((tm, d), row),
            _const_spec((1, d)),
            pl.BlockSpec((1, 1, d), bat),
            pl.BlockSpec((1, 1, d), bat),
            _const_spec((d, OA_END)),
            _const_spec((LANES, kd)),
            _const_spec((1, kd)),
            _const_spec((LANES, kd)),
            _const_spec((1, kd)),
        ],
        out_specs=[
            pl.BlockSpec((tm, kd), row),
            pl.BlockSpec((tm, vd), row),
            pl.BlockSpec((tm, kd), row),
            pl.BlockSpec((tm, vd), row),
            pl.BlockSpec((tm, kd), row),
            pl.BlockSpec((tm, kd), row),
        ],
        out_shape=[
            jax.ShapeDtypeStruct((n, kd), F32),
            jax.ShapeDtypeStruct((n, vd), BF16),
            jax.ShapeDtypeStruct((n, kd), F32),
            jax.ShapeDtypeStruct((n, vd), F32),
            jax.ShapeDtypeStruct((n, kd), F32),
            jax.ShapeDtypeStruct((n, kd), F32),
        ],
        compiler_params=_cparams(("arbitrary",)),
        name="odd_in",
    )(x, g1, sh, sc, w["wall"], w["wgf"], w["bgf"], w["wgb"], w["bgb"])


def _gla_kernel(*refs, reverse, nchunk, combine):
    if combine:
        q_ref, k_ref, g_ref, v_ref, s0_ref, of_ref, r_ref, on_ref, o_ref, sfin_ref, s_scr = refs
    else:
        q_ref, k_ref, g_ref, v_ref, s0_ref, o_ref, sfin_ref, s_scr = refs
    i = pl.program_id(2)

    @pl.when(i == 0)
    def _():
        s_scr[...] = s0_ref[...]

    ch = GLA_CHUNK
    rr = lax.broadcasted_iota(jnp.int32, (ch, ch), 0)
    cc = lax.broadcasted_iota(jnp.int32, (ch, ch), 1)
    mask = (rr <= cc) if reverse else (rr >= cc)
    tri = mask.astype(F32)
    order = range(nchunk - 1, -1, -1) if reverse else range(nchunk)
    for c in order:
        sl = slice(c * ch, (c + 1) * ch)
        g = g_ref[sl, :]
        q = q_ref[sl, :]
        k = k_ref[sl, :]
        v = v_ref[sl, :]
        bc = jnp.dot(tri, g, preferred_element_type=F32, precision=lax.Precision.HIGHEST)
        tot = bc[0:1, :] if reverse else bc[ch - 1:ch, :]
        q_t = (q * jnp.exp(bc)).astype(BF16)
        k_t = (k * jnp.exp(-bc)).astype(BF16)
        k_end = k * jnp.exp(tot - bc)
        a = lax.dot_general(q_t, k_t, (((1,), (1,)), ((), ())), preferred_element_type=F32)
        a = jnp.where(mask, a, 0.0).astype(BF16)
        s = s_scr[...]
        o = (jnp.dot(a, v, preferred_element_type=F32)
             + jnp.dot(q_t, s.astype(BF16), preferred_element_type=F32))
        dec = jnp.exp(jnp.sum(g.T, axis=1, keepdims=True))
        s_scr[...] = dec * s + jnp.dot(k_end.T.astype(BF16), v, preferred_element_type=F32)
        if combine:
            o = _rms(o + of_ref[sl, :], on_ref[...]) * r_ref[sl, :]
            o_ref[sl, :] = o.astype(o_ref.dtype)
        else:
            o_ref[sl, :] = o

    @pl.when(i == pl.num_programs(2) - 1)
    def _():
        sfin_ref[...] = s_scr[...]


def _gla_call(q, k, g, v, s0, *, reverse, tb, o_fwd=None, r=None, o_norm=None):
    b, l, _ = q.shape
    nb = l // tb
    combine = o_fwd is not None
    blk = (lambda bi, h, i: (bi, nb - 1 - i, h)) if reverse else (lambda bi, h, i: (bi, i, h))
    st = lambda bi, h, i: (bi, h, 0, 0)
    kspec = pl.BlockSpec((None, tb, GLA_DK), blk)
    vspec = pl.BlockSpec((None, tb, GLA_DV), blk)
    sspec = pl.BlockSpec((None, None, GLA_DK, GLA_DV), st)
    in_specs = [kspec, kspec, kspec, vspec, sspec]
    args = [q, k, g, v, s0]
    if combine:
        in_specs += [vspec, vspec, pl.BlockSpec((1, GLA_DV), lambda bi, h, i: (0, 0))]
        args += [o_fwd, r, o_norm]
    return pl.pallas_call(
        functools.partial(_gla_kernel, reverse=reverse, nchunk=tb // GLA_CHUNK, combine=combine),
        grid=(b, GLA_HEADS, nb),
        in_specs=in_specs,
        out_specs=[vspec, sspec],
        out_shape=[
            jax.ShapeDtypeStruct((b, l, GLA_HEADS * GLA_DV), BF16 if combine else F32),
            jax.ShapeDtypeStruct((b, GLA_HEADS, GLA_DK, GLA_DV), F32),
        ],
        scratch_shapes=[pltpu.VMEM((GLA_DK, GLA_DV), F32)],
        compiler_params=_cparams(("arbitrary", "arbitrary", "arbitrary")),
        name="gla_bwd" if reverse else "gla_fwd",
    )(*args)


def _rot_cols(w):
    q = MLA_ROPE // 4
    return jnp.concatenate([-w[..., q:2 * q], w[..., 0:q], -w[..., 3 * q:4 * q], w[..., 2 * q:3 * q]], axis=-1)


def _prep_even(w_in, q_norm, w_uq, kv_norm, w_ukv, cm_norm, cm_ws, cm_bs):
    d = w_in.shape[0]
    e_q, e_kv = MLA_Q_RANK, MLA_Q_RANK + MLA_KV_RANK
    e_r = e_kv + MLA_ROPE
    e_u = e_r + CM_WIDTH
    wkr = w_in[:, e_kv:e_r]
    wall = jnp.concatenate([w_in[:, :e_kv], wkr, _rot_cols(wkr), jnp.zeros((d, LANES - 2 * MLA_ROPE), F32),
                            w_in[:, e_r:e_u], w_in[:, e_u:]], axis=1).astype(BF16)
    uq = w_uq.reshape(MLA_Q_RANK, MLA_HEADS, MLA_NOPE + MLA_ROPE)
    uq_r = uq[..., MLA_NOPE:]
    wuqt = jnp.concatenate([uq, _rot_cols(uq_r)], axis=-1).reshape(MLA_Q_RANK, MLA_HEADS * HEAD_SLOT).T.astype(BF16)
    ukv = w_ukv.reshape(MLA_KV_RANK, MLA_HEADS, MLA_NOPE + MLA_V)
    wuk = jnp.concatenate([ukv[..., :MLA_NOPE], jnp.zeros((MLA_KV_RANK, MLA_HEADS, HEAD_SLOT - MLA_NOPE), F32)],
                          axis=-1).reshape(MLA_KV_RANK, MLA_HEADS * HEAD_SLOT).astype(BF16)
    wuvt = ukv[..., MLA_NOPE:].reshape(MLA_KV_RANK, MLA_HEADS * MLA_V).T.astype(BF16)
    bias = jnp.repeat(cm_bs.T, CM_GROUP_DIM, axis=1)
    return dict(wall=wall, qn=q_norm[None], wuqt=wuqt, kvn=kv_norm[None], wuk=wuk, wuvt=wuvt, cmn=cm_norm[None],
                ws=cm_ws.astype(BF16), bias=bias)


def _prep_odd(w_in, w_gf, b_gf, w_gb, b_gb):
    d = w_in.shape[0]
    o_k = GLA_HEADS * GLA_DK
    o_v = o_k + GLA_HEADS * GLA_DV
    o_zb = o_v + 2 * GLA_GATE_RANK
    o_q = o_zb + GLA_HEADS * GLA_DK
    wall = jnp.concatenate([w_in[:, :o_v], w_in[:, o_zb:o_q], w_in[:, o_q:], w_in[:, o_v:o_zb],
                            jnp.zeros((d, LANES - 2 * GLA_GATE_RANK), F32)], axis=1).astype(BF16)
    zr = GLA_GATE_RANK
    wgf = jnp.zeros((LANES, o_k), F32).at[:zr].set(w_gf).astype(BF16)
    wgb = jnp.zeros((LANES, o_k), F32).at[zr:2 * zr].set(w_gb).astype(BF16)
    return dict(wall=wall, wgf=wgf, bgf=b_gf[None], wgb=wgb, bgb=b_gb[None])


def _rope_tables(length):
    rows = length // GRID_W
    r = jnp.repeat(jnp.arange(rows, dtype=F32), GRID_W)
    col = jnp.tile(jnp.arange(GRID_W, dtype=F32), rows)
    half = MLA_ROPE // 2
    inv = ROPE_BASE ** (-jnp.arange(0, half, 2, dtype=F32) / half)
    ang_r = r[:, None] * inv
    ang_c = col[:, None] * inv
    ang = jnp.concatenate([ang_r, ang_r, ang_c, ang_c], axis=-1)
    one = jnp.ones((length, MLA_NOPE), F32)
    pad = jnp.zeros((length, HEAD_SLOT - MLA_NOPE - MLA_ROPE), F32)
    ta = jnp.concatenate([one, jnp.cos(ang), pad], axis=1)
    tb = jnp.concatenate([0.0 * one, jnp.sin(ang), pad], axis=1)
    return ta, tb, jnp.cos(ang).T, jnp.sin(ang).T


def _flat_tables(length):
    ta = jnp.concatenate([jnp.ones((length, MLA_NOPE + MLA_ROPE), F32),
                          jnp.zeros((length, HEAD_SLOT - MLA_NOPE - MLA_ROPE), F32)], axis=1)
    return ta, jnp.zeros_like(ta), jnp.ones((MLA_ROPE, length), F32), jnp.zeros((MLA_ROPE, length), F32)


def _row_tile(seq, want):
    t = min(seq, want)
    while seq % t:
        t //= 2
    return t


def kernel(x, c, ctx, c_ctx, ada_w, ada_b, norm1_g, norm2_g, mlp_w1, mlp_w2, ev_w_in, ev_q_norm, ev_w_uq, ev_kv_norm,
           ev_w_ukv, ev_cm_norm, ev_cm_ws, ev_cm_bs, ev_w_out, od_w_in, od_w_gf, od_b_gf, od_w_gb, od_b_gb, od_o_norm,
           od_w_out, final_g):
    batch, seq, d = x.shape
    lc = ctx.shape[1]
    depth = ada_w.shape[0]
    tm_l = _row_tile(seq, 512)
    tm_c = _row_tile(lc, 256)
    tq_l = _row_tile(seq, 256)
    tq_c = _row_tile(lc, 256)
    tb_l = _row_tile(seq, 256)
    tb_c = _row_tile(lc, 256)

    cvec = jnp.concatenate([c, c_ctx[None], jnp.zeros((8 - batch - 1, d), F32)], axis=0)
    mods = _ada_call(cvec, ada_w, ada_b).reshape(depth, 8, 6, d)

    tabs_l = _rope_tables(seq)
    tabs_c = _flat_tables(lc)
    hq = MLA_HEADS * HEAD_SLOT

    xl = x.reshape(batch * seq, d)
    xc = ctx.reshape(batch * lc, d)
    for i in range(depth):
        need_ctx = i < depth - 1
        j = i // 2
        ml = [mods[i, :batch, t][:, None, :] for t in range(6)]
        mc = [mods[i, batch:batch + 1, t][:, None, :] for t in range(6)]
        g1 = norm1_g[i][None]
        g2 = norm2_g[i][None]
        w1 = mlp_w1[i].astype(BF16)
        w2 = mlp_w2[i].astype(BF16)
        final = i == depth - 1
        if i % 2 == 0:
            w = _prep_even(ev_w_in[j], ev_q_norm[j], ev_w_uq[j], ev_kv_norm[j], ev_w_ukv[j], ev_cm_norm[j],
                           ev_cm_ws[j], ev_cm_bs[j])
            wo = ev_w_out[j].astype(BF16)
            qtl, kl, vtl, mll = _even_in_call(xl, g1, ml[0], ml[1], tabs_l, w, seq=seq, tm=tm_l)
            qtc, kc, vtc, mlc = _even_in_call(xc, g1, mc[0], mc[1], tabs_c, w, seq=batch * lc, tm=tm_c)
            k_all = jnp.concatenate([kl.reshape(batch, seq, hq), kc.reshape(batch, lc, hq)], axis=1)
            vt_all = jnp.concatenate([vtl.reshape(hq, batch, seq), vtc.reshape(hq, batch, lc)], axis=2)
            al = _attn_call(qtl, k_all.reshape(batch * (seq + lc), hq), vt_all.reshape(hq, batch * (seq + lc)),
                            batch=batch, tq=tq_l)
            xl = _post_call(xl, al, mll, 0, 0, ml[2], g2, ml[3], ml[4], ml[5], wo, w1, w2, final_g[None],
                            seq=seq, tm=tm_l, final=final)
            if need_ctx:
                ac = _attn_call(qtc, kc, vtc, batch=batch, tq=tq_c)
                xc = _post_call(xc, ac, mlc, 0, 0, mc[2], g2, mc[3], mc[4], mc[5], wo, w1, w2, final_g[None],
                                seq=batch * lc, tm=tm_c, final=False)
        else:
            w = _prep_odd(od_w_in[j], od_w_gf[j], od_b_gf[j], od_w_gb[j], od_b_gb[j])
            wo = od_w_out[j].astype(BF16)
            on = od_o_norm[j][None]
            kc, vc, qc, rc, dfc, dbc = _odd_in_call(xc, g1, mc[0], mc[1], w, seq=batch * lc, tm=tm_c)
            kl, vl, ql, rl, dfl, dbl = _odd_in_call(xl, g1, ml[0], ml[1], w, seq=seq, tm=tm_l)
            r3 = lambda t, n: t.reshape(batch, n, t.shape[-1])
            s0 = jnp.zeros((batch, GLA_HEADS, GLA_DK, GLA_DV), F32)
            ocf, s_f = _gla_call(r3(qc, lc), r3(kc, lc), r3(dfc, lc), r3(vc, lc), s0, reverse=False, tb=tb_c)
            mixc, s_b = _gla_call(r3(qc, lc), r3(kc, lc), r3(dbc, lc), r3(vc, lc), s0, reverse=True, tb=tb_c,
                                  o_fwd=ocf, r=r3(rc, lc), o_norm=on)
            olf, _ = _gla_call(r3(ql, seq), r3(kl, seq), r3(dfl, seq), r3(vl, seq), s_f, reverse=False, tb=tb_l)
            mixl, _ = _gla_call(r3(ql, seq), r3(kl, seq), r3(dbl, seq), r3(vl, seq), s_b, reverse=True, tb=tb_l,
                                o_fwd=olf, r=r3(rl, seq), o_norm=on)
            mixl = mixl.reshape(batch * seq, -1)
            xl = _post_call(xl, mixl, mixl, 0, 1, ml[2], g2, ml[3], ml[4], ml[5], wo, w1, w2, final_g[None],
                            seq=seq, tm=tm_l, final=final)
            if need_ctx:
                mixc = mixc.reshape(batch * lc, -1)
                xc = _post_call(xc, mixc, mixc, 0, 1, mc[2], g2, mc[3], mc[4], mc[5], wo, w1, w2, final_g[None],
                                seq=batch * lc, tm=tm_c, final=False)
    return xl.reshape(batch, seq, d)
```

```python
import functools

import jax
import jax.numpy as jnp
from jax import lax
from jax.experimental import pallas as pl
from jax.experimental.pallas import tpu as pltpu

F32 = jnp.float32
BF16 = jnp.bfloat16

D_MODEL = 1024
DEPTH = 4
GRID_W = 64
EPS = 1e-6
MLA_HEADS = 8
MLA_NOPE = 64
MLA_ROPE = 32
MLA_V = 64
MLA_Q_RANK = 384
MLA_KV_RANK = 256
MLA_SCALE = (MLA_NOPE + MLA_ROPE) ** -0.5
ROPE_BASE = 10000.0
CM_CHUNK = 128
CM_GROUPS = 4
CM_GROUP_DIM = 128
CM_WIDTH = CM_GROUPS * CM_GROUP_DIM
GLA_HEADS = 4
GLA_DK = 128
GLA_DV = 256
GLA_GATE_RANK = 16
GLA_TAU = 16.0
GLA_CHUNK = 64
GLA_PAIR = 2 * GLA_CHUNK
D_FF = 4 * D_MODEL

LANES = 128
HEAD_SLOT = LANES
VMEM_LIMIT = 56 * 1024 * 1024
MAX_KEY_TILE = 640
Q_PRESCALE = MLA_SCALE * 1.4426950408889634

EA_Q = 0
EA_KV = EA_Q + MLA_Q_RANK
EA_KR = EA_KV + MLA_KV_RANK
EA_U = EA_KR + LANES
EA_V = EA_U + CM_WIDTH
EA_END = EA_V + CM_WIDTH
OA_K = 0
OA_V = OA_K + GLA_HEADS * GLA_DK
OA_Q = OA_V + GLA_HEADS * GLA_DV
OA_R = OA_Q + GLA_HEADS * GLA_DK
OA_Z = OA_R + GLA_HEADS * GLA_DV
OA_END = OA_Z + LANES


def _cparams(sem):
    return pltpu.CompilerParams(dimension_semantics=sem, vmem_limit_bytes=VMEM_LIMIT)


def _rms(x, g):
    return x * lax.rsqrt(jnp.mean(x * x, axis=-1, keepdims=True) + EPS) * g


def _bdot(a, b):
    return jnp.dot(a.astype(BF16), b.astype(BF16), preferred_element_type=F32)


def _const_spec(shape):
    nd = len(shape)
    return pl.BlockSpec(shape, lambda *_: (0,) * nd, pipeline_mode=pl.Buffered(1))


def _ada_kernel(c_ref, w_ref, b_ref, o_ref):
    s = c_ref[...]
    s = s * jax.nn.sigmoid(s)
    o_ref[0] = _bdot(s, w_ref[0]) + b_ref[0]


def _ada_call(cvec, ada_w, ada_b):
    depth, d, n = ada_w.shape
    tn = 1536
    return pl.pallas_call(
        _ada_kernel,
        grid=(depth, n // tn),
        in_specs=[
            pl.BlockSpec((8, d), lambda l, j: (0, 0)),
            pl.BlockSpec((1, d, tn), lambda l, j: (l, 0, j)),
            pl.BlockSpec((1, 1, tn), lambda l, j: (l, 0, j)),
        ],
        out_specs=pl.BlockSpec((1, 8, tn), lambda l, j: (l, 0, j)),
        out_shape=jax.ShapeDtypeStruct((depth, 8, n), F32),
        compiler_params=_cparams(("arbitrary", "arbitrary")),
        name="ada_mod",
    )(cvec, ada_w, ada_b.reshape(depth, 1, n))


def _rope_slot(t, a, b):
    return t * a + pltpu.roll(t, HEAD_SLOT - MLA_ROPE, 1) * b


def _dot_nt(a, b):
    return lax.dot_general(a, b, (((1,), (1,)), ((), ())), preferred_element_type=F32)


def _even_in_kernel(*refs, tm, n_in):
    (x_ref, g1_ref, sh_ref, sc_ref, ta_ref, tb_ref, ct_ref, st_ref, wall_ref, qn_ref, wuqt_ref, kvn_ref, wuk_ref,
     wuvt_ref, cmn_ref, ws_ref, bias_ref) = refs[:17]
    qt_ref, k_ref, vt_ref, ml_ref = refs[n_in:]
    h = _rms(x_ref[...], g1_ref[...]) * (1.0 + sc_ref[0]) + sh_ref[0]
    p = _bdot(h, wall_ref[...])

    cq = _rms(p[:, EA_Q:EA_KV], qn_ref[...]).astype(BF16)
    qt = _dot_nt(wuqt_ref[...], cq)
    cos_t = ct_ref[...]
    sin_t = st_ref[...]
    r0, r1 = MLA_NOPE, MLA_NOPE + MLA_ROPE
    for hd in range(MLA_HEADS):
        b = hd * HEAD_SLOT
        qt_ref[b:b + r0, :] = (qt[b:b + r0, :] * Q_PRESCALE).astype(BF16)
        roped = qt[b + r0:b + r1, :] * cos_t + qt[b + r1:b + HEAD_SLOT, :] * sin_t
        qt_ref[b + r0:b + r1, :] = (roped * Q_PRESCALE).astype(BF16)
        qt_ref[b + r1:b + HEAD_SLOT, :] = jnp.zeros((HEAD_SLOT - r1, tm), BF16)

    ckv = _rms(p[:, EA_KV:EA_KR], kvn_ref[...]).astype(BF16)
    kn = jnp.dot(ckv, wuk_ref[...], preferred_element_type=F32)
    kr = _rope_slot(pltpu.roll(p[:, EA_KR:EA_U], MLA_NOPE, 1), ta_ref[...], tb_ref[...])
    for hd in range(MLA_HEADS):
        sl = slice(hd * HEAD_SLOT, (hd + 1) * HEAD_SLOT)
        k_ref[:, sl] = (kn[:, sl] + kr).astype(BF16)
    vt = _dot_nt(wuvt_ref[...], ckv)
    for hd in range(MLA_HEADS):
        b = hd * HEAD_SLOT
        vt_ref[b:b + MLA_V, :] = vt[hd * MLA_V:(hd + 1) * MLA_V, :].astype(BF16)
        vt_ref[b + MLA_V:b + HEAD_SLOT, :] = jnp.ones((HEAD_SLOT - MLA_V, tm), BF16)

    u = jax.nn.gelu(p[:, EA_U:EA_V])
    vv = jax.nn.gelu(p[:, EA_V:EA_END])
    cmn = cmn_ref[...]
    for g in range(CM_GROUPS):
        gl = slice(g * CM_GROUP_DIM, (g + 1) * CM_GROUP_DIM)
        vn = _rms(vv[:, gl], cmn).astype(BF16)
        w = ws_ref[g]
        for c in range(tm // CM_CHUNK):
            rs = slice(c * CM_CHUNK, (c + 1) * CM_CHUNK)
            y = jnp.dot(w, vn[rs], preferred_element_type=F32) + bias_ref[:, gl]
            ml_ref[rs, gl] = (u[rs, gl] * y).astype(BF16)


def _even_in_call(x, g1, sh, sc, tabs, w, *, seq, tm, rows, kv_rows, kv_off, kv_bufs=None):
    ta, tb, cos_t, sin_t = tabs
    n, d = x.shape
    batch = n // rows
    tpb = seq // tm
    tpt = ta.shape[0] // tm
    tpr = rows // tm
    kv_blk = lambda i: (i // tpr) * (kv_rows // tm) + kv_off // tm + i % tpr
    row = lambda i: (i, 0)
    col = lambda i: (0, i)
    bat = lambda i: (i // tpb, 0, 0)
    tab = lambda i: (i % tpt, 0)
    tab_t = lambda i: (0, i % tpt)
    hq = MLA_HEADS * HEAD_SLOT
    hv = MLA_HEADS * MLA_V
    in_specs = [
        pl.BlockSpec((tm, d), row),
        _const_spec((1, d)),
        pl.BlockSpec((1, 1, d), bat),
        pl.BlockSpec((1, 1, d), bat),
        pl.BlockSpec((tm, HEAD_SLOT), tab),
        pl.BlockSpec((tm, HEAD_SLOT), tab),
        pl.BlockSpec((MLA_ROPE, tm), tab_t),
        pl.BlockSpec((MLA_ROPE, tm), tab_t),
        _const_spec((d, EA_END)),
        _const_spec((1, MLA_Q_RANK)),
        _const_spec((hq, MLA_Q_RANK)),
        _const_spec((1, MLA_KV_RANK)),
        _const_spec((MLA_KV_RANK, hq)),
        _const_spec((hv, MLA_KV_RANK)),
        _const_spec((1, CM_GROUP_DIM)),
        _const_spec((CM_GROUPS, CM_CHUNK, CM_CHUNK)),
        _const_spec((CM_CHUNK, CM_WIDTH)),
    ]
    args = [x, g1, sh, sc, ta, tb, cos_t, sin_t, w["wall"], w["qn"], w["wuqt"], w["kvn"], w["wuk"], w["wuvt"],
            w["cmn"], w["ws"], w["bias"]]
    aliases = {}
    if kv_bufs is not None:
        aliases = {len(args): 1, len(args) + 1: 2}
        in_specs += [pl.BlockSpec(memory_space=pl.ANY), pl.BlockSpec(memory_space=pl.ANY)]
        args += list(kv_bufs)
    return pl.pallas_call(
        functools.partial(_even_in_kernel, tm=tm, n_in=len(args)),
        grid=(n // tm,),
        in_specs=in_specs,
        out_specs=[
            pl.BlockSpec((hq, tm), col),
            pl.BlockSpec((tm, hq), lambda i: (kv_blk(i), 0)),
            pl.BlockSpec((hq, tm), lambda i: (0, kv_blk(i))),
            pl.BlockSpec((tm, CM_WIDTH), row),
        ],
        out_shape=[
            jax.ShapeDtypeStruct((hq, n), BF16),
            jax.ShapeDtypeStruct((batch * kv_rows, hq), BF16),
            jax.ShapeDtypeStruct((hq, batch * kv_rows), BF16),
            jax.ShapeDtypeStruct((n, CM_WIDTH), BF16),
        ],
        input_output_aliases=aliases,
        compiler_params=_cparams(("arbitrary",)),
        name="even_in",
    )(*args)


def _attn_kernel(qt_ref, k_ref, vt_ref, o_ref, s_scr, p_scr, acc_scr, *, tq, tk, nk):
    def scores(t, h, slot):
        r0 = pl.multiple_of(t * tk, LANES)
        ks = k_ref[pl.ds(r0, tk), h * HEAD_SLOT:(h + 1) * HEAD_SLOT]
        s_scr[h, slot] = jnp.dot(ks, qt_ref[h * HEAD_SLOT:(h + 1) * HEAD_SLOT, :], preferred_element_type=F32)

    def softmax(h, slot, m):
        s = s_scr[h, slot]
        m_new = jnp.maximum(m, jnp.max(s, axis=0, keepdims=True))
        alpha = jnp.exp2(m - m_new)
        p_scr[h, slot] = jnp.exp2(s - m_new).astype(BF16)
        return m_new, alpha

    def weighted(t, h, slot, alpha):
        r0 = pl.multiple_of(t * tk, LANES)
        vs = vt_ref[h * HEAD_SLOT:(h + 1) * HEAD_SLOT, pl.ds(r0, tk)]
        acc_scr[h] = alpha * acc_scr[h] + jnp.dot(vs, p_scr[h, slot], preferred_element_type=F32)

    carry = []
    for h in range(2):
        acc_scr[h] = jnp.zeros((HEAD_SLOT, tq), F32)
        scores(0, h, 0)
        m, a = softmax(h, 0, jnp.full((1, tq), -jnp.inf, F32))
        scores(1, h, 1)
        carry.append((m, a))

    def body(jj, carry):
        t = 2 * jj
        out = []
        for h in range(2):
            m, a = carry[h]
            scores(t + 2, h, 0)
            m, a1 = softmax(h, 1, m)
            weighted(t, h, 0, a)
            scores(t + 3, h, 1)
            m, a2 = softmax(h, 0, m)
            weighted(t + 1, h, 1, a1)
            out.append((m, a2))
        return tuple(out)

    carry = lax.fori_loop(0, nk // 2 - 1, body, tuple(carry))
    outs = []
    for h in range(2):
        m, a = carry[h]
        m, a1 = softmax(h, 1, m)
        weighted(nk - 2, h, 0, a)
        weighted(nk - 1, h, 1, a1)
        acc = acc_scr[h]
        outs.append(acc[0:MLA_V, :] * (1.0 / acc[MLA_V:MLA_V + 1, :]))
    o_ref[...] = jnp.concatenate(outs, axis=0).T.astype(BF16)


def _key_tile(t_len):
    tk = (min(t_len // 2, MAX_KEY_TILE) // LANES) * LANES
    while t_len % tk or (t_len // tk) % 2:
        tk -= LANES
    return tk


def _attn_call(qt, k, vt, *, batch, tq, t_len, t_off):
    n = qt.shape[1]
    kb = k.shape[0] // batch // t_len
    ko = t_off // t_len
    nq = n // batch // tq
    tk = _key_tile(t_len)
    return pl.pallas_call(
        functools.partial(_attn_kernel, tq=tq, tk=tk, nk=t_len // tk),
        grid=(batch, MLA_HEADS // 2, nq),
        in_specs=[
            pl.BlockSpec((2 * HEAD_SLOT, tq), lambda b, hp, i: (hp, b * nq + i)),
            pl.BlockSpec((t_len, 2 * HEAD_SLOT), lambda b, hp, i: (b * kb + ko, hp)),
            pl.BlockSpec((2 * HEAD_SLOT, t_len), lambda b, hp, i: (hp, b * kb + ko)),
        ],
        out_specs=pl.BlockSpec((tq, 2 * MLA_V), lambda b, hp, i: (b * nq + i, hp)),
        out_shape=jax.ShapeDtypeStruct((n, MLA_HEADS * MLA_V), BF16),
        scratch_shapes=[
            pltpu.VMEM((2, 2, tk, tq), F32),
            pltpu.VMEM((2, 2, tk, tq), BF16),
            pltpu.VMEM((2, HEAD_SLOT, tq), F32),
        ],
        compiler_params=_cparams(("arbitrary", "arbitrary", "arbitrary")),
        name="mla_attn",
    )(qt, k, vt)


def _post_kernel(x_ref, a_ref, b_ref, gm_ref, g2_ref, sh_ref, sc_ref, gf_ref, wo_ref, w1_ref, w2_ref, fg_ref,
                 o_ref, *, final):
    half = wo_ref.shape[0] // 2
    y = (jnp.dot(a_ref[...], wo_ref[:half, :], preferred_element_type=F32)
         + jnp.dot(b_ref[...], wo_ref[half:, :], preferred_element_type=F32))
    x1 = x_ref[...] + gm_ref[0] * y
    h2 = (_rms(x1, g2_ref[...]) * (1.0 + sc_ref[0]) + sh_ref[0]).astype(BF16)
    acc = jnp.zeros_like(x1)
    fc = 1024
    for c in range(w1_ref.shape[1] // fc):
        hc = jnp.dot(h2, w1_ref[:, c * fc:(c + 1) * fc], preferred_element_type=F32)
        hc = jnp.square(jnp.maximum(hc, 0.0)).astype(BF16)
        acc = acc + jnp.dot(hc, w2_ref[c * fc:(c + 1) * fc, :], preferred_element_type=F32)
    x2 = x1 + gf_ref[0] * acc
    if final:
        x2 = _rms(x2, fg_ref[...])
    o_ref[...] = x2


def _post_call(x, mix_a, mix_b, col_a, col_b, gm, g2, sh, sc, gf, wo, w1, w2, fg, *, seq, tm, final):
    n, d = x.shape
    tpb = seq // tm
    row = lambda i: (i, 0)
    bat = lambda i: (i // tpb, 0, 0)
    half = wo.shape[0] // 2
    return pl.pallas_call(
        functools.partial(_post_kernel, final=final),
        grid=(n // tm,),
        in_specs=[
            pl.BlockSpec((tm, d), row),
            pl.BlockSpec((tm, half), lambda i: (i, col_a)),
            pl.BlockSpec((tm, half), lambda i: (i, col_b)),
            pl.BlockSpec((1, 1, d), bat),
            _const_spec((1, d)),
            pl.BlockSpec((1, 1, d), bat),
            pl.BlockSpec((1, 1, d), bat),
            pl.BlockSpec((1, 1, d), bat),
            _const_spec(wo.shape),
            _const_spec(w1.shape),
            _const_spec(w2.shape),
            _const_spec((1, d)),
        ],
        out_specs=pl.BlockSpec((tm, d), row),
        out_shape=jax.ShapeDtypeStruct((n, d), F32),
        compiler_params=_cparams(("arbitrary",)),
        name="post_mlp",
    )(x, mix_a, mix_b, gm, g2, sh, sc, gf, wo, w1, w2, fg)


def _log_decay(g):
    return (jnp.minimum(g, 0.0) - jnp.log1p(jnp.exp(-jnp.abs(g)))) * (1.0 / GLA_TAU)


def _chunk_cumsum(tri, x):
    hi = x.astype(BF16)
    r1 = x - hi.astype(F32)
    mid = r1.astype(BF16)
    lo = (r1 - mid.astype(F32)).astype(BF16)
    return (jnp.dot(tri, hi, preferred_element_type=F32) + jnp.dot(tri, mid, preferred_element_type=F32)
            + jnp.dot(tri, lo, preferred_element_type=F32))


def _odd_in_kernel(x_ref, g1_ref, sh_ref, sc_ref, wall_ref, wgf_ref, bgf_ref, wgb_ref, bgb_ref,
                   k_ref, v_ref, q_ref, r_ref, cf_ref, cb_ref, *, tm):
    h = _rms(x_ref[...], g1_ref[...]) * (1.0 + sc_ref[0]) + sh_ref[0]
    p = _bdot(h, wall_ref[...])
    k_ref[...] = p[:, OA_K:OA_V]
    v_ref[...] = p[:, OA_V:OA_Q].astype(BF16)
    q_ref[...] = p[:, OA_Q:OA_R] * (GLA_DK ** -0.5)
    r = p[:, OA_R:OA_Z]
    r_ref[...] = r * jax.nn.sigmoid(r)
    z = p[:, OA_Z:OA_END].astype(BF16)
    df = _log_decay(jnp.dot(z, wgf_ref[...], preferred_element_type=F32) + bgf_ref[...])
    db = _log_decay(jnp.dot(z, wgb_ref[...], preferred_element_type=F32) + bgb_ref[...])
    rr = lax.broadcasted_iota(jnp.int32, (GLA_PAIR, GLA_PAIR), 0)
    cc = lax.broadcasted_iota(jnp.int32, (GLA_PAIR, GLA_PAIR), 1)
    same = (rr // GLA_CHUNK) == (cc // GLA_CHUNK)
    tri_f = (same & (rr >= cc)).astype(BF16)
    tri_b = (same & (rr <= cc)).astype(BF16)
    for c in range(tm // GLA_PAIR):
        rs = slice(c * GLA_PAIR, (c + 1) * GLA_PAIR)
        cf_ref[rs, :] = _chunk_cumsum(tri_f, df[rs])
        cb_ref[rs, :] = _chunk_cumsum(tri_b, db[rs])


def _odd_in_call(x, g1, sh, sc, w, *, seq, tm):
    n, d = x.shape
    tpb = seq // tm
    row = lambda i: (i, 0)
    bat = lambda i: (i // tpb, 0, 0)
    kd = GLA_HEADS * GLA_DK
    vd = GLA_HEADS * GLA_DV
    return pl.pallas_call(
        functools.partial(_odd_in_kernel, tm=tm),
        grid=(n // tm,),
        in_specs=[
            pl.BlockSpec((tm, d), row),
            _const_spec((1, d)),
            pl.BlockSpec((1, 1, d), bat),
            pl.BlockSpec((1, 1, d), bat),
            _const_spec((d, OA_END)),
            _const_spec((LANES, kd)),
            _const_spec((1, kd)),
            _const_spec((LANES, kd)),
            _const_spec((1, kd)),
        ],
        out_specs=[
            pl.BlockSpec((tm, kd), row),
            pl.BlockSpec((tm, vd), row),
            pl.BlockSpec((tm, kd), row),
            pl.BlockSpec((tm, vd), row),
            pl.BlockSpec((tm, kd), row),
            pl.BlockSpec((tm, kd), row),
        ],
        out_shape=[
            jax.ShapeDtypeStruct((n, kd), F32),
            jax.ShapeDtypeStruct((n, vd), BF16),
            jax.ShapeDtypeStruct((n, kd), F32),
            jax.ShapeDtypeStruct((n, vd), F32),
            jax.ShapeDtypeStruct((n, kd), F32),
            jax.ShapeDtypeStruct((n, kd), F32),
        ],
        compiler_params=_cparams(("arbitrary",)),
        name="odd_in",
    )(x, g1, sh, sc, w["wall"], w["wgf"], w["bgf"], w["wgb"], w["bgb"])


def _dot_tn(a, b):
    return lax.dot_general(a, b, (((0,), (0,)), ((), ())), preferred_element_type=F32)


def _gla_kernel(*refs, reverse, npair, combine):
    if combine:
        q_ref, k_ref, c_ref, v_ref, s0_ref, of_ref, r_ref, on_ref, o_ref, sfin_ref, s_scr = refs
    else:
        q_ref, k_ref, c_ref, v_ref, s0_ref, o_ref, sfin_ref, s_scr = refs
    i = pl.program_id(1)

    @pl.when(i == 0)
    def _():
        s_scr[...] = s0_ref[...]

    ch, pair = GLA_CHUNK, GLA_PAIR
    rr = lax.broadcasted_iota(jnp.int32, (pair, pair), 0)
    cc = lax.broadcasted_iota(jnp.int32, (pair, pair), 1)
    mask = ((rr // ch) == (cc // ch)) & ((rr <= cc) if reverse else (rr >= cc))
    lo = lax.broadcasted_iota(jnp.int32, (pair, GLA_DK), 0) < ch
    for pr in (range(npair - 1, -1, -1) if reverse else range(npair)):
        rs = slice(pr * pair, (pr + 1) * pair)
        for h in range(GLA_HEADS):
            ks = slice(h * GLA_DK, (h + 1) * GLA_DK)
            vs = slice(h * GLA_DV, (h + 1) * GLA_DV)
            bc = c_ref[rs, ks]
            k = k_ref[rs, ks]
            v = v_ref[rs, vs]
            if reverse:
                t_lo, t_hi = bc[0:1, :], bc[ch:ch + 1, :]
            else:
                t_lo, t_hi = bc[ch - 1:ch, :], bc[pair - 1:pair, :]
            q_t = (q_ref[rs, ks] * jnp.exp(bc)).astype(BF16)
            k_t = (k * jnp.exp(-bc)).astype(BF16)
            k_end = k * jnp.exp(jnp.where(lo, t_lo, t_hi) - bc)
            a = jnp.where(mask, _dot_nt(q_t, k_t), 0.0).astype(BF16)
            o = jnp.dot(a, v, preferred_element_type=F32)
            st = s_scr[h]
            halves = [(slice(0, ch), lo, t_lo), (slice(ch, pair), ~lo, t_hi)]
            inter = {}
            for hs, hm, tot in (halves[::-1] if reverse else halves):
                inter[hs.start] = _dot_nt(q_t[hs], st.astype(BF16))
                st = jnp.exp(tot) * st + _dot_tn(v, jnp.where(hm, k_end, 0.0).astype(BF16))
            s_scr[h] = st
            o = o + jnp.concatenate([inter[0], inter[ch]], axis=0)
            if combine:
                o = _rms(o + of_ref[rs, vs], on_ref[...]) * r_ref[rs, vs]
            o_ref[rs, vs] = o.astype(o_ref.dtype)

    @pl.when(i == pl.num_programs(1) - 1)
    def _():
        sfin_ref[...] = s_scr[...]


def _gla_call(q, k, c, v, s0, *, reverse, tb, o_fwd=None, r=None, o_norm=None):
    b, l, _ = q.shape
    nb = l // tb
    combine = o_fwd is not None
    blk = (lambda bi, i: (bi, nb - 1 - i, 0)) if reverse else (lambda bi, i: (bi, i, 0))
    kspec = pl.BlockSpec((None, tb, GLA_HEADS * GLA_DK), blk)
    vspec = pl.BlockSpec((None, tb, GLA_HEADS * GLA_DV), blk)
    sspec = pl.BlockSpec((None, GLA_HEADS, GLA_DV, GLA_DK), lambda bi, i: (bi, 0, 0, 0))
    in_specs = [kspec, kspec, kspec, vspec, sspec]
    args = [q, k, c, v, s0]
    if combine:
        in_specs += [vspec, vspec, pl.BlockSpec((1, GLA_DV), lambda bi, i: (0, 0))]
        args += [o_fwd, r, o_norm]
    return pl.pallas_call(
        functools.partial(_gla_kernel, reverse=reverse, npair=tb // GLA_PAIR, combine=combine),
        grid=(b, nb),
        in_specs=in_specs,
        out_specs=[vspec, sspec],
        out_shape=[
            jax.ShapeDtypeStruct((b, l, GLA_HEADS * GLA_DV), BF16 if combine else F32),
            jax.ShapeDtypeStruct((b, GLA_HEADS, GLA_DV, GLA_DK), F32),
        ],
        scratch_shapes=[pltpu.VMEM((GLA_HEADS, GLA_DV, GLA_DK), F32)],
        compiler_params=_cparams(("arbitrary", "arbitrary")),
        name="gla_bwd" if reverse else "gla_fwd",
    )(*args)


def _rot_cols(w):
    q = MLA_ROPE // 4
    return jnp.concatenate([-w[..., q:2 * q], w[..., 0:q], -w[..., 3 * q:4 * q], w[..., 2 * q:3 * q]], axis=-1)


def _prep_even(w_in, q_norm, w_uq, kv_norm, w_ukv, cm_norm, cm_ws, cm_bs):
    d = w_in.shape[0]
    e_q, e_kv = MLA_Q_RANK, MLA_Q_RANK + MLA_KV_RANK
    e_r = e_kv + MLA_ROPE
    e_u = e_r + CM_WIDTH
    wkr = w_in[:, e_kv:e_r]
    wall = jnp.concatenate([w_in[:, :e_kv], wkr, _rot_cols(wkr), jnp.zeros((d, LANES - 2 * MLA_ROPE), F32),
                            w_in[:, e_r:e_u], w_in[:, e_u:]], axis=1).astype(BF16)
    uq = w_uq.reshape(MLA_Q_RANK, MLA_HEADS, MLA_NOPE + MLA_ROPE)
    uq_r = uq[..., MLA_NOPE:]
    wuqt = jnp.concatenate([uq, _rot_cols(uq_r)], axis=-1).reshape(MLA_Q_RANK, MLA_HEADS * HEAD_SLOT).T.astype(BF16)
    ukv = w_ukv.reshape(MLA_KV_RANK, MLA_HEADS, MLA_NOPE + MLA_V)
    wuk = jnp.concatenate([ukv[..., :MLA_NOPE], jnp.zeros((MLA_KV_RANK, MLA_HEADS, HEAD_SLOT - MLA_NOPE), F32)],
                          axis=-1).reshape(MLA_KV_RANK, MLA_HEADS * HEAD_SLOT).astype(BF16)
    wuvt = ukv[..., MLA_NOPE:].reshape(MLA_KV_RANK, MLA_HEADS * MLA_V).T.astype(BF16)
    bias = jnp.repeat(cm_bs.T, CM_GROUP_DIM, axis=1)
    return dict(wall=wall, qn=q_norm[None], wuqt=wuqt, kvn=kv_norm[None], wuk=wuk, wuvt=wuvt, cmn=cm_norm[None],
                ws=cm_ws.astype(BF16), bias=bias)


def _prep_odd(w_in, w_gf, b_gf, w_gb, b_gb):
    d = w_in.shape[0]
    o_k = GLA_HEADS * GLA_DK
    o_v = o_k + GLA_HEADS * GLA_DV
    o_zb = o_v + 2 * GLA_GATE_RANK
    o_q = o_zb + GLA_HEADS * GLA_DK
    wall = jnp.concatenate([w_in[:, :o_v], w_in[:, o_zb:o_q], w_in[:, o_q:], w_in[:, o_v:o_zb],
                            jnp.zeros((d, LANES - 2 * GLA_GATE_RANK), F32)], axis=1).astype(BF16)
    zr = GLA_GATE_RANK
    wgf = jnp.zeros((LANES, o_k), F32).at[:zr].set(w_gf).astype(BF16)
    wgb = jnp.zeros((LANES, o_k), F32).at[zr:2 * zr].set(w_gb).astype(BF16)
    return dict(wall=wall, wgf=wgf, bgf=b_gf[None], wgb=wgb, bgb=b_gb[None])


def _rope_tables(length):
    rows = length // GRID_W
    r = jnp.repeat(jnp.arange(rows, dtype=F32), GRID_W)
    col = jnp.tile(jnp.arange(GRID_W, dtype=F32), rows)
    half = MLA_ROPE // 2
    inv = ROPE_BASE ** (-jnp.arange(0, half, 2, dtype=F32) / half)
    ang_r = r[:, None] * inv
    ang_c = col[:, None] * inv
    ang = jnp.concatenate([ang_r, ang_r, ang_c, ang_c], axis=-1)
    one = jnp.ones((length, MLA_NOPE), F32)
    pad = jnp.zeros((length, HEAD_SLOT - MLA_NOPE - MLA_ROPE), F32)
    ta = jnp.concatenate([one, jnp.cos(ang), pad], axis=1)
    tb = jnp.concatenate([0.0 * one, jnp.sin(ang), pad], axis=1)
    return ta, tb, jnp.cos(ang).T, jnp.sin(ang).T


def _flat_tables(length):
    ta = jnp.concatenate([jnp.ones((length, MLA_NOPE + MLA_ROPE), F32),
                          jnp.zeros((length, HEAD_SLOT - MLA_NOPE - MLA_ROPE), F32)], axis=1)
    return ta, jnp.zeros_like(ta), jnp.ones((MLA_ROPE, length), F32), jnp.zeros((MLA_ROPE, length), F32)


def _row_tile(seq, want):
    t = min(seq, want)
    while seq % t:
        t //= 2
    return t


def kernel(x, c, ctx, c_ctx, ada_w, ada_b, norm1_g, norm2_g, mlp_w1, mlp_w2, ev_w_in, ev_q_norm, ev_w_uq, ev_kv_norm,
           ev_w_ukv, ev_cm_norm, ev_cm_ws, ev_cm_bs, ev_w_out, od_w_in, od_w_gf, od_b_gf, od_w_gb, od_b_gb, od_o_norm,
           od_w_out, final_g):
    batch, seq, d = x.shape
    lc = ctx.shape[1]
    depth = ada_w.shape[0]
    tm_l = _row_tile(seq, 512)
    tm_c = _row_tile(lc, 256)
    tm_e = tm_c
    assert seq % tm_e == 0 and lc % tm_e == 0 and seq % lc == 0
    tq_l = _row_tile(seq, 256)
    tq_c = _row_tile(lc, 256)
    tb_l = _row_tile(seq, 256)
    tb_c = _row_tile(lc, 256)

    cvec = jnp.concatenate([c, c_ctx[None], jnp.zeros((8 - batch - 1, d), F32)], axis=0)
    mods = _ada_call(cvec, ada_w, ada_b).reshape(depth, 8, 6, d)

    tabs_l = _rope_tables(seq)
    tabs_c = _flat_tables(lc)
    hq = MLA_HEADS * HEAD_SLOT

    xl = x.reshape(batch * seq, d)
    xc = ctx.reshape(batch * lc, d)
    for i in range(depth):
        need_ctx = i < depth - 1
        j = i // 2
        ml = [mods[i, :batch, t][:, None, :] for t in range(6)]
        mc = [mods[i, batch:batch + 1, t][:, None, :] for t in range(6)]
        g1 = norm1_g[i][None]
        g2 = norm2_g[i][None]
        w1 = mlp_w1[i].astype(BF16)
        w2 = mlp_w2[i].astype(BF16)
        final = i == depth - 1
        if i % 2 == 0:
            w = _prep_even(ev_w_in[j], ev_q_norm[j], ev_w_uq[j], ev_kv_norm[j], ev_w_ukv[j], ev_cm_norm[j],
                           ev_cm_ws[j], ev_cm_bs[j])
            wo = ev_w_out[j].astype(BF16)
            kv_rows = seq + lc
            qtl, k_all, vt_all, mll = _even_in_call(xl, g1, ml[0], ml[1], tabs_l, w, seq=seq, tm=tm_e, rows=seq,
                                                    kv_rows=kv_rows, kv_off=0)
            qtc, k_all, vt_all, mlc = _even_in_call(xc, g1, mc[0], mc[1], tabs_c, w, seq=batch * lc, tm=tm_e, rows=lc,
                                                    kv_rows=kv_rows, kv_off=seq, kv_bufs=(k_all, vt_all))
            al = _attn_call(qtl, k_all, vt_all, batch=batch, tq=tq_l, t_len=kv_rows, t_off=0)
            xl = _post_call(xl, al, mll, 0, 0, ml[2], g2, ml[3], ml[4], ml[5], wo, w1, w2, final_g[None],
                            seq=seq, tm=tm_l, final=final)
            if need_ctx:
                ac = _attn_call(qtc, k_all, vt_all, batch=batch, tq=tq_c, t_len=lc, t_off=seq)
                xc = _post_call(xc, ac, mlc, 0, 0, mc[2], g2, mc[3], mc[4], mc[5], wo, w1, w2, final_g[None],
                                seq=batch * lc, tm=tm_c, final=False)
        else:
            w = _prep_odd(od_w_in[j], od_w_gf[j], od_b_gf[j], od_w_gb[j], od_b_gb[j])
            wo = od_w_out[j].astype(BF16)
            on = od_o_norm[j][None]
            kc, vc, qc, rc, dfc, dbc = _odd_in_call(xc, g1, mc[0], mc[1], w, seq=batch * lc, tm=tm_c)
            kl, vl, ql, rl, dfl, dbl = _odd_in_call(xl, g1, ml[0], ml[1], w, seq=seq, tm=tm_l)
            r3 = lambda t, n: t.reshape(batch, n, t.shape[-1])
            s0 = jnp.zeros((batch, GLA_HEADS, GLA_DV, GLA_DK), F32)
            ocf, s_f = _gla_call(r3(qc, lc), r3(kc, lc), r3(dfc, lc), r3(vc, lc), s0, reverse=False, tb=tb_c)
            mixc, s_b = _gla_call(r3(qc, lc), r3(kc, lc), r3(dbc, lc), r3(vc, lc), s0, reverse=True, tb=tb_c,
                                  o_fwd=ocf, r=r3(rc, lc), o_norm=on)
            olf, _ = _gla_call(r3(ql, seq), r3(kl, seq), r3(dfl, seq), r3(vl, seq), s_f, reverse=False, tb=tb_l)
            mixl, _ = _gla_call(r3(ql, seq), r3(kl, seq), r3(dbl, seq), r3(vl, seq), s_b, reverse=True, tb=tb_l,
                                o_fwd=olf, r=r3(rl, seq), o_norm=on)
            mixl = mixl.reshape(batch * seq, -1)
            xl = _post_call(xl, mixl, mixl, 0, 1, ml[2], g2, ml[3], ml[4], ml[5], wo, w1, w2, final_g[None],
                            seq=seq, tm=tm_l, final=final)
            if need_ctx:
                mixc = mixc.reshape(batch * lc, -1)
                xc = _post_call(xc, mixc, mixc, 0, 1, mc[2], g2, mc[3], mc[4], mc[5], wo, w1, w2, final_g[None],
                                seq=batch * lc, tm=tm_c, final=False)
    return xl.reshape(batch, seq, d)
```

```python
import functools

import jax
import jax.numpy as jnp
from jax import lax
from jax.experimental import pallas as pl
from jax.experimental.pallas import tpu as pltpu

F32 = jnp.float32
BF16 = jnp.bfloat16

D_MODEL = 1024
DEPTH = 4
GRID_W = 64
EPS = 1e-6
MLA_HEADS = 8
MLA_NOPE = 64
MLA_ROPE = 32
MLA_V = 64
MLA_Q_RANK = 384
MLA_KV_RANK = 256
MLA_SCALE = (MLA_NOPE + MLA_ROPE) ** -0.5
ROPE_BASE = 10000.0
CM_CHUNK = 128
CM_GROUPS = 4
CM_GROUP_DIM = 128
CM_WIDTH = CM_GROUPS * CM_GROUP_DIM
GLA_HEADS = 4
GLA_DK = 128
GLA_DV = 256
GLA_GATE_RANK = 16
GLA_TAU = 16.0
GLA_CHUNK = 64
GLA_PAIR = 2 * GLA_CHUNK
D_FF = 4 * D_MODEL

LANES = 128
HEAD_SLOT = LANES
BF16_ROWS = 16
V_SLOT = MLA_V + BF16_ROWS
VMEM_LIMIT = 56 * 1024 * 1024
MAX_KEY_TILE = 256
ATTN_UNROLL = 8
Q_PRESCALE = MLA_SCALE * 1.4426950408889634

EA_Q = 0
EA_KV = EA_Q + MLA_Q_RANK
EA_KR = EA_KV + MLA_KV_RANK
EA_U = EA_KR + LANES
EA_V = EA_U + CM_WIDTH
EA_END = EA_V + CM_WIDTH
OA_K = 0
OA_V = OA_K + GLA_HEADS * GLA_DK
OA_Q = OA_V + GLA_HEADS * GLA_DV
OA_R = OA_Q + GLA_HEADS * GLA_DK
OA_Z = OA_R + GLA_HEADS * GLA_DV
OA_END = OA_Z + LANES


def _cparams(sem):
    return pltpu.CompilerParams(dimension_semantics=sem, vmem_limit_bytes=VMEM_LIMIT)


def _rms(x, g):
    return x * lax.rsqrt(jnp.mean(x * x, axis=-1, keepdims=True) + EPS) * g


def _bdot(a, b):
    return jnp.dot(a.astype(BF16), b.astype(BF16), preferred_element_type=F32)


def _const_spec(shape):
    nd = len(shape)
    return pl.BlockSpec(shape, lambda *_: (0,) * nd, pipeline_mode=pl.Buffered(1))


def _ada_kernel(c_ref, w_ref, b_ref, o_ref):
    s = c_ref[...]
    s = s * jax.nn.sigmoid(s)
    o_ref[0] = _bdot(s, w_ref[0]) + b_ref[0]


def _ada_call(cvec, ada_w, ada_b):
    depth, d, n = ada_w.shape
    tn = 1536
    return pl.pallas_call(
        _ada_kernel,
        grid=(depth, n // tn),
        in_specs=[
            pl.BlockSpec((8, d), lambda l, j: (0, 0)),
            pl.BlockSpec((1, d, tn), lambda l, j: (l, 0, j)),
            pl.BlockSpec((1, 1, tn), lambda l, j: (l, 0, j)),
        ],
        out_specs=pl.BlockSpec((1, 8, tn), lambda l, j: (l, 0, j)),
        out_shape=jax.ShapeDtypeStruct((depth, 8, n), F32),
        compiler_params=_cparams(("arbitrary", "arbitrary")),
        name="ada_mod",
    )(cvec, ada_w, ada_b.reshape(depth, 1, n))


def _rope_slot(t, a, b):
    return t * a + pltpu.roll(t, HEAD_SLOT - MLA_ROPE, 1) * b


def _dot_nt(a, b):
    return lax.dot_general(a, b, (((1,), (1,)), ((), ())), preferred_element_type=F32)


def _even_in_kernel(*refs, tm, n_in):
    (x_ref, g1_ref, sh_ref, sc_ref, ta_ref, tb_ref, ct_ref, st_ref, wall_ref, qn_ref, wuqt_ref, kvn_ref, wuk_ref,
     wuvt_ref, cmn_ref, ws_ref, bias_ref) = refs[:17]
    qt_ref, k_ref, vt_ref, ml_ref = refs[n_in:]
    h = _rms(x_ref[...], g1_ref[...]) * (1.0 + sc_ref[0]) + sh_ref[0]
    p = _bdot(h, wall_ref[...])

    cq = _rms(p[:, EA_Q:EA_KV], qn_ref[...]).astype(BF16)
    qt = _dot_nt(wuqt_ref[...], cq)
    cos_t = ct_ref[...]
    sin_t = st_ref[...]
    r0, r1 = MLA_NOPE, MLA_NOPE + MLA_ROPE
    for hd in range(MLA_HEADS):
        b = hd * HEAD_SLOT
        qt_ref[b:b + r0, :] = (qt[b:b + r0, :] * Q_PRESCALE).astype(BF16)
        roped = qt[b + r0:b + r1, :] * cos_t + qt[b + r1:b + HEAD_SLOT, :] * sin_t
        qt_ref[b + r0:b + r1, :] = (roped * Q_PRESCALE).astype(BF16)
        qt_ref[b + r1:b + HEAD_SLOT, :] = jnp.zeros((HEAD_SLOT - r1, tm), BF16)

    ckv = _rms(p[:, EA_KV:EA_KR], kvn_ref[...]).astype(BF16)
    kn = jnp.dot(ckv, wuk_ref[...], preferred_element_type=F32)
    kr = _rope_slot(pltpu.roll(p[:, EA_KR:EA_U], MLA_NOPE, 1), ta_ref[...], tb_ref[...])
    for hd in range(MLA_HEADS):
        sl = slice(hd * HEAD_SLOT, (hd + 1) * HEAD_SLOT)
        k_ref[:, sl] = (kn[:, sl] + kr).astype(BF16)
    vt = _dot_nt(wuvt_ref[...], ckv)
    for hd in range(MLA_HEADS):
        b = hd * V_SLOT
        vt_ref[b:b + MLA_V, :] = vt[hd * MLA_V:(hd + 1) * MLA_V, :].astype(BF16)
        vt_ref[b + MLA_V:b + V_SLOT, :] = jnp.ones((V_SLOT - MLA_V, tm), BF16)

    u = jax.nn.gelu(p[:, EA_U:EA_V])
    vv = jax.nn.gelu(p[:, EA_V:EA_END])
    cmn = cmn_ref[...]
    for g in range(CM_GROUPS):
        gl = slice(g * CM_GROUP_DIM, (g + 1) * CM_GROUP_DIM)
        vn = _rms(vv[:, gl], cmn).astype(BF16)
        w = ws_ref[g]
        for c in range(tm // CM_CHUNK):
            rs = slice(c * CM_CHUNK, (c + 1) * CM_CHUNK)
            y = jnp.dot(w, vn[rs], preferred_element_type=F32) + bias_ref[:, gl]
            ml_ref[rs, gl] = (u[rs, gl] * y).astype(BF16)


def _even_in_call(x, g1, sh, sc, tabs, w, *, seq, tm, rows, kv_rows, kv_off, kv_bufs=None):
    ta, tb, cos_t, sin_t = tabs
    n, d = x.shape
    batch = n // rows
    tpb = seq // tm
    tpt = ta.shape[0] // tm
    tpr = rows // tm
    kv_blk = lambda i: (i // tpr) * (kv_rows // tm) + kv_off // tm + i % tpr
    row = lambda i: (i, 0)
    col = lambda i: (0, i)
    bat = lambda i: (i // tpb, 0, 0)
    tab = lambda i: (i % tpt, 0)
    tab_t = lambda i: (0, i % tpt)
    hq = MLA_HEADS * HEAD_SLOT
    hv = MLA_HEADS * MLA_V
    in_specs = [
        pl.BlockSpec((tm, d), row),
        _const_spec((1, d)),
        pl.BlockSpec((1, 1, d), bat),
        pl.BlockSpec((1, 1, d), bat),
        pl.BlockSpec((tm, HEAD_SLOT), tab),
        pl.BlockSpec((tm, HEAD_SLOT), tab),
        pl.BlockSpec((MLA_ROPE, tm), tab_t),
        pl.BlockSpec((MLA_ROPE, tm), tab_t),
        _const_spec((d, EA_END)),
        _const_spec((1, MLA_Q_RANK)),
        _const_spec((hq, MLA_Q_RANK)),
        _const_spec((1, MLA_KV_RANK)),
        _const_spec((MLA_KV_RANK, hq)),
        _const_spec((hv, MLA_KV_RANK)),
        _const_spec((1, CM_GROUP_DIM)),
        _const_spec((CM_GROUPS, CM_CHUNK, CM_CHUNK)),
        _const_spec((CM_CHUNK, CM_WIDTH)),
    ]
    args = [x, g1, sh, sc, ta, tb, cos_t, sin_t, w["wall"], w["qn"], w["wuqt"], w["kvn"], w["wuk"], w["wuvt"],
            w["cmn"], w["ws"], w["bias"]]
    aliases = {}
    if kv_bufs is not None:
        aliases = {len(args): 1, len(args) + 1: 2}
        in_specs += [pl.BlockSpec(memory_space=pl.ANY), pl.BlockSpec(memory_space=pl.ANY)]
        args += list(kv_bufs)
    return pl.pallas_call(
        functools.partial(_even_in_kernel, tm=tm, n_in=len(args)),
        grid=(n // tm,),
        in_specs=in_specs,
        out_specs=[
            pl.BlockSpec((hq, tm), col),
            pl.BlockSpec((tm, hq), lambda i: (kv_blk(i), 0)),
            pl.BlockSpec((MLA_HEADS * V_SLOT, tm), lambda i: (0, kv_blk(i))),
            pl.BlockSpec((tm, CM_WIDTH), row),
        ],
        out_shape=[
            jax.ShapeDtypeStruct((hq, n), BF16),
            jax.ShapeDtypeStruct((batch * kv_rows, hq), BF16),
            jax.ShapeDtypeStruct((MLA_HEADS * V_SLOT, batch * kv_rows), BF16),
            jax.ShapeDtypeStruct((n, CM_WIDTH), BF16),
        ],
        input_output_aliases=aliases,
        compiler_params=_cparams(("arbitrary",)),
        name="even_in",
    )(*args)


def _attn_kernel(qt_ref, k_ref, vt_ref, o_ref, s_scr, p_scr, acc_scr, *, tq, tk, nk, unroll):
    def rows(t):
        return t * tk if isinstance(t, int) else pl.multiple_of(t * tk, LANES)

    def step(t, par, h, m, alpha, scores=True, softmax=True, weighted=True):
        new_alpha = alpha
        if scores:
            ks = k_ref[pl.ds(rows(t), tk), h * HEAD_SLOT:(h + 1) * HEAD_SLOT]
            s_scr[h, par] = jnp.dot(ks, qt_ref[h * HEAD_SLOT:(h + 1) * HEAD_SLOT, :], preferred_element_type=F32)
        if weighted:
            vs = vt_ref[h * V_SLOT:(h + 1) * V_SLOT, pl.ds(rows(t - 2), tk)]
            acc_scr[h] = alpha * acc_scr[h] + jnp.dot(vs, p_scr[h, par], preferred_element_type=F32)
        if softmax:
            s = s_scr[h, 1 - par]
            m_new = jnp.maximum(m, jnp.max(s, axis=0, keepdims=True))
            new_alpha = jnp.exp2(m - m_new)
            p_scr[h, 1 - par] = jnp.exp2(s - m_new).astype(BF16)
            m = m_new
        return m, new_alpha

    carry = []
    for h in range(2):
        acc_scr[h] = jnp.zeros((V_SLOT, tq), F32)
        m, a = jnp.full((1, tq), -jnp.inf, F32), jnp.zeros((1, tq), F32)
        m, a = step(0, 0, h, m, a, softmax=False, weighted=False)
        m, a = step(1, 1, h, m, a, weighted=False)
        carry.append((m, a))

    groups = (nk - 2) // unroll

    def body(g, carry):
        t0 = 2 + unroll * g
        out = []
        for h in range(2):
            m, a = carry[h]
            for u in range(unroll):
                m, a = step(t0 + u, u % 2, h, m, a)
            out.append((m, a))
        return tuple(out)

    carry = lax.fori_loop(0, groups, body, tuple(carry))
    outs = []
    for h in range(2):
        m, a = carry[h]
        for t in range(2 + unroll * groups, nk):
            m, a = step(t, t % 2, h, m, a)
        m, a = step(nk, nk % 2, h, m, a, scores=False)
        m, a = step(nk + 1, (nk + 1) % 2, h, m, a, scores=False, softmax=False)
        acc = acc_scr[h]
        outs.append(acc[0:MLA_V, :] * (1.0 / acc[MLA_V:MLA_V + 1, :]))
    o_ref[...] = jnp.concatenate(outs, axis=0).T.astype(BF16)


def _key_tile(t_len):
    tk = (min(t_len // 2, MAX_KEY_TILE) // LANES) * LANES
    while t_len % tk:
        tk -= LANES
    return tk


def _attn_call(qt, k, vt, *, batch, tq, t_len, t_off):
    n = qt.shape[1]
    kb = k.shape[0] // batch // t_len
    ko = t_off // t_len
    nq = n // batch // tq
    tk = _key_tile(t_len)
    return pl.pallas_call(
        functools.partial(_attn_kernel, tq=tq, tk=tk, nk=t_len // tk, unroll=ATTN_UNROLL),
        grid=(batch, MLA_HEADS // 2, nq),
        in_specs=[
            pl.BlockSpec((2 * HEAD_SLOT, tq), lambda b, hp, i: (hp, b * nq + i)),
            pl.BlockSpec((t_len, 2 * HEAD_SLOT), lambda b, hp, i: (b * kb + ko, hp)),
            pl.BlockSpec((2 * V_SLOT, t_len), lambda b, hp, i: (hp, b * kb + ko)),
        ],
        out_specs=pl.BlockSpec((tq, 2 * MLA_V), lambda b, hp, i: (b * nq + i, hp)),
        out_shape=jax.ShapeDtypeStruct((n, MLA_HEADS * MLA_V), BF16),
        scratch_shapes=[
            pltpu.VMEM((2, 2, tk, tq), F32),
            pltpu.VMEM((2, 2, tk, tq), BF16),
            pltpu.VMEM((2, V_SLOT, tq), F32),
        ],
        compiler_params=_cparams(("arbitrary", "arbitrary", "arbitrary")),
        name="mla_attn",
    )(qt, k, vt)


def _post_kernel(x_ref, a_ref, b_ref, gm_ref, g2_ref, sh_ref, sc_ref, gf_ref, wo_ref, w1_ref, w2_ref, fg_ref,
                 o_ref, *, final):
    half = wo_ref.shape[0] // 2
    y = (jnp.dot(a_ref[...], wo_ref[:half, :], preferred_element_type=F32)
         + jnp.dot(b_ref[...], wo_ref[half:, :], preferred_element_type=F32))
    x1 = x_ref[...] + gm_ref[0] * y
    h2 = (_rms(x1, g2_ref[...]) * (1.0 + sc_ref[0]) + sh_ref[0]).astype(BF16)
    acc = jnp.zeros_like(x1)
    fc = 1024
    for c in range(w1_ref.shape[1] // fc):
        hc = jnp.dot(h2, w1_ref[:, c * fc:(c + 1) * fc], preferred_element_type=F32)
        hc = jnp.square(jnp.maximum(hc, 0.0)).astype(BF16)
        acc = acc + jnp.dot(hc, w2_ref[c * fc:(c + 1) * fc, :], preferred_element_type=F32)
    x2 = x1 + gf_ref[0] * acc
    if final:
        x2 = _rms(x2, fg_ref[...])
    o_ref[...] = x2


def _post_call(x, mix_a, mix_b, col_a, col_b, gm, g2, sh, sc, gf, wo, w1, w2, fg, *, seq, tm, final):
    n, d = x.shape
    tpb = seq // tm
    row = lambda i: (i, 0)
    bat = lambda i: (i // tpb, 0, 0)
    half = wo.shape[0] // 2
    return pl.pallas_call(
        functools.partial(_post_kernel, final=final),
        grid=(n // tm,),
        in_specs=[
            pl.BlockSpec((tm, d), row),
            pl.BlockSpec((tm, half), lambda i: (i, col_a)),
            pl.BlockSpec((tm, half), lambda i: (i, col_b)),
            pl.BlockSpec((1, 1, d), bat),
            _const_spec((1, d)),
            pl.BlockSpec((1, 1, d), bat),
            pl.BlockSpec((1, 1, d), bat),
            pl.BlockSpec((1, 1, d), bat),
            _const_spec(wo.shape),
            _const_spec(w1.shape),
            _const_spec(w2.shape),
            _const_spec((1, d)),
        ],
        out_specs=pl.BlockSpec((tm, d), row),
        out_shape=jax.ShapeDtypeStruct((n, d), F32),
        compiler_params=_cparams(("arbitrary",)),
        name="post_mlp",
    )(x, mix_a, mix_b, gm, g2, sh, sc, gf, wo, w1, w2, fg)


def _log_decay(g):
    return (jnp.minimum(g, 0.0) - jnp.log1p(jnp.exp(-jnp.abs(g)))) * (1.0 / GLA_TAU)


def _chunk_cumsum(tri, x):
    hi = x.astype(BF16)
    r1 = x - hi.astype(F32)
    mid = r1.astype(BF16)
    lo = (r1 - mid.astype(F32)).astype(BF16)
    return (jnp.dot(tri, hi, preferred_element_type=F32) + jnp.dot(tri, mid, preferred_element_type=F32)
            + jnp.dot(tri, lo, preferred_element_type=F32))


def _odd_in_kernel(x_ref, g1_ref, sh_ref, sc_ref, wall_ref, wgf_ref, bgf_ref, wgb_ref, bgb_ref,
                   k_ref, v_ref, q_ref, r_ref, cf_ref, cb_ref, *, tm):
    h = _rms(x_ref[...], g1_ref[...]) * (1.0 + sc_ref[0]) + sh_ref[0]
    p = _bdot(h, wall_ref[...])
    k_ref[...] = p[:, OA_K:OA_V]
    v_ref[...] = p[:, OA_V:OA_Q].astype(BF16)
    q_ref[...] = p[:, OA_Q:OA_R] * (GLA_DK ** -0.5)
    r = p[:, OA_R:OA_Z]
    r_ref[...] = r * jax.nn.sigmoid(r)
    z = p[:, OA_Z:OA_END].astype(BF16)
    df = _log_decay(jnp.dot(z, wgf_ref[...], preferred_element_type=F32) + bgf_ref[...])
    db = _log_decay(jnp.dot(z, wgb_ref[...], preferred_element_type=F32) + bgb_ref[...])
    rr = lax.broadcasted_iota(jnp.int32, (GLA_PAIR, GLA_PAIR), 0)
    cc = lax.broadcasted_iota(jnp.int32, (GLA_PAIR, GLA_PAIR), 1)
    same = (rr // GLA_CHUNK) == (cc // GLA_CHUNK)
    tri_f = (same & (rr >= cc)).astype(BF16)
    tri_b = (same & (rr <= cc)).astype(BF16)
    for c in range(tm // GLA_PAIR):
        rs = slice(c * GLA_PAIR, (c + 1) * GLA_PAIR)
        cf_ref[rs, :] = _chunk_cumsum(tri_f, df[rs])
        cb_ref[rs, :] = _chunk_cumsum(tri_b, db[rs])


def _odd_in_call(x, g1, sh, sc, w, *, seq, tm):
    n, d = x.shape
    tpb = seq // tm
    row = lambda i: (i, 0)
    bat = lambda i: (i // tpb, 0, 0)
    kd = GLA_HEADS * GLA_DK
    vd = GLA_HEADS * GLA_DV
    return pl.pallas_call(
        functools.partial(_odd_in_kernel, tm=tm),
        grid=(n // tm,),
        in_specs=[
            pl.BlockSpec((tm, d), row),
            _const_spec((1, d)),
            pl.BlockSpec((1, 1, d), bat),
            pl.BlockSpec((1, 1, d), bat),
            _const_spec((d, OA_END)),
            _const_spec((LANES, kd)),
            _const_spec((1, kd)),
            _const_spec((LANES, kd)),
            _const_spec((1, kd)),
        ],
        out_specs=[
            pl.BlockSpec((tm, kd), row),
            pl.BlockSpec((tm, vd), row),
            pl.BlockSpec((tm, kd), row),
            pl.BlockSpec((tm, vd), row),
            pl.BlockSpec((tm, kd), row),
            pl.BlockSpec((tm, kd), row),
        ],
        out_shape=[
            jax.ShapeDtypeStruct((n, kd), F32),
            jax.ShapeDtypeStruct((n, vd), BF16),
            jax.ShapeDtypeStruct((n, kd), F32),
            jax.ShapeDtypeStruct((n, vd), F32),
            jax.ShapeDtypeStruct((n, kd), F32),
            jax.ShapeDtypeStruct((n, kd), F32),
        ],
        compiler_params=_cparams(("arbitrary",)),
        name="odd_in",
    )(x, g1, sh, sc, w["wall"], w["wgf"], w["bgf"], w["wgb"], w["bgb"])


def _dot_tn(a, b):
    return lax.dot_general(a, b, (((0,), (0,)), ((), ())), preferred_element_type=F32)


def _gla_kernel(*refs, reverse, npair, combine):
    if combine:
        q_ref, k_ref, c_ref, v_ref, s0_ref, of_ref, r_ref, on_ref, o_ref, sfin_ref, s_scr = refs
    else:
        q_ref, k_ref, c_ref, v_ref, s0_ref, o_ref, sfin_ref, s_scr = refs
    i = pl.program_id(1)

    @pl.when(i == 0)
    def _():
        s_scr[...] = s0_ref[...]

    ch, pair = GLA_CHUNK, GLA_PAIR
    rr = lax.broadcasted_iota(jnp.int32, (pair, pair), 0)
    cc = lax.broadcasted_iota(jnp.int32, (pair, pair), 1)
    mask = ((rr // ch) == (cc // ch)) & ((rr <= cc) if reverse else (rr >= cc))
    lo = lax.broadcasted_iota(jnp.int32, (pair, GLA_DK), 0) < ch
    for pr in (range(npair - 1, -1, -1) if reverse else range(npair)):
        rs = slice(pr * pair, (pr + 1) * pair)
        for h in range(GLA_HEADS):
            ks = slice(h * GLA_DK, (h + 1) * GLA_DK)
            vs = slice(h * GLA_DV, (h + 1) * GLA_DV)
            bc = c_ref[rs, ks]
            k = k_ref[rs, ks]
            v = v_ref[rs, vs]
            if reverse:
                t_lo, t_hi = bc[0:1, :], bc[ch:ch + 1, :]
            else:
                t_lo, t_hi = bc[ch - 1:ch, :], bc[pair - 1:pair, :]
            q_t = (q_ref[rs, ks] * jnp.exp(bc)).astype(BF16)
            k_t = (k * jnp.exp(-bc)).astype(BF16)
            k_end = k * jnp.exp(jnp.where(lo, t_lo, t_hi) - bc)
            a = jnp.where(mask, _dot_nt(q_t, k_t), 0.0).astype(BF16)
            o = jnp.dot(a, v, preferred_element_type=F32)
            st = s_scr[h]
            halves = [(slice(0, ch), lo, t_lo), (slice(ch, pair), ~lo, t_hi)]
            inter = {}
            for hs, hm, tot in (halves[::-1] if reverse else halves):
                inter[hs.start] = _dot_nt(q_t[hs], st.astype(BF16))
                st = jnp.exp(tot) * st + _dot_tn(v, jnp.where(hm, k_end, 0.0).astype(BF16))
            s_scr[h] = st
            o = o + jnp.concatenate([inter[0], inter[ch]], axis=0)
            if combine:
                o = _rms(o + of_ref[rs, vs], on_ref[...]) * r_ref[rs, vs]
            o_ref[rs, vs] = o.astype(o_ref.dtype)

    @pl.when(i == pl.num_programs(1) - 1)
    def _():
        sfin_ref[...] = s_scr[...]


def _gla_call(q, k, c, v, s0, *, reverse, tb, o_fwd=None, r=None, o_norm=None):
    b, l, _ = q.shape
    nb = l // tb
    combine = o_fwd is not None
    blk = (lambda bi, i: (bi, nb - 1 - i, 0)) if reverse else (lambda bi, i: (bi, i, 0))
    kspec = pl.BlockSpec((None, tb, GLA_HEADS * GLA_DK), blk)
    vspec = pl.BlockSpec((None, tb, GLA_HEADS * GLA_DV), blk)
    sspec = pl.BlockSpec((None, GLA_HEADS, GLA_DV, GLA_DK), lambda bi, i: (bi, 0, 0, 0))
    in_specs = [kspec, kspec, kspec, vspec, sspec]
    args = [q, k, c, v, s0]
    if combine:
        in_specs += [vspec, vspec, pl.BlockSpec((1, GLA_DV), lambda bi, i: (0, 0))]
        args += [o_fwd, r, o_norm]
    return pl.pallas_call(
        functools.partial(_gla_kernel, reverse=reverse, npair=tb // GLA_PAIR, combine=combine),
        grid=(b, nb),
        in_specs=in_specs,
        out_specs=[vspec, sspec],
        out_shape=[
            jax.ShapeDtypeStruct((b, l, GLA_HEADS * GLA_DV), BF16 if combine else F32),
            jax.ShapeDtypeStruct((b, GLA_HEADS, GLA_DV, GLA_DK), F32),
        ],
        scratch_shapes=[pltpu.VMEM((GLA_HEADS, GLA_DV, GLA_DK), F32)],
        compiler_params=_cparams(("arbitrary", "arbitrary")),
        name="gla_bwd" if reverse else "gla_fwd",
    )(*args)


def _rot_cols(w):
    q = MLA_ROPE // 4
    return jnp.concatenate([-w[..., q:2 * q], w[..., 0:q], -w[..., 3 * q:4 * q], w[..., 2 * q:3 * q]], axis=-1)


def _prep_even(w_in, q_norm, w_uq, kv_norm, w_ukv, cm_norm, cm_ws, cm_bs):
    d = w_in.shape[0]
    e_q, e_kv = MLA_Q_RANK, MLA_Q_RANK + MLA_KV_RANK
    e_r = e_kv + MLA_ROPE
    e_u = e_r + CM_WIDTH
    wkr = w_in[:, e_kv:e_r]
    wall = jnp.concatenate([w_in[:, :e_kv], wkr, _rot_cols(wkr), jnp.zeros((d, LANES - 2 * MLA_ROPE), F32),
                            w_in[:, e_r:e_u], w_in[:, e_u:]], axis=1).astype(BF16)
    uq = w_uq.reshape(MLA_Q_RANK, MLA_HEADS, MLA_NOPE + MLA_ROPE)
    uq_r = uq[..., MLA_NOPE:]
    wuqt = jnp.concatenate([uq, _rot_cols(uq_r)], axis=-1).reshape(MLA_Q_RANK, MLA_HEADS * HEAD_SLOT).T.astype(BF16)
    ukv = w_ukv.reshape(MLA_KV_RANK, MLA_HEADS, MLA_NOPE + MLA_V)
    wuk = jnp.concatenate([ukv[..., :MLA_NOPE], jnp.zeros((MLA_KV_RANK, MLA_HEADS, HEAD_SLOT - MLA_NOPE), F32)],
                          axis=-1).reshape(MLA_KV_RANK, MLA_HEADS * HEAD_SLOT).astype(BF16)
    wuvt = ukv[..., MLA_NOPE:].reshape(MLA_KV_RANK, MLA_HEADS * MLA_V).T.astype(BF16)
    bias = jnp.repeat(cm_bs.T, CM_GROUP_DIM, axis=1)
    return dict(wall=wall, qn=q_norm[None], wuqt=wuqt, kvn=kv_norm[None], wuk=wuk, wuvt=wuvt, cmn=cm_norm[None],
                ws=cm_ws.astype(BF16), bias=bias)


def _prep_odd(w_in, w_gf, b_gf, w_gb, b_gb):
    d = w_in.shape[0]
    o_k = GLA_HEADS * GLA_DK
    o_v = o_k + GLA_HEADS * GLA_DV
    o_zb = o_v + 2 * GLA_GATE_RANK
    o_q = o_zb + GLA_HEADS * GLA_DK
    wall = jnp.concatenate([w_in[:, :o_v], w_in[:, o_zb:o_q], w_in[:, o_q:], w_in[:, o_v:o_zb],
                            jnp.zeros((d, LANES - 2 * GLA_GATE_RANK), F32)], axis=1).astype(BF16)
    zr = GLA_GATE_RANK
    wgf = jnp.zeros((LANES, o_k), F32).at[:zr].set(w_gf).astype(BF16)
    wgb = jnp.zeros((LANES, o_k), F32).at[zr:2 * zr].set(w_gb).astype(BF16)
    return dict(wall=wall, wgf=wgf, bgf=b_gf[None], wgb=wgb, bgb=b_gb[None])


def _rope_tables(length):
    rows = length // GRID_W
    r = jnp.repeat(jnp.arange(rows, dtype=F32), GRID_W)
    col = jnp.tile(jnp.arange(GRID_W, dtype=F32), rows)
    half = MLA_ROPE // 2
    inv = ROPE_BASE ** (-jnp.arange(0, half, 2, dtype=F32) / half)
    ang_r = r[:, None] * inv
    ang_c = col[:, None] * inv
    ang = jnp.concatenate([ang_r, ang_r, ang_c, ang_c], axis=-1)
    one = jnp.ones((length, MLA_NOPE), F32)
    pad = jnp.zeros((length, HEAD_SLOT - MLA_NOPE - MLA_ROPE), F32)
    ta = jnp.concatenate([one, jnp.cos(ang), pad], axis=1)
    tb = jnp.concatenate([0.0 * one, jnp.sin(ang), pad], axis=1)
    return ta, tb, jnp.cos(ang).T, jnp.sin(ang).T


def _flat_tables(length):
    ta = jnp.concatenate([jnp.ones((length, MLA_NOPE + MLA_ROPE), F32),
                          jnp.zeros((length, HEAD_SLOT - MLA_NOPE - MLA_ROPE), F32)], axis=1)
    return ta, jnp.zeros_like(ta), jnp.ones((MLA_ROPE, length), F32), jnp.zeros((MLA_ROPE, length), F32)


def _row_tile(seq, want):
    t = min(seq, want)
    while seq % t:
        t //= 2
    return t


def kernel(x, c, ctx, c_ctx, ada_w, ada_b, norm1_g, norm2_g, mlp_w1, mlp_w2, ev_w_in, ev_q_norm, ev_w_uq, ev_kv_norm,
           ev_w_ukv, ev_cm_norm, ev_cm_ws, ev_cm_bs, ev_w_out, od_w_in, od_w_gf, od_b_gf, od_w_gb, od_b_gb, od_o_norm,
           od_w_out, final_g):
    batch, seq, d = x.shape
    lc = ctx.shape[1]
    depth = ada_w.shape[0]
    tm_l = _row_tile(seq, 512)
    tm_c = _row_tile(lc, 256)
    tm_e = tm_c
    assert seq % tm_e == 0 and lc % tm_e == 0 and seq % lc == 0
    tq_l = _row_tile(seq, 512)
    tq_c = _row_tile(lc, 256)
    tb_l = _row_tile(seq, 256)
    tb_c = _row_tile(lc, 256)

    cvec = jnp.concatenate([c, c_ctx[None], jnp.zeros((8 - batch - 1, d), F32)], axis=0)
    mods = _ada_call(cvec, ada_w, ada_b).reshape(depth, 8, 6, d)

    tabs_l = _rope_tables(seq)
    tabs_c = _flat_tables(lc)
    hq = MLA_HEADS * HEAD_SLOT

    xl = x.reshape(batch * seq, d)
    xc = ctx.reshape(batch * lc, d)
    for i in range(depth):
        need_ctx = i < depth - 1
        j = i // 2
        ml = [mods[i, :batch, t][:, None, :] for t in range(6)]
        mc = [mods[i, batch:batch + 1, t][:, None, :] for t in range(6)]
        g1 = norm1_g[i][None]
        g2 = norm2_g[i][None]
        w1 = mlp_w1[i].astype(BF16)
        w2 = mlp_w2[i].astype(BF16)
        final = i == depth - 1
        if i % 2 == 0:
            w = _prep_even(ev_w_in[j], ev_q_norm[j], ev_w_uq[j], ev_kv_norm[j], ev_w_ukv[j], ev_cm_norm[j],
                           ev_cm_ws[j], ev_cm_bs[j])
            wo = ev_w_out[j].astype(BF16)
            kv_rows = seq + lc
            qtl, k_all, vt_all, mll = _even_in_call(xl, g1, ml[0], ml[1], tabs_l, w, seq=seq, tm=tm_e, rows=seq,
                                                    kv_rows=kv_rows, kv_off=0)
            qtc, k_all, vt_all, mlc = _even_in_call(xc, g1, mc[0], mc[1], tabs_c, w, seq=batch * lc, tm=tm_e, rows=lc,
                                                    kv_rows=kv_rows, kv_off=seq, kv_bufs=(k_all, vt_all))
            al = _attn_call(qtl, k_all, vt_all, batch=batch, tq=tq_l, t_len=kv_rows, t_off=0)
            xl = _post_call(xl, al, mll, 0, 0, ml[2], g2, ml[3], ml[4], ml[5], wo, w1, w2, final_g[None],
                            seq=seq, tm=tm_l, final=final)
            if need_ctx:
                ac = _attn_call(qtc, k_all, vt_all, batch=batch, tq=tq_c, t_len=lc, t_off=seq)
                xc = _post_call(xc, ac, mlc, 0, 0, mc[2], g2, mc[3], mc[4], mc[5], wo, w1, w2, final_g[None],
                                seq=batch * lc, tm=tm_c, final=False)
        else:
            w = _prep_odd(od_w_in[j], od_w_gf[j], od_b_gf[j], od_w_gb[j], od_b_gb[j])
            wo = od_w_out[j].astype(BF16)
            on = od_o_norm[j][None]
            kc, vc, qc, rc, dfc, dbc = _odd_in_call(xc, g1, mc[0], mc[1], w, seq=batch * lc, tm=tm_c)
            kl, vl, ql, rl, dfl, dbl = _odd_in_call(xl, g1, ml[0], ml[1], w, seq=seq, tm=tm_l)
            r3 = lambda t, n: t.reshape(batch, n, t.shape[-1])
            s0 = jnp.zeros((batch, GLA_HEADS, GLA_DV, GLA_DK), F32)
            ocf, s_f = _gla_call(r3(qc, lc), r3(kc, lc), r3(dfc, lc), r3(vc, lc), s0, reverse=False, tb=tb_c)
            mixc, s_b = _gla_call(r3(qc, lc), r3(kc, lc), r3(dbc, lc), r3(vc, lc), s0, reverse=True, tb=tb_c,
                                  o_fwd=ocf, r=r3(rc, lc), o_norm=on)
            olf, _ = _gla_call(r3(ql, seq), r3(kl, seq), r3(dfl, seq), r3(vl, seq), s_f, reverse=False, tb=tb_l)
            mixl, _ = _gla_call(r3(ql, seq), r3(kl, seq), r3(dbl, seq), r3(vl, seq), s_b, reverse=True, tb=tb_l,
                                o_fwd=olf, r=r3(rl, seq), o_norm=on)
            mixl = mixl.reshape(batch * seq, -1)
            xl = _post_call(xl, mixl, mixl, 0, 1, ml[2], g2, ml[3], ml[4], ml[5], wo, w1, w2, final_g[None],
                            seq=seq, tm=tm_l, final=final)
            if need_ctx:
                mixc = mixc.reshape(batch * lc, -1)
                xc = _post_call(xc, mixc, mixc, 0, 1, mc[2], g2, mc[3], mc[4], mc[5], wo, w1, w2, final_g[None],
                                seq=batch * lc, tm=tm_c, final=False)
    return xl.reshape(batch, seq, d)
```

```python
import functools

import jax
import jax.numpy as jnp
from jax import lax
from jax.experimental import pallas as pl
from jax.experimental.pallas import tpu as pltpu

F32 = jnp.float32
BF16 = jnp.bfloat16

D_MODEL = 1024
DEPTH = 4
GRID_W = 64
EPS = 1e-6
MLA_HEADS = 8
MLA_NOPE = 64
MLA_ROPE = 32
MLA_V = 64
MLA_Q_RANK = 384
MLA_KV_RANK = 256
MLA_SCALE = (MLA_NOPE + MLA_ROPE) ** -0.5
ROPE_BASE = 10000.0
CM_CHUNK = 128
CM_GROUPS = 4
CM_GROUP_DIM = 128
CM_WIDTH = CM_GROUPS * CM_GROUP_DIM
GLA_HEADS = 4
GLA_DK = 128
GLA_DV = 256
GLA_GATE_RANK = 16
GLA_TAU = 16.0
GLA_CHUNK = 64
GLA_PAIR = 2 * GLA_CHUNK
D_FF = 4 * D_MODEL

LANES = 128
HEAD_SLOT = LANES
BF16_ROWS = 16
V_SLOT = MLA_V + BF16_ROWS
VMEM_LIMIT = 56 * 1024 * 1024
MAX_KEY_TILE = 256
ATTN_UNROLL = 8
ATTN_SLOTS = 4
Q_PRESCALE = MLA_SCALE * 1.4426950408889634

EA_Q = 0
EA_KV = EA_Q + MLA_Q_RANK
EA_KR = EA_KV + MLA_KV_RANK
EA_U = EA_KR + LANES
EA_V = EA_U + CM_WIDTH
EA_END = EA_V + CM_WIDTH
OA_Z = 0
OA_R = OA_Z + LANES
OA_Q = OA_R + GLA_HEADS * GLA_DV
OA_K = OA_Q + GLA_HEADS * GLA_DK
OA_V = OA_K + GLA_HEADS * GLA_DK
OA_END = OA_V + GLA_HEADS * GLA_DV


def _cparams(sem):
    return pltpu.CompilerParams(dimension_semantics=sem, vmem_limit_bytes=VMEM_LIMIT)


def _rms(x, g):
    return x * lax.rsqrt(jnp.mean(x * x, axis=-1, keepdims=True) + EPS) * g


def _bdot(a, b):
    return jnp.dot(a.astype(BF16), b.astype(BF16), preferred_element_type=F32)


def _const_spec(shape):
    nd = len(shape)
    return pl.BlockSpec(shape, lambda *_: (0,) * nd, pipeline_mode=pl.Buffered(1))


def _ada_kernel(c_ref, w_ref, b_ref, o_ref):
    s = c_ref[...]
    s = s * jax.nn.sigmoid(s)
    o_ref[0] = _bdot(s, w_ref[0]) + b_ref[0]


def _ada_call(cvec, ada_w, ada_b):
    depth, d, n = ada_w.shape
    tn = 1536
    return pl.pallas_call(
        _ada_kernel,
        grid=(depth, n // tn),
        in_specs=[
            pl.BlockSpec((8, d), lambda l, j: (0, 0)),
            pl.BlockSpec((1, d, tn), lambda l, j: (l, 0, j)),
            pl.BlockSpec((1, 1, tn), lambda l, j: (l, 0, j)),
        ],
        out_specs=pl.BlockSpec((1, 8, tn), lambda l, j: (l, 0, j)),
        out_shape=jax.ShapeDtypeStruct((depth, 8, n), F32),
        compiler_params=_cparams(("arbitrary", "arbitrary")),
        name="ada_mod",
    )(cvec, ada_w, ada_b.reshape(depth, 1, n))


def _rope_slot(t, a, b):
    return t * a + pltpu.roll(t, HEAD_SLOT - MLA_ROPE, 1) * b


def _dot_nt(a, b):
    return lax.dot_general(a, b, (((1,), (1,)), ((), ())), preferred_element_type=F32)


def _even_in_kernel(*refs, tm, n_in):
    (x_ref, g1_ref, sh_ref, sc_ref, ta_ref, tb_ref, ct_ref, st_ref, wall_ref, qn_ref, wuqt_ref, kvn_ref, wuk_ref,
     wuvt_ref, cmn_ref, ws_ref, bias_ref) = refs[:17]
    qt_ref, k_ref, vt_ref, ml_ref = refs[n_in:]
    h = _rms(x_ref[...], g1_ref[...]) * (1.0 + sc_ref[0]) + sh_ref[0]
    p = _bdot(h, wall_ref[...])

    cq = _rms(p[:, EA_Q:EA_KV], qn_ref[...]).astype(BF16)
    qt = _dot_nt(wuqt_ref[...], cq)
    cos_t = ct_ref[...]
    sin_t = st_ref[...]
    r0, r1 = MLA_NOPE, MLA_NOPE + MLA_ROPE
    for hd in range(MLA_HEADS):
        b = hd * HEAD_SLOT
        qt_ref[b:b + r0, :] = (qt[b:b + r0, :] * Q_PRESCALE).astype(BF16)
        roped = qt[b + r0:b + r1, :] * cos_t + qt[b + r1:b + HEAD_SLOT, :] * sin_t
        qt_ref[b + r0:b + r1, :] = (roped * Q_PRESCALE).astype(BF16)
        qt_ref[b + r1:b + HEAD_SLOT, :] = jnp.zeros((HEAD_SLOT - r1, tm), BF16)

    ckv = _rms(p[:, EA_KV:EA_KR], kvn_ref[...]).astype(BF16)
    kn = jnp.dot(ckv, wuk_ref[...], preferred_element_type=F32)
    kr = _rope_slot(pltpu.roll(p[:, EA_KR:EA_U], MLA_NOPE, 1), ta_ref[...], tb_ref[...])
    for hd in range(MLA_HEADS):
        sl = slice(hd * HEAD_SLOT, (hd + 1) * HEAD_SLOT)
        k_ref[:, sl] = (kn[:, sl] + kr).astype(BF16)
    vt = _dot_nt(wuvt_ref[...], ckv)
    for hd in range(MLA_HEADS):
        b = hd * V_SLOT
        vt_ref[b:b + MLA_V, :] = vt[hd * MLA_V:(hd + 1) * MLA_V, :].astype(BF16)
        vt_ref[b + MLA_V:b + V_SLOT, :] = jnp.ones((V_SLOT - MLA_V, tm), BF16)

    u = jax.nn.gelu(p[:, EA_U:EA_V])
    vv = jax.nn.gelu(p[:, EA_V:EA_END])
    cmn = cmn_ref[...]
    for g in range(CM_GROUPS):
        gl = slice(g * CM_GROUP_DIM, (g + 1) * CM_GROUP_DIM)
        vn = _rms(vv[:, gl], cmn).astype(BF16)
        w = ws_ref[g]
        for c in range(tm // CM_CHUNK):
            rs = slice(c * CM_CHUNK, (c + 1) * CM_CHUNK)
            y = jnp.dot(w, vn[rs], preferred_element_type=F32) + bias_ref[:, gl]
            ml_ref[rs, gl] = (u[rs, gl] * y).astype(BF16)


def _even_in_call(x, g1, sh, sc, tabs, w, *, seq, tm, rows, kv_rows, kv_off, kv_bufs=None):
    ta, tb, cos_t, sin_t = tabs
    n, d = x.shape
    batch = n // rows
    tpb = seq // tm
    tpt = ta.shape[0] // tm
    tpr = rows // tm
    kv_blk = lambda i: (i // tpr) * (kv_rows // tm) + kv_off // tm + i % tpr
    row = lambda i: (i, 0)
    col = lambda i: (0, i)
    bat = lambda i: (i // tpb, 0, 0)
    tab = lambda i: (i % tpt, 0)
    tab_t = lambda i: (0, i % tpt)
    hq = MLA_HEADS * HEAD_SLOT
    hv = MLA_HEADS * MLA_V
    in_specs = [
        pl.BlockSpec((tm, d), row),
        _const_spec((1, d)),
        pl.BlockSpec((1, 1, d), bat),
        pl.BlockSpec((1, 1, d), bat),
        pl.BlockSpec((tm, HEAD_SLOT), tab),
        pl.BlockSpec((tm, HEAD_SLOT), tab),
        pl.BlockSpec((MLA_ROPE, tm), tab_t),
        pl.BlockSpec((MLA_ROPE, tm), tab_t),
        _const_spec((d, EA_END)),
        _const_spec((1, MLA_Q_RANK)),
        _const_spec((hq, MLA_Q_RANK)),
        _const_spec((1, MLA_KV_RANK)),
        _const_spec((MLA_KV_RANK, hq)),
        _const_spec((hv, MLA_KV_RANK)),
        _const_spec((1, CM_GROUP_DIM)),
        _const_spec((CM_GROUPS, CM_CHUNK, CM_CHUNK)),
        _const_spec((CM_CHUNK, CM_WIDTH)),
    ]
    args = [x, g1, sh, sc, ta, tb, cos_t, sin_t, w["wall"], w["qn"], w["wuqt"], w["kvn"], w["wuk"], w["wuvt"],
            w["cmn"], w["ws"], w["bias"]]
    aliases = {}
    if kv_bufs is not None:
        aliases = {len(args): 1, len(args) + 1: 2}
        in_specs += [pl.BlockSpec(memory_space=pl.ANY), pl.BlockSpec(memory_space=pl.ANY)]
        args += list(kv_bufs)
    return pl.pallas_call(
        functools.partial(_even_in_kernel, tm=tm, n_in=len(args)),
        grid=(n // tm,),
        in_specs=in_specs,
        out_specs=[
            pl.BlockSpec((hq, tm), col),
            pl.BlockSpec((tm, hq), lambda i: (kv_blk(i), 0)),
            pl.BlockSpec((MLA_HEADS * V_SLOT, tm), lambda i: (0, kv_blk(i))),
            pl.BlockSpec((tm, CM_WIDTH), row),
        ],
        out_shape=[
            jax.ShapeDtypeStruct((hq, n), BF16),
            jax.ShapeDtypeStruct((batch * kv_rows, hq), BF16),
            jax.ShapeDtypeStruct((MLA_HEADS * V_SLOT, batch * kv_rows), BF16),
            jax.ShapeDtypeStruct((n, CM_WIDTH), BF16),
        ],
        input_output_aliases=aliases,
        compiler_params=_cparams(("arbitrary",)),
        name="even_in",
    )(*args)


def _attn_kernel(qt_ref, k_ref, vt_ref, o_ref, s_scr, p_scr, acc_scr, *, tq, tk, nk, unroll):
    nsl = s_scr.shape[1]

    def rows(t):
        return t * tk if isinstance(t, int) else pl.multiple_of(t * tk, LANES)

    def step(t, ph, h, state, scores=True, softmax=True, weighted=True):
        m, alpha, tile_max = state
        new_alpha, new_tile_max = alpha, tile_max
        if scores:
            ks = k_ref[pl.ds(rows(t), tk), h * HEAD_SLOT:(h + 1) * HEAD_SLOT]
            s = jnp.dot(ks, qt_ref[h * HEAD_SLOT:(h + 1) * HEAD_SLOT, :], preferred_element_type=F32)
            s_scr[h, ph] = s
            new_tile_max = jnp.max(s, axis=0, keepdims=True)
        if weighted:
            vs = vt_ref[h * V_SLOT:(h + 1) * V_SLOT, pl.ds(rows(t - 2), tk)]
            acc_scr[h] = alpha * acc_scr[h] + jnp.dot(vs, p_scr[h, (ph - 2) % nsl], preferred_element_type=F32)
        if softmax:
            m_new = jnp.maximum(m, tile_max)
            new_alpha = jnp.exp2(m - m_new)
            p_scr[h, (ph - 1) % nsl] = jnp.exp2(s_scr[h, (ph - 1) % nsl] - m_new).astype(BF16)
            m = m_new
        return m, new_alpha, new_tile_max

    carry = []
    for h in range(2):
        acc_scr[h] = jnp.zeros((V_SLOT, tq), F32)
        neg = jnp.full((1, tq), -jnp.inf, F32)
        st = step(0, 0, h, (neg, jnp.zeros((1, tq), F32), neg), softmax=False, weighted=False)
        carry.append(step(1, 1 % nsl, h, st, weighted=False))

    assert unroll % nsl == 0
    groups = (nk - 2) // unroll

    def body(g, carry):
        t0 = 2 + unroll * g
        out = []
        for h in range(2):
            st = carry[h]
            for u in range(unroll):
                st = step(t0 + u, (2 + u) % nsl, h, st)
            out.append(st)
        return tuple(out)

    carry = lax.fori_loop(0, groups, body, tuple(carry))
    outs = []
    for h in range(2):
        st = carry[h]
        for t in range(2 + unroll * groups, nk):
            st = step(t, t % nsl, h, st)
        st = step(nk, nk % nsl, h, st, scores=False)
        step(nk + 1, (nk + 1) % nsl, h, st, scores=False, softmax=False)
        acc = acc_scr[h]
        outs.append(acc[0:MLA_V, :] * (1.0 / acc[MLA_V:MLA_V + 1, :]))
    o_ref[...] = jnp.concatenate(outs, axis=0).T.astype(BF16)


def _key_tile(t_len):
    tk = (min(t_len // 2, MAX_KEY_TILE) // LANES) * LANES
    while t_len % tk:
        tk -= LANES
    return tk


def _attn_call(qt, k, vt, *, batch, tq, t_len, t_off):
    n = qt.shape[1]
    kb = k.shape[0] // batch // t_len
    ko = t_off // t_len
    nq = n // batch // tq
    tk = _key_tile(t_len)
    return pl.pallas_call(
        functools.partial(_attn_kernel, tq=tq, tk=tk, nk=t_len // tk, unroll=ATTN_UNROLL),
        grid=(batch, MLA_HEADS // 2, nq),
        in_specs=[
            pl.BlockSpec((2 * HEAD_SLOT, tq), lambda b, hp, i: (hp, b * nq + i)),
            pl.BlockSpec((t_len, 2 * HEAD_SLOT), lambda b, hp, i: (b * kb + ko, hp)),
            pl.BlockSpec((2 * V_SLOT, t_len), lambda b, hp, i: (hp, b * kb + ko)),
        ],
        out_specs=pl.BlockSpec((tq, 2 * MLA_V), lambda b, hp, i: (b * nq + i, hp)),
        out_shape=jax.ShapeDtypeStruct((n, MLA_HEADS * MLA_V), BF16),
        scratch_shapes=[
            pltpu.VMEM((2, ATTN_SLOTS, tk, tq), F32),
            pltpu.VMEM((2, ATTN_SLOTS, tk, tq), BF16),
            pltpu.VMEM((2, V_SLOT, tq), F32),
        ],
        compiler_params=_cparams(("arbitrary", "arbitrary", "arbitrary")),
        name="mla_attn",
    )(qt, k, vt)


def _post_kernel(x_ref, a_ref, b_ref, gm_ref, g2_ref, sh_ref, sc_ref, gf_ref, wo_ref, w1_ref, w2_ref, fg_ref,
                 o_ref, *, final):
    half = wo_ref.shape[0] // 2
    y = (jnp.dot(a_ref[...], wo_ref[:half, :], preferred_element_type=F32)
         + jnp.dot(b_ref[...], wo_ref[half:, :], preferred_element_type=F32))
    x1 = x_ref[...] + gm_ref[0] * y
    h2 = (_rms(x1, g2_ref[...]) * (1.0 + sc_ref[0]) + sh_ref[0]).astype(BF16)
    acc = jnp.zeros_like(x1)
    fc = 1024
    for c in range(w1_ref.shape[1] // fc):
        hc = jnp.dot(h2, w1_ref[:, c * fc:(c + 1) * fc], preferred_element_type=F32)
        hc = jnp.square(jnp.maximum(hc, 0.0)).astype(BF16)
        acc = acc + jnp.dot(hc, w2_ref[c * fc:(c + 1) * fc, :], preferred_element_type=F32)
    x2 = x1 + gf_ref[0] * acc
    if final:
        x2 = _rms(x2, fg_ref[...])
    o_ref[...] = x2


def _post_call(x, mix_a, mix_b, col_a, col_b, gm, g2, sh, sc, gf, wo, w1, w2, fg, *, seq, tm, final):
    n, d = x.shape
    tpb = seq // tm
    row = lambda i: (i, 0)
    bat = lambda i: (i // tpb, 0, 0)
    half = wo.shape[0] // 2
    return pl.pallas_call(
        functools.partial(_post_kernel, final=final),
        grid=(n // tm,),
        in_specs=[
            pl.BlockSpec((tm, d), row),
            pl.BlockSpec((tm, half), lambda i: (i, col_a)),
            pl.BlockSpec((tm, half), lambda i: (i, col_b)),
            pl.BlockSpec((1, 1, d), bat),
            _const_spec((1, d)),
            pl.BlockSpec((1, 1, d), bat),
            pl.BlockSpec((1, 1, d), bat),
            pl.BlockSpec((1, 1, d), bat),
            _const_spec(wo.shape),
            _const_spec(w1.shape),
            _const_spec(w2.shape),
            _const_spec((1, d)),
        ],
        out_specs=pl.BlockSpec((tm, d), row),
        out_shape=jax.ShapeDtypeStruct((n, d), F32),
        compiler_params=_cparams(("arbitrary",)),
        name="post_mlp",
    )(x, mix_a, mix_b, gm, g2, sh, sc, gf, wo, w1, w2, fg)


def _log_decay(g):
    return (jnp.minimum(g, 0.0) - jnp.log1p(jnp.exp(-jnp.abs(g)))) * (1.0 / GLA_TAU)


def _chunk_cumsum(tri, x):
    hi = x.astype(BF16)
    r1 = x - hi.astype(F32)
    mid = r1.astype(BF16)
    lo = (r1 - mid.astype(F32)).astype(BF16)
    return (jnp.dot(tri, hi, preferred_element_type=F32) + jnp.dot(tri, mid, preferred_element_type=F32)
            + jnp.dot(tri, lo, preferred_element_type=F32))


def _odd_in_kernel(x_ref, g1_ref, sh_ref, sc_ref, wall_ref, wgf_ref, bgf_ref, wgb_ref, bgb_ref,
                   k_ref, v_ref, q_ref, r_ref, cf_ref, cb_ref, *, tm):
    rr = lax.broadcasted_iota(jnp.int32, (GLA_PAIR, GLA_PAIR), 0)
    cc = lax.broadcasted_iota(jnp.int32, (GLA_PAIR, GLA_PAIR), 1)
    same = (rr // GLA_CHUNK) == (cc // GLA_CHUNK)
    tri_f = (same & (rr >= cc)).astype(BF16)
    tri_b = (same & (rr <= cc)).astype(BF16)
    h = _rms(x_ref[...], g1_ref[...]) * (1.0 + sc_ref[0]) + sh_ref[0]
    p = _bdot(h, wall_ref[...])
    k_ref[...] = p[:, OA_K:OA_V]
    v_ref[...] = p[:, OA_V:OA_END].astype(BF16)
    q_ref[...] = p[:, OA_Q:OA_K] * (GLA_DK ** -0.5)
    r = p[:, OA_R:OA_Q]
    r_ref[...] = r * jax.nn.sigmoid(r)
    z = p[:, OA_Z:OA_R].astype(BF16)
    df = _log_decay(jnp.dot(z, wgf_ref[...], preferred_element_type=F32) + bgf_ref[...])
    db = _log_decay(jnp.dot(z, wgb_ref[...], preferred_element_type=F32) + bgb_ref[...])
    for c in range(tm // GLA_PAIR):
        rs = slice(c * GLA_PAIR, (c + 1) * GLA_PAIR)
        cf_ref[rs, :] = _chunk_cumsum(tri_f, df[rs])
        cb_ref[rs, :] = _chunk_cumsum(tri_b, db[rs])


def _odd_in_call(x, g1, sh, sc, w, *, seq, tm):
    n, d = x.shape
    tpb = seq // tm
    row = lambda i: (i, 0)
    bat = lambda i: (i // tpb, 0, 0)
    kd = GLA_HEADS * GLA_DK
    vd = GLA_HEADS * GLA_DV
    return pl.pallas_call(
        functools.partial(_odd_in_kernel, tm=tm),
        grid=(n // tm,),
        in_specs=[
            pl.BlockSpec((tm, d), row),
            _const_spec((1, d)),
            pl.BlockSpec((1, 1, d), bat),
            pl.BlockSpec((1, 1, d), bat),
            _const_spec((d, OA_END)),
            _const_spec((LANES, kd)),
            _const_spec((1, kd)),
            _const_spec((LANES, kd)),
            _const_spec((1, kd)),
        ],
        out_specs=[
            pl.BlockSpec((tm, kd), row),
            pl.BlockSpec((tm, vd), row),
            pl.BlockSpec((tm, kd), row),
            pl.BlockSpec((tm, vd), row),
            pl.BlockSpec((tm, kd), row),
            pl.BlockSpec((tm, kd), row),
        ],
        out_shape=[
            jax.ShapeDtypeStruct((n, kd), F32),
            jax.ShapeDtypeStruct((n, vd), BF16),
            jax.ShapeDtypeStruct((n, kd), F32),
            jax.ShapeDtypeStruct((n, vd), F32),
            jax.ShapeDtypeStruct((n, kd), F32),
            jax.ShapeDtypeStruct((n, kd), F32),
        ],
        compiler_params=_cparams(("arbitrary",)),
        name="odd_in",
    )(x, g1, sh, sc, w["wall"], w["wgf"], w["bgf"], w["wgb"], w["bgb"])


def _dot_tn(a, b):
    return lax.dot_general(a, b, (((0,), (0,)), ((), ())), preferred_element_type=F32)


def _gla_kernel(*refs, reverse, npair, combine):
    if combine:
        q_ref, k_ref, c_ref, v_ref, s0_ref, of_ref, r_ref, on_ref, o_ref, sfin_ref, s_scr = refs
    else:
        q_ref, k_ref, c_ref, v_ref, s0_ref, o_ref, sfin_ref, s_scr = refs
    i = pl.program_id(1)

    @pl.when(i == 0)
    def _():
        s_scr[...] = s0_ref[...]

    ch, pair = GLA_CHUNK, GLA_PAIR
    rr = lax.broadcasted_iota(jnp.int32, (pair, pair), 0)
    cc = lax.broadcasted_iota(jnp.int32, (pair, pair), 1)
    mask = ((rr // ch) == (cc // ch)) & ((rr <= cc) if reverse else (rr >= cc))
    lo = lax.broadcasted_iota(jnp.int32, (pair, GLA_DK), 0) < ch
    for pr in (range(npair - 1, -1, -1) if reverse else range(npair)):
        rs = slice(pr * pair, (pr + 1) * pair)
        for h in range(GLA_HEADS):
            ks = slice(h * GLA_DK, (h + 1) * GLA_DK)
            vs = slice(h * GLA_DV, (h + 1) * GLA_DV)
            bc = c_ref[rs, ks]
            k = k_ref[rs, ks]
            v = v_ref[rs, vs]
            if reverse:
                t_lo, t_hi = bc[0:1, :], bc[ch:ch + 1, :]
            else:
                t_lo, t_hi = bc[ch - 1:ch, :], bc[pair - 1:pair, :]
            q_t = (q_ref[rs, ks] * jnp.exp(bc)).astype(BF16)
            k_t = (k * jnp.exp(-bc)).astype(BF16)
            k_end = k * jnp.exp(jnp.where(lo, t_lo, t_hi) - bc)
            a = jnp.where(mask, _dot_nt(q_t, k_t), 0.0).astype(BF16)
            o = jnp.dot(a, v, preferred_element_type=F32)
            st = s_scr[h]
            halves = [(slice(0, ch), lo, t_lo), (slice(ch, pair), ~lo, t_hi)]
            inter = {}
            for hs, hm, tot in (halves[::-1] if reverse else halves):
                inter[hs.start] = _dot_nt(q_t[hs], st.astype(BF16))
                st = jnp.exp(tot) * st + _dot_tn(v, jnp.where(hm, k_end, 0.0).astype(BF16))
            s_scr[h] = st
            o = o + jnp.concatenate([inter[0], inter[ch]], axis=0)
            if combine:
                o = _rms(o + of_ref[rs, vs], on_ref[...]) * r_ref[rs, vs]
            o_ref[rs, vs] = o.astype(o_ref.dtype)

    @pl.when(i == pl.num_programs(1) - 1)
    def _():
        sfin_ref[...] = s_scr[...]


def _gla_call(q, k, c, v, s0, *, reverse, tb, o_fwd=None, r=None, o_norm=None):
    b, l, _ = q.shape
    nb = l // tb
    combine = o_fwd is not None
    blk = (lambda bi, i: (bi, nb - 1 - i, 0)) if reverse else (lambda bi, i: (bi, i, 0))
    kspec = pl.BlockSpec((None, tb, GLA_HEADS * GLA_DK), blk)
    vspec = pl.BlockSpec((None, tb, GLA_HEADS * GLA_DV), blk)
    sspec = pl.BlockSpec((None, GLA_HEADS, GLA_DV, GLA_DK), lambda bi, i: (bi, 0, 0, 0))
    in_specs = [kspec, kspec, kspec, vspec, sspec]
    args = [q, k, c, v, s0]
    if combine:
        in_specs += [vspec, vspec, pl.BlockSpec((1, GLA_DV), lambda bi, i: (0, 0))]
        args += [o_fwd, r, o_norm]
    return pl.pallas_call(
        functools.partial(_gla_kernel, reverse=reverse, npair=tb // GLA_PAIR, combine=combine),
        grid=(b, nb),
        in_specs=in_specs,
        out_specs=[vspec, sspec],
        out_shape=[
            jax.ShapeDtypeStruct((b, l, GLA_HEADS * GLA_DV), BF16 if combine else F32),
            jax.ShapeDtypeStruct((b, GLA_HEADS, GLA_DV, GLA_DK), F32),
        ],
        scratch_shapes=[pltpu.VMEM((GLA_HEADS, GLA_DV, GLA_DK), F32)],
        compiler_params=_cparams(("arbitrary", "arbitrary")),
        name="gla_bwd" if reverse else "gla_fwd",
    )(*args)


def _rot_cols(w):
    q = MLA_ROPE // 4
    return jnp.concatenate([-w[..., q:2 * q], w[..., 0:q], -w[..., 3 * q:4 * q], w[..., 2 * q:3 * q]], axis=-1)


def _prep_even(w_in, q_norm, w_uq, kv_norm, w_ukv, cm_norm, cm_ws, cm_bs):
    d = w_in.shape[0]
    e_q, e_kv = MLA_Q_RANK, MLA_Q_RANK + MLA_KV_RANK
    e_r = e_kv + MLA_ROPE
    e_u = e_r + CM_WIDTH
    wkr = w_in[:, e_kv:e_r]
    wall = jnp.concatenate([w_in[:, :e_kv], wkr, _rot_cols(wkr), jnp.zeros((d, LANES - 2 * MLA_ROPE), F32),
                            w_in[:, e_r:e_u], w_in[:, e_u:]], axis=1).astype(BF16)
    uq = w_uq.reshape(MLA_Q_RANK, MLA_HEADS, MLA_NOPE + MLA_ROPE)
    uq_r = uq[..., MLA_NOPE:]
    wuqt = jnp.concatenate([uq, _rot_cols(uq_r)], axis=-1).reshape(MLA_Q_RANK, MLA_HEADS * HEAD_SLOT).T.astype(BF16)
    ukv = w_ukv.reshape(MLA_KV_RANK, MLA_HEADS, MLA_NOPE + MLA_V)
    wuk = jnp.concatenate([ukv[..., :MLA_NOPE], jnp.zeros((MLA_KV_RANK, MLA_HEADS, HEAD_SLOT - MLA_NOPE), F32)],
                          axis=-1).reshape(MLA_KV_RANK, MLA_HEADS * HEAD_SLOT).astype(BF16)
    wuvt = ukv[..., MLA_NOPE:].reshape(MLA_KV_RANK, MLA_HEADS * MLA_V).T.astype(BF16)
    bias = jnp.repeat(cm_bs.T, CM_GROUP_DIM, axis=1)
    return dict(wall=wall, qn=q_norm[None], wuqt=wuqt, kvn=kv_norm[None], wuk=wuk, wuvt=wuvt, cmn=cm_norm[None],
                ws=cm_ws.astype(BF16), bias=bias)


def _prep_odd(w_in, w_gf, b_gf, w_gb, b_gb):
    d = w_in.shape[0]
    o_k = GLA_HEADS * GLA_DK
    o_v = o_k + GLA_HEADS * GLA_DV
    o_zb = o_v + 2 * GLA_GATE_RANK
    o_q = o_zb + GLA_HEADS * GLA_DK
    wall = jnp.concatenate([w_in[:, o_v:o_zb], jnp.zeros((d, LANES - 2 * GLA_GATE_RANK), F32), w_in[:, o_q:],
                            w_in[:, o_zb:o_q], w_in[:, :o_v]], axis=1).astype(BF16)
    zr = GLA_GATE_RANK
    wgf = jnp.zeros((LANES, o_k), F32).at[:zr].set(w_gf).astype(BF16)
    wgb = jnp.zeros((LANES, o_k), F32).at[zr:2 * zr].set(w_gb).astype(BF16)
    return dict(wall=wall, wgf=wgf, bgf=b_gf[None], wgb=wgb, bgb=b_gb[None])


def _rope_tables(length):
    rows = length // GRID_W
    r = jnp.repeat(jnp.arange(rows, dtype=F32), GRID_W)
    col = jnp.tile(jnp.arange(GRID_W, dtype=F32), rows)
    half = MLA_ROPE // 2
    inv = ROPE_BASE ** (-jnp.arange(0, half, 2, dtype=F32) / half)
    ang_r = r[:, None] * inv
    ang_c = col[:, None] * inv
    ang = jnp.concatenate([ang_r, ang_r, ang_c, ang_c], axis=-1)
    one = jnp.ones((length, MLA_NOPE), F32)
    pad = jnp.zeros((length, HEAD_SLOT - MLA_NOPE - MLA_ROPE), F32)
    ta = jnp.concatenate([one, jnp.cos(ang), pad], axis=1)
    tb = jnp.concatenate([0.0 * one, jnp.sin(ang), pad], axis=1)
    return ta, tb, jnp.cos(ang).T, jnp.sin(ang).T


def _flat_tables(length):
    ta = jnp.concatenate([jnp.ones((length, MLA_NOPE + MLA_ROPE), F32),
                          jnp.zeros((length, HEAD_SLOT - MLA_NOPE - MLA_ROPE), F32)], axis=1)
    return ta, jnp.zeros_like(ta), jnp.ones((MLA_ROPE, length), F32), jnp.zeros((MLA_ROPE, length), F32)


def _row_tile(seq, want):
    t = min(seq, want)
    while seq % t:
        t //= 2
    return t


def kernel(x, c, ctx, c_ctx, ada_w, ada_b, norm1_g, norm2_g, mlp_w1, mlp_w2, ev_w_in, ev_q_norm, ev_w_uq, ev_kv_norm,
           ev_w_ukv, ev_cm_norm, ev_cm_ws, ev_cm_bs, ev_w_out, od_w_in, od_w_gf, od_b_gf, od_w_gb, od_b_gb, od_o_norm,
           od_w_out, final_g):
    batch, seq, d = x.shape
    lc = ctx.shape[1]
    depth = ada_w.shape[0]
    tm_l = _row_tile(seq, 512)
    tm_c = _row_tile(lc, 256)
    tm_e = tm_c
    assert seq % tm_e == 0 and lc % tm_e == 0 and seq % lc == 0
    tq_l = _row_tile(seq, 512)
    tq_c = _row_tile(lc, 256)
    tb_l = _row_tile(seq, 256)
    tb_c = _row_tile(lc, 256)

    cvec = jnp.concatenate([c, c_ctx[None], jnp.zeros((8 - batch - 1, d), F32)], axis=0)
    mods = _ada_call(cvec, ada_w, ada_b).reshape(depth, 8, 6, d)

    tabs_l = _rope_tables(seq)
    tabs_c = _flat_tables(lc)
    hq = MLA_HEADS * HEAD_SLOT

    xl = x.reshape(batch * seq, d)
    xc = ctx.reshape(batch * lc, d)
    for i in range(depth):
        need_ctx = i < depth - 1
        j = i // 2
        ml = [mods[i, :batch, t][:, None, :] for t in range(6)]
        mc = [mods[i, batch:batch + 1, t][:, None, :] for t in range(6)]
        g1 = norm1_g[i][None]
        g2 = norm2_g[i][None]
        w1 = mlp_w1[i].astype(BF16)
        w2 = mlp_w2[i].astype(BF16)
        final = i == depth - 1
        if i % 2 == 0:
            w = _prep_even(ev_w_in[j], ev_q_norm[j], ev_w_uq[j], ev_kv_norm[j], ev_w_ukv[j], ev_cm_norm[j],
                           ev_cm_ws[j], ev_cm_bs[j])
            wo = ev_w_out[j].astype(BF16)
            kv_rows = seq + lc
            qtl, k_all, vt_all, mll = _even_in_call(xl, g1, ml[0], ml[1], tabs_l, w, seq=seq, tm=tm_e, rows=seq,
                                                    kv_rows=kv_rows, kv_off=0)
            qtc, k_all, vt_all, mlc = _even_in_call(xc, g1, mc[0], mc[1], tabs_c, w, seq=batch * lc, tm=tm_e, rows=lc,
                                                    kv_rows=kv_rows, kv_off=seq, kv_bufs=(k_all, vt_all))
            al = _attn_call(qtl, k_all, vt_all, batch=batch, tq=tq_l, t_len=kv_rows, t_off=0)
            xl = _post_call(xl, al, mll, 0, 0, ml[2], g2, ml[3], ml[4], ml[5], wo, w1, w2, final_g[None],
                            seq=seq, tm=tm_l, final=final)
            if need_ctx:
                ac = _attn_call(qtc, k_all, vt_all, batch=batch, tq=tq_c, t_len=lc, t_off=seq)
                xc = _post_call(xc, ac, mlc, 0, 0, mc[2], g2, mc[3], mc[4], mc[5], wo, w1, w2, final_g[None],
                                seq=batch * lc, tm=tm_c, final=False)
        else:
            w = _prep_odd(od_w_in[j], od_w_gf[j], od_b_gf[j], od_w_gb[j], od_b_gb[j])
            wo = od_w_out[j].astype(BF16)
            on = od_o_norm[j][None]
            kc, vc, qc, rc, dfc, dbc = _odd_in_call(xc, g1, mc[0], mc[1], w, seq=batch * lc, tm=tm_c)
            kl, vl, ql, rl, dfl, dbl = _odd_in_call(xl, g1, ml[0], ml[1], w, seq=seq, tm=tm_l)
            r3 = lambda t, n: t.reshape(batch, n, t.shape[-1])
            s0 = jnp.zeros((batch, GLA_HEADS, GLA_DV, GLA_DK), F32)
            ocf, s_f = _gla_call(r3(qc, lc), r3(kc, lc), r3(dfc, lc), r3(vc, lc), s0, reverse=False, tb=tb_c)
            mixc, s_b = _gla_call(r3(qc, lc), r3(kc, lc), r3(dbc, lc), r3(vc, lc), s0, reverse=True, tb=tb_c,
                                  o_fwd=ocf, r=r3(rc, lc), o_norm=on)
            olf, _ = _gla_call(r3(ql, seq), r3(kl, seq), r3(dfl, seq), r3(vl, seq), s_f, reverse=False, tb=tb_l)
            mixl, _ = _gla_call(r3(ql, seq), r3(kl, seq), r3(dbl, seq), r3(vl, seq), s_b, reverse=True, tb=tb_l,
                                o_fwd=olf, r=r3(rl, seq), o_norm=on)
            mixl = mixl.reshape(batch * seq, -1)
            xl = _post_call(xl, mixl, mixl, 0, 1, ml[2], g2, ml[3], ml[4], ml[5], wo, w1, w2, final_g[None],
                            seq=seq, tm=tm_l, final=final)
            if need_ctx:
                mixc = mixc.reshape(batch * lc, -1)
                xc = _post_call(xc, mixc, mixc, 0, 1, mc[2], g2, mc[3], mc[4], mc[5], wo, w1, w2, final_g[None],
                                seq=batch * lc, tm=tm_c, final=False)
    return xl.reshape(batch, seq, d)
```

```python
import functools

import jax
import jax.numpy as jnp
from jax import lax
from jax.experimental import pallas as pl
from jax.experimental.pallas import tpu as pltpu

F32 = jnp.float32
BF16 = jnp.bfloat16

D_MODEL = 1024
DEPTH = 4
GRID_W = 64
EPS = 1e-6
MLA_HEADS = 8
MLA_NOPE = 64
MLA_ROPE = 32
MLA_V = 64
MLA_Q_RANK = 384
MLA_KV_RANK = 256
MLA_SCALE = (MLA_NOPE + MLA_ROPE) ** -0.5
ROPE_BASE = 10000.0
CM_CHUNK = 128
CM_GROUPS = 4
CM_GROUP_DIM = 128
CM_WIDTH = CM_GROUPS * CM_GROUP_DIM
GLA_HEADS = 4
GLA_DK = 128
GLA_DV = 256
GLA_GATE_RANK = 16
GLA_TAU = 16.0
GLA_CHUNK = 64
GLA_PAIR = 2 * GLA_CHUNK
D_FF = 4 * D_MODEL

LANES = 128
HEAD_SLOT = LANES
BF16_ROWS = 16
V_SLOT = MLA_V + BF16_ROWS
VMEM_LIMIT = 56 * 1024 * 1024
MAX_KEY_TILE = 256
ATTN_UNROLL = 8
ATTN_SLOTS = 4
Q_PRESCALE = MLA_SCALE * 1.4426950408889634

EA_Q = 0
EA_KV = EA_Q + MLA_Q_RANK
EA_KR = EA_KV + MLA_KV_RANK
EA_U = EA_KR + LANES
EA_V = EA_U + CM_WIDTH
EA_END = EA_V + CM_WIDTH
OA_Z = 0
OA_R = OA_Z + LANES
OA_Q = OA_R + GLA_HEADS * GLA_DV
OA_K = OA_Q + GLA_HEADS * GLA_DK
OA_V = OA_K + GLA_HEADS * GLA_DK
OA_END = OA_V + GLA_HEADS * GLA_DV


def _cparams(sem):
    return pltpu.CompilerParams(dimension_semantics=sem, vmem_limit_bytes=VMEM_LIMIT)


def _rms(x, g):
    return x * lax.rsqrt(jnp.mean(x * x, axis=-1, keepdims=True) + EPS) * g


def _bdot(a, b):
    return jnp.dot(a.astype(BF16), b.astype(BF16), preferred_element_type=F32)


def _const_spec(shape):
    nd = len(shape)
    return pl.BlockSpec(shape, lambda *_: (0,) * nd, pipeline_mode=pl.Buffered(1))


def _ada_kernel(c_ref, w_ref, b_ref, o_ref):
    s = c_ref[...]
    s = s * jax.nn.sigmoid(s)
    o_ref[0] = _bdot(s, w_ref[0]) + b_ref[0]


def _ada_call(cvec, ada_w, ada_b):
    depth, d, n = ada_w.shape
    tn = 1536
    return pl.pallas_call(
        _ada_kernel,
        grid=(depth, n // tn),
        in_specs=[
            pl.BlockSpec((8, d), lambda l, j: (0, 0)),
            pl.BlockSpec((1, d, tn), lambda l, j: (l, 0, j)),
            pl.BlockSpec((1, 1, tn), lambda l, j: (l, 0, j)),
        ],
        out_specs=pl.BlockSpec((1, 8, tn), lambda l, j: (l, 0, j)),
        out_shape=jax.ShapeDtypeStruct((depth, 8, n), F32),
        compiler_params=_cparams(("arbitrary", "arbitrary")),
        name="ada_mod",
    )(cvec, ada_w, ada_b.reshape(depth, 1, n))


def _rope_slot(t, a, b):
    return t * a + pltpu.roll(t, HEAD_SLOT - MLA_ROPE, 1) * b


def _dot_nt(a, b):
    return lax.dot_general(a, b, (((1,), (1,)), ((), ())), preferred_element_type=F32)


def _even_in_kernel(x_ref, g1_ref, sh_ref, sc_ref, ta_ref, tb_ref, ct_ref, st_ref, wall_ref, qn_ref, wuqt_ref,
                    kvn_ref, wuk_ref, wuvt_ref, cmn_ref, ws_ref, bias_ref, qt_ref, k_ref, vt_ref, ml_ref, *, tm):
    h = _rms(x_ref[...], g1_ref[...]) * (1.0 + sc_ref[0]) + sh_ref[0]
    p = _bdot(h, wall_ref[...])

    cq = _rms(p[:, EA_Q:EA_KV], qn_ref[...]).astype(BF16)
    qt = _dot_nt(wuqt_ref[...], cq)
    cos_t = ct_ref[...]
    sin_t = st_ref[...]
    r0, r1 = MLA_NOPE, MLA_NOPE + MLA_ROPE
    for hd in range(MLA_HEADS):
        b = hd * HEAD_SLOT
        qt_ref[b:b + r0, :] = (qt[b:b + r0, :] * Q_PRESCALE).astype(BF16)
        roped = qt[b + r0:b + r1, :] * cos_t + qt[b + r1:b + HEAD_SLOT, :] * sin_t
        qt_ref[b + r0:b + r1, :] = (roped * Q_PRESCALE).astype(BF16)
        qt_ref[b + r1:b + HEAD_SLOT, :] = jnp.zeros((HEAD_SLOT - r1, tm), BF16)

    ckv = _rms(p[:, EA_KV:EA_KR], kvn_ref[...]).astype(BF16)
    kn = jnp.dot(ckv, wuk_ref[...], preferred_element_type=F32)
    kr = _rope_slot(pltpu.roll(p[:, EA_KR:EA_U], MLA_NOPE, 1), ta_ref[...], tb_ref[...])
    for hd in range(MLA_HEADS):
        sl = slice(hd * HEAD_SLOT, (hd + 1) * HEAD_SLOT)
        k_ref[:, sl] = (kn[:, sl] + kr).astype(BF16)
    vt = _dot_nt(wuvt_ref[...], ckv)
    for hd in range(MLA_HEADS):
        b = hd * V_SLOT
        vt_ref[b:b + MLA_V, :] = vt[hd * MLA_V:(hd + 1) * MLA_V, :].astype(BF16)
        vt_ref[b + MLA_V:b + V_SLOT, :] = jnp.ones((V_SLOT - MLA_V, tm), BF16)

    u = jax.nn.gelu(p[:, EA_U:EA_V])
    vv = jax.nn.gelu(p[:, EA_V:EA_END])
    cmn = cmn_ref[...]
    for g in range(CM_GROUPS):
        gl = slice(g * CM_GROUP_DIM, (g + 1) * CM_GROUP_DIM)
        vn = _rms(vv[:, gl], cmn).astype(BF16)
        w = ws_ref[g]
        for c in range(tm // CM_CHUNK):
            rs = slice(c * CM_CHUNK, (c + 1) * CM_CHUNK)
            y = jnp.dot(w, vn[rs], preferred_element_type=F32) + bias_ref[:, gl]
            ml_ref[rs, gl] = (u[rs, gl] * y).astype(BF16)


def _even_in_call(x, g1, sh, sc, tabs, w, *, seq, tm):
    ta, tb, cos_t, sin_t = tabs
    n, d = x.shape
    tpb = seq // tm
    tpt = ta.shape[0] // tm
    row = lambda i: (i, 0)
    col = lambda i: (0, i)
    bat = lambda i: (i // tpb, 0, 0)
    tab = lambda i: (i % tpt, 0)
    tab_t = lambda i: (0, i % tpt)
    hq = MLA_HEADS * HEAD_SLOT
    hv = MLA_HEADS * MLA_V
    in_specs = [
        pl.BlockSpec((tm, d), row),
        _const_spec((1, d)),
        pl.BlockSpec((1, 1, d), bat),
        pl.BlockSpec((1, 1, d), bat),
        pl.BlockSpec((tm, HEAD_SLOT), tab),
        pl.BlockSpec((tm, HEAD_SLOT), tab),
        pl.BlockSpec((MLA_ROPE, tm), tab_t),
        pl.BlockSpec((MLA_ROPE, tm), tab_t),
        _const_spec((d, EA_END)),
        _const_spec((1, MLA_Q_RANK)),
        _const_spec((hq, MLA_Q_RANK)),
        _const_spec((1, MLA_KV_RANK)),
        _const_spec((MLA_KV_RANK, hq)),
        _const_spec((hv, MLA_KV_RANK)),
        _const_spec((1, CM_GROUP_DIM)),
        _const_spec((CM_GROUPS, CM_CHUNK, CM_CHUNK)),
        _const_spec((CM_CHUNK, CM_WIDTH)),
    ]
    args = [x, g1, sh, sc, ta, tb, cos_t, sin_t, w["wall"], w["qn"], w["wuqt"], w["kvn"], w["wuk"], w["wuvt"],
            w["cmn"], w["ws"], w["bias"]]
    return pl.pallas_call(
        functools.partial(_even_in_kernel, tm=tm),
        grid=(n // tm,),
        in_specs=in_specs,
        out_specs=[
            pl.BlockSpec((hq, tm), col),
            pl.BlockSpec((tm, hq), row),
            pl.BlockSpec((MLA_HEADS * V_SLOT, tm), col),
            pl.BlockSpec((tm, CM_WIDTH), row),
        ],
        out_shape=[
            jax.ShapeDtypeStruct((hq, n), BF16),
            jax.ShapeDtypeStruct((n, hq), BF16),
            jax.ShapeDtypeStruct((MLA_HEADS * V_SLOT, n), BF16),
            jax.ShapeDtypeStruct((n, CM_WIDTH), BF16),
        ],
        compiler_params=_cparams(("arbitrary",)),
        name="even_in",
    )(*args)


def _attn_kernel(*refs, tq, tk, seg_tiles, unroll):
    nseg = len(seg_tiles)
    qt_ref = refs[0]
    kv_refs = [(refs[1 + 2 * i], refs[2 + 2 * i]) for i in range(nseg)]
    o_ref, s_scr, p_scr, acc_scr = refs[1 + 2 * nseg:]
    nsl = s_scr.shape[1]
    nk = sum(seg_tiles)

    def tile(t):
        if not isinstance(t, int):
            return kv_refs[0] + (pl.multiple_of(t * tk, LANES),)
        for (k_ref, vt_ref), n in zip(kv_refs, seg_tiles):
            if t < n:
                return k_ref, vt_ref, t * tk
            t -= n
        raise ValueError("key tile out of range")

    def step(t, ph, h, state, scores=True, softmax=True, weighted=True):
        m, alpha, tile_max = state
        new_alpha, new_tile_max = alpha, tile_max
        if scores:
            k_ref, _, r0 = tile(t)
            ks = k_ref[pl.ds(r0, tk), h * HEAD_SLOT:(h + 1) * HEAD_SLOT]
            s = jnp.dot(ks, qt_ref[h * HEAD_SLOT:(h + 1) * HEAD_SLOT, :], preferred_element_type=F32)
            s_scr[h, ph] = s
            new_tile_max = jnp.max(s, axis=0, keepdims=True)
        if weighted:
            _, vt_ref, r0 = tile(t - 2)
            vs = vt_ref[h * V_SLOT:(h + 1) * V_SLOT, pl.ds(r0, tk)]
            acc_scr[h] = alpha * acc_scr[h] + jnp.dot(vs, p_scr[h, (ph - 2) % nsl], preferred_element_type=F32)
        if softmax:
            m_new = jnp.maximum(m, tile_max)
            new_alpha = jnp.exp2(m - m_new)
            p_scr[h, (ph - 1) % nsl] = jnp.exp2(s_scr[h, (ph - 1) % nsl] - m_new).astype(BF16)
            m = m_new
        return m, new_alpha, new_tile_max

    carry = []
    for h in range(2):
        acc_scr[h] = jnp.zeros((V_SLOT, tq), F32)
        neg = jnp.full((1, tq), -jnp.inf, F32)
        st = step(0, 0, h, (neg, jnp.zeros((1, tq), F32), neg), softmax=False, weighted=False)
        carry.append(step(1, 1 % nsl, h, st, weighted=False))

    assert unroll % nsl == 0
    groups = max(seg_tiles[0] - 2, 0) // unroll

    def body(g, carry):
        t0 = 2 + unroll * g
        out = []
        for h in range(2):
            st = carry[h]
            for u in range(unroll):
                st = step(t0 + u, (2 + u) % nsl, h, st)
            out.append(st)
        return tuple(out)

    carry = lax.fori_loop(0, groups, body, tuple(carry))
    outs = []
    for h in range(2):
        st = carry[h]
        for t in range(2 + unroll * groups, nk):
            st = step(t, t % nsl, h, st)
        st = step(nk, nk % nsl, h, st, scores=False)
        step(nk + 1, (nk + 1) % nsl, h, st, scores=False, softmax=False)
        acc = acc_scr[h]
        outs.append(acc[0:MLA_V, :] * (1.0 / acc[MLA_V:MLA_V + 1, :]))
    o_ref[...] = jnp.concatenate(outs, axis=0).T.astype(BF16)


def _key_tile(t_len):
    tk = (min(t_len // 2, MAX_KEY_TILE) // LANES) * LANES
    while t_len % tk:
        tk -= LANES
    return tk


def _attn_call(qt, kvs, *, batch, tq, tk):
    n = qt.shape[1]
    nq = n // batch // tq
    in_specs = [pl.BlockSpec((2 * HEAD_SLOT, tq), lambda b, hp, i: (hp, b * nq + i))]
    args = [qt]
    seg_tiles = []
    for k, vt in kvs:
        t_len = k.shape[0] // batch
        assert t_len % tk == 0
        seg_tiles.append(t_len // tk)
        in_specs.append(pl.BlockSpec((t_len, 2 * HEAD_SLOT), lambda b, hp, i: (b, hp)))
        in_specs.append(pl.BlockSpec((2 * V_SLOT, t_len), lambda b, hp, i: (hp, b)))
        args += [k, vt]
    assert sum(seg_tiles) >= 2
    return pl.pallas_call(
        functools.partial(_attn_kernel, tq=tq, tk=tk, seg_tiles=tuple(seg_tiles), unroll=ATTN_UNROLL),
        grid=(batch, MLA_HEADS // 2, nq),
        in_specs=in_specs,
        out_specs=pl.BlockSpec((tq, 2 * MLA_V), lambda b, hp, i: (b * nq + i, hp)),
        out_shape=jax.ShapeDtypeStruct((n, MLA_HEADS * MLA_V), BF16),
        scratch_shapes=[
            pltpu.VMEM((2, ATTN_SLOTS, tk, tq), F32),
            pltpu.VMEM((2, ATTN_SLOTS, tk, tq), BF16),
            pltpu.VMEM((2, V_SLOT, tq), F32),
        ],
        compiler_params=_cparams(("arbitrary", "arbitrary", "arbitrary")),
        name="mla_attn",
    )(*args)


def _post_kernel(x_ref, a_ref, b_ref, gm_ref, g2_ref, sh_ref, sc_ref, gf_ref, wo_ref, w1_ref, w2_ref, fg_ref,
                 o_ref, *, final):
    half = wo_ref.shape[0] // 2
    y = (jnp.dot(a_ref[...], wo_ref[:half, :], preferred_element_type=F32)
         + jnp.dot(b_ref[...], wo_ref[half:, :], preferred_element_type=F32))
    x1 = x_ref[...] + gm_ref[0] * y
    h2 = (_rms(x1, g2_ref[...]) * (1.0 + sc_ref[0]) + sh_ref[0]).astype(BF16)
    acc = jnp.zeros_like(x1)
    fc = 1024
    for c in range(w1_ref.shape[1] // fc):
        hc = jnp.dot(h2, w1_ref[:, c * fc:(c + 1) * fc], preferred_element_type=F32)
        hc = jnp.square(jnp.maximum(hc, 0.0)).astype(BF16)
        acc = acc + jnp.dot(hc, w2_ref[c * fc:(c + 1) * fc, :], preferred_element_type=F32)
    x2 = x1 + gf_ref[0] * acc
    if final:
        x2 = _rms(x2, fg_ref[...])
    o_ref[...] = x2


def _post_call(x, mix_a, mix_b, col_a, col_b, gm, g2, sh, sc, gf, wo, w1, w2, fg, *, seq, tm, final):
    n, d = x.shape
    tpb = seq // tm
    row = lambda i: (i, 0)
    bat = lambda i: (i // tpb, 0, 0)
    half = wo.shape[0] // 2
    return pl.pallas_call(
        functools.partial(_post_kernel, final=final),
        grid=(n // tm,),
        in_specs=[
            pl.BlockSpec((tm, d), row),
            pl.BlockSpec((tm, half), lambda i: (i, col_a)),
            pl.BlockSpec((tm, half), lambda i: (i, col_b)),
            pl.BlockSpec((1, 1, d), bat),
            _const_spec((1, d)),
            pl.BlockSpec((1, 1, d), bat),
            pl.BlockSpec((1, 1, d), bat),
            pl.BlockSpec((1, 1, d), bat),
            _const_spec(wo.shape),
            _const_spec(w1.shape),
            _const_spec(w2.shape),
            _const_spec((1, d)),
        ],
        out_specs=pl.BlockSpec((tm, d), row),
        out_shape=jax.ShapeDtypeStruct((n, d), F32),
        compiler_params=_cparams(("arbitrary",)),
        name="post_mlp",
    )(x, mix_a, mix_b, gm, g2, sh, sc, gf, wo, w1, w2, fg)


def _log_decay(g):
    return (jnp.minimum(g, 0.0) - jnp.log1p(jnp.exp(-jnp.abs(g)))) * (1.0 / GLA_TAU)


def _chunk_cumsum(tri, x):
    hi = x.astype(BF16)
    r1 = x - hi.astype(F32)
    mid = r1.astype(BF16)
    lo = (r1 - mid.astype(F32)).astype(BF16)
    return (jnp.dot(tri, hi, preferred_element_type=F32) + jnp.dot(tri, mid, preferred_element_type=F32)
            + jnp.dot(tri, lo, preferred_element_type=F32))


def _odd_in_kernel(x_ref, g1_ref, sh_ref, sc_ref, wall_ref, wgf_ref, bgf_ref, wgb_ref, bgb_ref,
                   k_ref, v_ref, q_ref, r_ref, cf_ref, cb_ref, *, tm):
    rr = lax.broadcasted_iota(jnp.int32, (GLA_PAIR, GLA_PAIR), 0)
    cc = lax.broadcasted_iota(jnp.int32, (GLA_PAIR, GLA_PAIR), 1)
    same = (rr // GLA_CHUNK) == (cc // GLA_CHUNK)
    tri_f = (same & (rr >= cc)).astype(BF16)
    tri_b = (same & (rr <= cc)).astype(BF16)
    h = _rms(x_ref[...], g1_ref[...]) * (1.0 + sc_ref[0]) + sh_ref[0]
    p = _bdot(h, wall_ref[...])
    k_ref[...] = p[:, OA_K:OA_V]
    v_ref[...] = p[:, OA_V:OA_END].astype(BF16)
    q_ref[...] = p[:, OA_Q:OA_K] * (GLA_DK ** -0.5)
    r = p[:, OA_R:OA_Q]
    r_ref[...] = r * jax.nn.sigmoid(r)
    z = p[:, OA_Z:OA_R].astype(BF16)
    df = _log_decay(jnp.dot(z, wgf_ref[...], preferred_element_type=F32) + bgf_ref[...])
    db = _log_decay(jnp.dot(z, wgb_ref[...], preferred_element_type=F32) + bgb_ref[...])
    for c in range(tm // GLA_PAIR):
        rs = slice(c * GLA_PAIR, (c + 1) * GLA_PAIR)
        cf_ref[rs, :] = _chunk_cumsum(tri_f, df[rs])
        cb_ref[rs, :] = _chunk_cumsum(tri_b, db[rs])


def _odd_in_call(x, g1, sh, sc, w, *, seq, tm):
    n, d = x.shape
    tpb = seq // tm
    row = lambda i: (i, 0)
    bat = lambda i: (i // tpb, 0, 0)
    kd = GLA_HEADS * GLA_DK
    vd = GLA_HEADS * GLA_DV
    return pl.pallas_call(
        functools.partial(_odd_in_kernel, tm=tm),
        grid=(n // tm,),
        in_specs=[
            pl.BlockSpec((tm, d), row),
            _const_spec((1, d)),
            pl.BlockSpec((1, 1, d), bat),
            pl.BlockSpec((1, 1, d), bat),
            _const_spec((d, OA_END)),
            _const_spec((LANES, kd)),
            _const_spec((1, kd)),
            _const_spec((LANES, kd)),
            _const_spec((1, kd)),
        ],
        out_specs=[
            pl.BlockSpec((tm, kd), row),
            pl.BlockSpec((tm, vd), row),
            pl.BlockSpec((tm, kd), row),
            pl.BlockSpec((tm, vd), row),
            pl.BlockSpec((tm, kd), row),
            pl.BlockSpec((tm, kd), row),
        ],
        out_shape=[
            jax.ShapeDtypeStruct((n, kd), F32),
            jax.ShapeDtypeStruct((n, vd), BF16),
            jax.ShapeDtypeStruct((n, kd), F32),
            jax.ShapeDtypeStruct((n, vd), F32),
            jax.ShapeDtypeStruct((n, kd), F32),
            jax.ShapeDtypeStruct((n, kd), F32),
        ],
        compiler_params=_cparams(("arbitrary",)),
        name="odd_in",
    )(x, g1, sh, sc, w["wall"], w["wgf"], w["bgf"], w["wgb"], w["bgb"])


def _dot_tn(a, b):
    return lax.dot_general(a, b, (((0,), (0,)), ((), ())), preferred_element_type=F32)


def _gla_kernel(*refs, reverse, npair, combine):
    if combine:
        q_ref, k_ref, c_ref, v_ref, s0_ref, of_ref, r_ref, on_ref, o_ref, sfin_ref, s_scr = refs
    else:
        q_ref, k_ref, c_ref, v_ref, s0_ref, o_ref, sfin_ref, s_scr = refs
    i = pl.program_id(1)

    @pl.when(i == 0)
    def _():
        s_scr[...] = s0_ref[...]

    ch, pair = GLA_CHUNK, GLA_PAIR
    rr = lax.broadcasted_iota(jnp.int32, (pair, pair), 0)
    cc = lax.broadcasted_iota(jnp.int32, (pair, pair), 1)
    mask = ((rr // ch) == (cc // ch)) & ((rr <= cc) if reverse else (rr >= cc))
    lo = lax.broadcasted_iota(jnp.int32, (pair, GLA_DK), 0) < ch
    for pr in (range(npair - 1, -1, -1) if reverse else range(npair)):
        rs = slice(pr * pair, (pr + 1) * pair)
        for h in range(GLA_HEADS):
            ks = slice(h * GLA_DK, (h + 1) * GLA_DK)
            vs = slice(h * GLA_DV, (h + 1) * GLA_DV)
            bc = c_ref[rs, ks]
            k = k_ref[rs, ks]
            v = v_ref[rs, vs]
            if reverse:
                t_lo, t_hi = bc[0:1, :], bc[ch:ch + 1, :]
            else:
                t_lo, t_hi = bc[ch - 1:ch, :], bc[pair - 1:pair, :]
            q_t = (q_ref[rs, ks] * jnp.exp(bc)).astype(BF16)
            k_t = (k * jnp.exp(-bc)).astype(BF16)
            k_end = k * jnp.exp(jnp.where(lo, t_lo, t_hi) - bc)
            a = jnp.where(mask, _dot_nt(q_t, k_t), 0.0).astype(BF16)
            o = jnp.dot(a, v, preferred_element_type=F32)
            st = s_scr[h]
            halves = [(slice(0, ch), lo, t_lo), (slice(ch, pair), ~lo, t_hi)]
            inter = {}
            for hs, hm, tot in (halves[::-1] if reverse else halves):
                inter[hs.start] = _dot_nt(q_t[hs], st.astype(BF16))
                st = jnp.exp(tot) * st + _dot_tn(v, jnp.where(hm, k_end, 0.0).astype(BF16))
            s_scr[h] = st
            o = o + jnp.concatenate([inter[0], inter[ch]], axis=0)
            if combine:
                o = _rms(o + of_ref[rs, vs], on_ref[...]) * r_ref[rs, vs]
            o_ref[rs, vs] = o.astype(o_ref.dtype)

    @pl.when(i == pl.num_programs(1) - 1)
    def _():
        sfin_ref[...] = s_scr[...]


def _gla_call(q, k, c, v, s0, *, reverse, tb, o_fwd=None, r=None, o_norm=None):
    b, l, _ = q.shape
    nb = l // tb
    combine = o_fwd is not None
    blk = (lambda bi, i: (bi, nb - 1 - i, 0)) if reverse else (lambda bi, i: (bi, i, 0))
    kspec = pl.BlockSpec((None, tb, GLA_HEADS * GLA_DK), blk)
    vspec = pl.BlockSpec((None, tb, GLA_HEADS * GLA_DV), blk)
    sspec = pl.BlockSpec((None, GLA_HEADS, GLA_DV, GLA_DK), lambda bi, i: (bi, 0, 0, 0))
    in_specs = [kspec, kspec, kspec, vspec, sspec]
    args = [q, k, c, v, s0]
    if combine:
        in_specs += [vspec, vspec, pl.BlockSpec((1, GLA_DV), lambda bi, i: (0, 0))]
        args += [o_fwd, r, o_norm]
    return pl.pallas_call(
        functools.partial(_gla_kernel, reverse=reverse, npair=tb // GLA_PAIR, combine=combine),
        grid=(b, nb),
        in_specs=in_specs,
        out_specs=[vspec, sspec],
        out_shape=[
            jax.ShapeDtypeStruct((b, l, GLA_HEADS * GLA_DV), BF16 if combine else F32),
            jax.ShapeDtypeStruct((b, GLA_HEADS, GLA_DV, GLA_DK), F32),
        ],
        scratch_shapes=[pltpu.VMEM((GLA_HEADS, GLA_DV, GLA_DK), F32)],
        compiler_params=_cparams(("arbitrary", "arbitrary")),
        name="gla_bwd" if reverse else "gla_fwd",
    )(*args)


def _rot_cols(w):
    q = MLA_ROPE // 4
    return jnp.concatenate([-w[..., q:2 * q], w[..., 0:q], -w[..., 3 * q:4 * q], w[..., 2 * q:3 * q]], axis=-1)


def _prep_even(w_in, q_norm, w_uq, kv_norm, w_ukv, cm_norm, cm_ws, cm_bs):
    d = w_in.shape[0]
    e_q, e_kv = MLA_Q_RANK, MLA_Q_RANK + MLA_KV_RANK
    e_r = e_kv + MLA_ROPE
    e_u = e_r + CM_WIDTH
    wkr = w_in[:, e_kv:e_r]
    wall = jnp.concatenate([w_in[:, :e_kv], wkr, _rot_cols(wkr), jnp.zeros((d, LANES - 2 * MLA_ROPE), F32),
                            w_in[:, e_r:e_u], w_in[:, e_u:]], axis=1).astype(BF16)
    uq = w_uq.reshape(MLA_Q_RANK, MLA_HEADS, MLA_NOPE + MLA_ROPE)
    uq_r = uq[..., MLA_NOPE:]
    wuqt = jnp.concatenate([uq, _rot_cols(uq_r)], axis=-1).reshape(MLA_Q_RANK, MLA_HEADS * HEAD_SLOT).T.astype(BF16)
    ukv = w_ukv.reshape(MLA_KV_RANK, MLA_HEADS, MLA_NOPE + MLA_V)
    wuk = jnp.concatenate([ukv[..., :MLA_NOPE], jnp.zeros((MLA_KV_RANK, MLA_HEADS, HEAD_SLOT - MLA_NOPE), F32)],
                          axis=-1).reshape(MLA_KV_RANK, MLA_HEADS * HEAD_SLOT).astype(BF16)
    wuvt = ukv[..., MLA_NOPE:].reshape(MLA_KV_RANK, MLA_HEADS * MLA_V).T.astype(BF16)
    bias = jnp.repeat(cm_bs.T, CM_GROUP_DIM, axis=1)
    return dict(wall=wall, qn=q_norm[None], wuqt=wuqt, kvn=kv_norm[None], wuk=wuk, wuvt=wuvt, cmn=cm_norm[None],
                ws=cm_ws.astype(BF16), bias=bias)


def _prep_odd(w_in, w_gf, b_gf, w_gb, b_gb):
    d = w_in.shape[0]
    o_k = GLA_HEADS * GLA_DK
    o_v = o_k + GLA_HEADS * GLA_DV
    o_zb = o_v + 2 * GLA_GATE_RANK
    o_q = o_zb + GLA_HEADS * GLA_DK
    wall = jnp.concatenate([w_in[:, o_v:o_zb], jnp.zeros((d, LANES - 2 * GLA_GATE_RANK), F32), w_in[:, o_q:],
                            w_in[:, o_zb:o_q], w_in[:, :o_v]], axis=1).astype(BF16)
    zr = GLA_GATE_RANK
    wgf = jnp.zeros((LANES, o_k), F32).at[:zr].set(w_gf).astype(BF16)
    wgb = jnp.zeros((LANES, o_k), F32).at[zr:2 * zr].set(w_gb).astype(BF16)
    return dict(wall=wall, wgf=wgf, bgf=b_gf[None], wgb=wgb, bgb=b_gb[None])


def _rope_tables(length):
    rows = length // GRID_W
    r = jnp.repeat(jnp.arange(rows, dtype=F32), GRID_W)
    col = jnp.tile(jnp.arange(GRID_W, dtype=F32), rows)
    half = MLA_ROPE // 2
    inv = ROPE_BASE ** (-jnp.arange(0, half, 2, dtype=F32) / half)
    ang_r = r[:, None] * inv
    ang_c = col[:, None] * inv
    ang = jnp.concatenate([ang_r, ang_r, ang_c, ang_c], axis=-1)
    one = jnp.ones((length, MLA_NOPE), F32)
    pad = jnp.zeros((length, HEAD_SLOT - MLA_NOPE - MLA_ROPE), F32)
    ta = jnp.concatenate([one, jnp.cos(ang), pad], axis=1)
    tb = jnp.concatenate([0.0 * one, jnp.sin(ang), pad], axis=1)
    return ta, tb, jnp.cos(ang).T, jnp.sin(ang).T


def _flat_tables(length):
    ta = jnp.concatenate([jnp.ones((length, MLA_NOPE + MLA_ROPE), F32),
                          jnp.zeros((length, HEAD_SLOT - MLA_NOPE - MLA_ROPE), F32)], axis=1)
    return ta, jnp.zeros_like(ta), jnp.ones((MLA_ROPE, length), F32), jnp.zeros((MLA_ROPE, length), F32)


def _row_tile(seq, want):
    t = min(seq, want)
    while seq % t:
        t //= 2
    return t


def kernel(x, c, ctx, c_ctx, ada_w, ada_b, norm1_g, norm2_g, mlp_w1, mlp_w2, ev_w_in, ev_q_norm, ev_w_uq, ev_kv_norm,
           ev_w_ukv, ev_cm_norm, ev_cm_ws, ev_cm_bs, ev_w_out, od_w_in, od_w_gf, od_b_gf, od_w_gb, od_b_gb, od_o_norm,
           od_w_out, final_g):
    batch, seq, d = x.shape
    lc = ctx.shape[1]
    depth = ada_w.shape[0]
    tm_l = _row_tile(seq, 512)
    tm_c = _row_tile(lc, 256)
    tm_e = tm_l
    tk_l = min(MAX_KEY_TILE, lc)
    assert seq % tk_l == 0 and lc % tk_l == 0
    tq_l = _row_tile(seq, 512)
    tq_c = _row_tile(lc, 256)
    tb_l = _row_tile(seq, 256)
    tb_c = _row_tile(lc, 256)

    cvec = jnp.concatenate([c, c_ctx[None], jnp.zeros((8 - batch - 1, d), F32)], axis=0)
    mods = _ada_call(cvec, ada_w, ada_b).reshape(depth, 8, 6, d)

    tabs_l = _rope_tables(seq)
    tabs_c = _flat_tables(lc)

    xl = x.reshape(batch * seq, d)
    xc = ctx.reshape(batch * lc, d)
    for i in range(depth):
        need_ctx = i < depth - 1
        j = i // 2
        ml = [mods[i, :batch, t][:, None, :] for t in range(6)]
        mc = [mods[i, batch:batch + 1, t][:, None, :] for t in range(6)]
        g1 = norm1_g[i][None]
        g2 = norm2_g[i][None]
        w1 = mlp_w1[i].astype(BF16)
        w2 = mlp_w2[i].astype(BF16)
        final = i == depth - 1
        if i % 2 == 0:
            w = _prep_even(ev_w_in[j], ev_q_norm[j], ev_w_uq[j], ev_kv_norm[j], ev_w_ukv[j], ev_cm_norm[j],
                           ev_cm_ws[j], ev_cm_bs[j])
            wo = ev_w_out[j].astype(BF16)
            qtl, kl, vtl, mll = _even_in_call(xl, g1, ml[0], ml[1], tabs_l, w, seq=seq, tm=tm_e)
            qtc, kc, vtc, mlc = _even_in_call(xc, g1, mc[0], mc[1], tabs_c, w, seq=batch * lc, tm=tm_c)
            al = _attn_call(qtl, [(kl, vtl), (kc, vtc)], batch=batch, tq=tq_l, tk=tk_l)
            xl = _post_call(xl, al, mll, 0, 0, ml[2], g2, ml[3], ml[4], ml[5], wo, w1, w2, final_g[None],
                            seq=seq, tm=tm_l, final=final)
            if need_ctx:
                ac = _attn_call(qtc, [(kc, vtc)], batch=batch, tq=tq_c, tk=_key_tile(lc))
                xc = _post_call(xc, ac, mlc, 0, 0, mc[2], g2, mc[3], mc[4], mc[5], wo, w1, w2, final_g[None],
                                seq=batch * lc, tm=tm_c, final=False)
        else:
            w = _prep_odd(od_w_in[j], od_w_gf[j], od_b_gf[j], od_w_gb[j], od_b_gb[j])
            wo = od_w_out[j].astype(BF16)
            on = od_o_norm[j][None]
            kc, vc, qc, rc, dfc, dbc = _odd_in_call(xc, g1, mc[0], mc[1], w, seq=batch * lc, tm=tm_c)
            kl, vl, ql, rl, dfl, dbl = _odd_in_call(xl, g1, ml[0], ml[1], w, seq=seq, tm=tm_l)
            r3 = lambda t, n: t.reshape(batch, n, t.shape[-1])
            s0 = jnp.zeros((batch, GLA_HEADS, GLA_DV, GLA_DK), F32)
            ocf, s_f = _gla_call(r3(qc, lc), r3(kc, lc), r3(dfc, lc), r3(vc, lc), s0, reverse=False, tb=tb_c)
            mixc, s_b = _gla_call(r3(qc, lc), r3(kc, lc), r3(dbc, lc), r3(vc, lc), s0, reverse=True, tb=tb_c,
                                  o_fwd=ocf, r=r3(rc, lc), o_norm=on)
            olf, _ = _gla_call(r3(ql, seq), r3(kl, seq), r3(dfl, seq), r3(vl, seq), s_f, reverse=False, tb=tb_l)
            mixl, _ = _gla_call(r3(ql, seq), r3(kl, seq), r3(dbl, seq), r3(vl, seq), s_b, reverse=True, tb=tb_l,
                                o_fwd=olf, r=r3(rl, seq), o_norm=on)
            mixl = mixl.reshape(batch * seq, -1)
            xl = _post_call(xl, mixl, mixl, 0, 1, ml[2], g2, ml[3], ml[4], ml[5], wo, w1, w2, final_g[None],
                            seq=seq, tm=tm_l, final=final)
            if need_ctx:
                mixc = mixc.reshape(batch * lc, -1)
                xc = _post_call(xc, mixc, mixc, 0, 1, mc[2], g2, mc[3], mc[4], mc[5], wo, w1, w2, final_g[None],
                                seq=batch * lc, tm=tm_c, final=False)
    return xl.reshape(batch, seq, d)
```

```python
import functools

import jax
import jax.numpy as jnp
from jax import lax
from jax.experimental import pallas as pl
from jax.experimental.pallas import tpu as pltpu

F32 = jnp.float32
BF16 = jnp.bfloat16

D_MODEL = 1024
DEPTH = 4
GRID_W = 64
EPS = 1e-6
MLA_HEADS = 8
MLA_NOPE = 64
MLA_ROPE = 32
MLA_V = 64
MLA_Q_RANK = 384
MLA_KV_RANK = 256
MLA_SCALE = (MLA_NOPE + MLA_ROPE) ** -0.5
ROPE_BASE = 10000.0
CM_CHUNK = 128
CM_GROUPS = 4
CM_GROUP_DIM = 128
CM_WIDTH = CM_GROUPS * CM_GROUP_DIM
GLA_HEADS = 4
GLA_DK = 128
GLA_DV = 256
GLA_GATE_RANK = 16
GLA_TAU = 16.0
GLA_CHUNK = 64
GLA_PAIR = 2 * GLA_CHUNK
D_FF = 4 * D_MODEL

LANES = 128
HEAD_SLOT = LANES
BF16_ROWS = 16
V_SLOT = MLA_V + BF16_ROWS
VMEM_LIMIT = 56 * 1024 * 1024
MAX_KEY_TILE = 256
ATTN_UNROLL = 63
ATTN_SLOTS = 3
Q_PRESCALE = MLA_SCALE * 1.4426950408889634

EA_Q = 0
EA_KV = EA_Q + MLA_Q_RANK
EA_KR = EA_KV + MLA_KV_RANK
EA_U = EA_KR + LANES
EA_V = EA_U + CM_WIDTH
EA_END = EA_V + CM_WIDTH
OA_Z = 0
OA_R = OA_Z + LANES
OA_Q = OA_R + GLA_HEADS * GLA_DV
OA_K = OA_Q + GLA_HEADS * GLA_DK
OA_V = OA_K + GLA_HEADS * GLA_DK
OA_END = OA_V + GLA_HEADS * GLA_DV


def _cparams(sem):
    return pltpu.CompilerParams(dimension_semantics=sem, vmem_limit_bytes=VMEM_LIMIT)


def _rms(x, g):
    return x * lax.rsqrt(jnp.mean(x * x, axis=-1, keepdims=True) + EPS) * g


def _bdot(a, b):
    return jnp.dot(a.astype(BF16), b.astype(BF16), preferred_element_type=F32)


def _const_spec(shape):
    nd = len(shape)
    return pl.BlockSpec(shape, lambda *_: (0,) * nd, pipeline_mode=pl.Buffered(1))


def _ada_kernel(c_ref, w_ref, b_ref, o_ref):
    s = c_ref[...]
    s = s * jax.nn.sigmoid(s)
    o_ref[0] = _bdot(s, w_ref[0]) + b_ref[0]


def _ada_call(cvec, ada_w, ada_b):
    depth, d, n = ada_w.shape
    tn = 1536
    return pl.pallas_call(
        _ada_kernel,
        grid=(depth, n // tn),
        in_specs=[
            pl.BlockSpec((8, d), lambda l, j: (0, 0)),
            pl.BlockSpec((1, d, tn), lambda l, j: (l, 0, j)),
            pl.BlockSpec((1, 1, tn), lambda l, j: (l, 0, j)),
        ],
        out_specs=pl.BlockSpec((1, 8, tn), lambda l, j: (l, 0, j)),
        out_shape=jax.ShapeDtypeStruct((depth, 8, n), F32),
        compiler_params=_cparams(("arbitrary", "arbitrary")),
        name="ada_mod",
    )(cvec, ada_w, ada_b.reshape(depth, 1, n))


def _rope_slot(t, a, b):
    return t * a + pltpu.roll(t, HEAD_SLOT - MLA_ROPE, 1) * b


def _dot_nt(a, b):
    return lax.dot_general(a, b, (((1,), (1,)), ((), ())), preferred_element_type=F32)


def _even_in_kernel(x_ref, g1_ref, sh_ref, sc_ref, ta_ref, tb_ref, ct_ref, st_ref, wall_ref, qn_ref, wuqt_ref,
                    kvn_ref, wuk_ref, wuvt_ref, cmn_ref, ws_ref, bias_ref, qt_ref, k_ref, vt_ref, ml_ref, *, tm):
    h = _rms(x_ref[...], g1_ref[...]) * (1.0 + sc_ref[0]) + sh_ref[0]
    p = _bdot(h, wall_ref[...])

    cq = _rms(p[:, EA_Q:EA_KV], qn_ref[...]).astype(BF16)
    qt = _dot_nt(wuqt_ref[...], cq)
    cos_t = ct_ref[...]
    sin_t = st_ref[...]
    r0, r1 = MLA_NOPE, MLA_NOPE + MLA_ROPE
    for hd in range(MLA_HEADS):
        b = hd * HEAD_SLOT
        qt_ref[b:b + r0, :] = (qt[b:b + r0, :] * Q_PRESCALE).astype(BF16)
        roped = qt[b + r0:b + r1, :] * cos_t + qt[b + r1:b + HEAD_SLOT, :] * sin_t
        qt_ref[b + r0:b + r1, :] = (roped * Q_PRESCALE).astype(BF16)
        qt_ref[b + r1:b + HEAD_SLOT, :] = jnp.zeros((HEAD_SLOT - r1, tm), BF16)

    ckv = _rms(p[:, EA_KV:EA_KR], kvn_ref[...]).astype(BF16)
    kn = jnp.dot(ckv, wuk_ref[...], preferred_element_type=F32)
    kr = _rope_slot(pltpu.roll(p[:, EA_KR:EA_U], MLA_NOPE, 1), ta_ref[...], tb_ref[...])
    for hd in range(MLA_HEADS):
        sl = slice(hd * HEAD_SLOT, (hd + 1) * HEAD_SLOT)
        k_ref[:, sl] = (kn[:, sl] + kr).astype(BF16)
    vt = _dot_nt(wuvt_ref[...], ckv)
    for hd in range(MLA_HEADS):
        b = hd * V_SLOT
        vt_ref[b:b + MLA_V, :] = vt[hd * MLA_V:(hd + 1) * MLA_V, :].astype(BF16)
        vt_ref[b + MLA_V:b + V_SLOT, :] = jnp.ones((V_SLOT - MLA_V, tm), BF16)

    u = jax.nn.gelu(p[:, EA_U:EA_V])
    vv = jax.nn.gelu(p[:, EA_V:EA_END])
    cmn = cmn_ref[...]
    for g in range(CM_GROUPS):
        gl = slice(g * CM_GROUP_DIM, (g + 1) * CM_GROUP_DIM)
        vn = _rms(vv[:, gl], cmn).astype(BF16)
        w = ws_ref[g]
        for c in range(tm // CM_CHUNK):
            rs = slice(c * CM_CHUNK, (c + 1) * CM_CHUNK)
            y = jnp.dot(w, vn[rs], preferred_element_type=F32) + bias_ref[:, gl]
            ml_ref[rs, gl] = (u[rs, gl] * y).astype(BF16)


def _even_in_call(x, g1, sh, sc, tabs, w, *, seq, tm):
    ta, tb, cos_t, sin_t = tabs
    n, d = x.shape
    tpb = seq // tm
    tpt = ta.shape[0] // tm
    row = lambda i: (i, 0)
    col = lambda i: (0, i)
    bat = lambda i: (i // tpb, 0, 0)
    tab = lambda i: (i % tpt, 0)
    tab_t = lambda i: (0, i % tpt)
    hq = MLA_HEADS * HEAD_SLOT
    hv = MLA_HEADS * MLA_V
    in_specs = [
        pl.BlockSpec((tm, d), row),
        _const_spec((1, d)),
        pl.BlockSpec((1, 1, d), bat),
        pl.BlockSpec((1, 1, d), bat),
        pl.BlockSpec((tm, HEAD_SLOT), tab),
        pl.BlockSpec((tm, HEAD_SLOT), tab),
        pl.BlockSpec((MLA_ROPE, tm), tab_t),
        pl.BlockSpec((MLA_ROPE, tm), tab_t),
        _const_spec((d, EA_END)),
        _const_spec((1, MLA_Q_RANK)),
        _const_spec((hq, MLA_Q_RANK)),
        _const_spec((1, MLA_KV_RANK)),
        _const_spec((MLA_KV_RANK, hq)),
        _const_spec((hv, MLA_KV_RANK)),
        _const_spec((1, CM_GROUP_DIM)),
        _const_spec((CM_GROUPS, CM_CHUNK, CM_CHUNK)),
        _const_spec((CM_CHUNK, CM_WIDTH)),
    ]
    args = [x, g1, sh, sc, ta, tb, cos_t, sin_t, w["wall"], w["qn"], w["wuqt"], w["kvn"], w["wuk"], w["wuvt"],
            w["cmn"], w["ws"], w["bias"]]
    return pl.pallas_call(
        functools.partial(_even_in_kernel, tm=tm),
        grid=(n // tm,),
        in_specs=in_specs,
        out_specs=[
            pl.BlockSpec((hq, tm), col),
            pl.BlockSpec((tm, hq), row),
            pl.BlockSpec((MLA_HEADS * V_SLOT, tm), col),
            pl.BlockSpec((tm, CM_WIDTH), row),
        ],
        out_shape=[
            jax.ShapeDtypeStruct((hq, n), BF16),
            jax.ShapeDtypeStruct((n, hq), BF16),
            jax.ShapeDtypeStruct((MLA_HEADS * V_SLOT, n), BF16),
            jax.ShapeDtypeStruct((n, CM_WIDTH), BF16),
        ],
        compiler_params=_cparams(("arbitrary",)),
        name="even_in",
    )(*args)


def _attn_kernel(*refs, tq, tk, seg_tiles, unroll):
    nseg = len(seg_tiles)
    qt_ref = refs[0]
    kv_refs = [(refs[1 + 2 * i], refs[2 + 2 * i]) for i in range(nseg)]
    o_ref, s_scr, p_scr, acc_scr = refs[1 + 2 * nseg:]
    nsl = s_scr.shape[1]
    nk = sum(seg_tiles)

    def tile(t):
        if not isinstance(t, int):
            return kv_refs[0] + (pl.multiple_of(t * tk, LANES),)
        for (k_ref, vt_ref), n in zip(kv_refs, seg_tiles):
            if t < n:
                return k_ref, vt_ref, t * tk
            t -= n
        raise ValueError("key tile out of range")

    def step(t, ph, h, state, scores=True, softmax=True, weighted=True):
        m, alpha, tile_max = state
        new_alpha, new_tile_max = alpha, tile_max
        if scores:
            k_ref, _, r0 = tile(t)
            ks = k_ref[pl.ds(r0, tk), h * HEAD_SLOT:(h + 1) * HEAD_SLOT]
            s = jnp.dot(ks, qt_ref[h * HEAD_SLOT:(h + 1) * HEAD_SLOT, :], preferred_element_type=F32)
            s_scr[h, ph] = s
            new_tile_max = jnp.max(s, axis=0, keepdims=True)
        if weighted:
            _, vt_ref, r0 = tile(t - 2)
            vs = vt_ref[h * V_SLOT:(h + 1) * V_SLOT, pl.ds(r0, tk)]
            acc_scr[h] = alpha * acc_scr[h] + jnp.dot(vs, p_scr[h, (ph - 2) % nsl], preferred_element_type=F32)
        if softmax:
            m_new = jnp.maximum(m, tile_max)
            new_alpha = jnp.exp2(m - m_new)
            p_scr[h, (ph - 1) % nsl] = jnp.exp2(s_scr[h, (ph - 1) % nsl] - m_new).astype(BF16)
            m = m_new
        return m, new_alpha, new_tile_max

    carry = []
    for h in range(2):
        acc_scr[h] = jnp.zeros((V_SLOT, tq), F32)
        neg = jnp.full((1, tq), -jnp.inf, F32)
        st = step(0, 0, h, (neg, jnp.zeros((1, tq), F32), neg), softmax=False, weighted=False)
        carry.append(step(1, 1 % nsl, h, st, weighted=False))

    assert unroll % nsl == 0
    groups = max(seg_tiles[0] - 2, 0) // unroll

    def body(g, carry):
        t0 = 2 + unroll * g
        out = []
        for h in range(2):
            st = carry[h]
            for u in range(unroll):
                st = step(t0 + u, (2 + u) % nsl, h, st)
            out.append(st)
        return tuple(out)

    carry = lax.fori_loop(0, groups, body, tuple(carry))
    outs = []
    for h in range(2):
        st = carry[h]
        for t in range(2 + unroll * groups, nk):
            st = step(t, t % nsl, h, st)
        st = step(nk, nk % nsl, h, st, scores=False)
        step(nk + 1, (nk + 1) % nsl, h, st, scores=False, softmax=False)
        acc = acc_scr[h]
        outs.append(acc[0:MLA_V, :] * (1.0 / acc[MLA_V:MLA_V + 1, :]))
    o_ref[...] = jnp.concatenate(outs, axis=0).T.astype(BF16)


def _key_tile(t_len):
    tk = (min(t_len // 2, MAX_KEY_TILE) // LANES) * LANES
    while t_len % tk:
        tk -= LANES
    return tk


def _attn_call(qt, kvs, *, batch, tq, tk):
    n = qt.shape[1]
    nq = n // batch // tq
    in_specs = [pl.BlockSpec((2 * HEAD_SLOT, tq), lambda b, hp, i: (hp, b * nq + i))]
    args = [qt]
    seg_tiles = []
    for k, vt in kvs:
        t_len = k.shape[0] // batch
        assert t_len % tk == 0
        seg_tiles.append(t_len // tk)
        in_specs.append(pl.BlockSpec((t_len, 2 * HEAD_SLOT), lambda b, hp, i: (b, hp)))
        in_specs.append(pl.BlockSpec((2 * V_SLOT, t_len), lambda b, hp, i: (hp, b)))
        args += [k, vt]
    assert sum(seg_tiles) >= 2
    return pl.pallas_call(
        functools.partial(_attn_kernel, tq=tq, tk=tk, seg_tiles=tuple(seg_tiles), unroll=ATTN_UNROLL),
        grid=(batch, MLA_HEADS // 2, nq),
        in_specs=in_specs,
        out_specs=pl.BlockSpec((tq, 2 * MLA_V), lambda b, hp, i: (b * nq + i, hp)),
        out_shape=jax.ShapeDtypeStruct((n, MLA_HEADS * MLA_V), BF16),
        scratch_shapes=[
            pltpu.VMEM((2, ATTN_SLOTS, tk, tq), F32),
            pltpu.VMEM((2, ATTN_SLOTS, tk, tq), BF16),
            pltpu.VMEM((2, V_SLOT, tq), F32),
        ],
        compiler_params=_cparams(("arbitrary", "arbitrary", "arbitrary")),
        name="mla_attn",
    )(*args)


def _post_kernel(x_ref, a_ref, b_ref, gm_ref, g2_ref, sh_ref, sc_ref, gf_ref, wo_ref, w1_ref, w2_ref, fg_ref,
                 o_ref, *, final):
    half = wo_ref.shape[0] // 2
    y = (jnp.dot(a_ref[...], wo_ref[:half, :], preferred_element_type=F32)
         + jnp.dot(b_ref[...], wo_ref[half:, :], preferred_element_type=F32))
    x1 = x_ref[...] + gm_ref[0] * y
    h2 = (_rms(x1, g2_ref[...]) * (1.0 + sc_ref[0]) + sh_ref[0]).astype(BF16)
    acc = jnp.zeros_like(x1)
    fc = 1024
    for c in range(w1_ref.shape[1] // fc):
        hc = jnp.dot(h2, w1_ref[:, c * fc:(c + 1) * fc], preferred_element_type=F32)
        hc = jnp.square(jnp.maximum(hc, 0.0)).astype(BF16)
        acc = acc + jnp.dot(hc, w2_ref[c * fc:(c + 1) * fc, :], preferred_element_type=F32)
    x2 = x1 + gf_ref[0] * acc
    if final:
        x2 = _rms(x2, fg_ref[...])
    o_ref[...] = x2


def _post_call(x, mix_a, mix_b, col_a, col_b, gm, g2, sh, sc, gf, wo, w1, w2, fg, *, seq, tm, final):
    n, d = x.shape
    tpb = seq // tm
    row = lambda i: (i, 0)
    bat = lambda i: (i // tpb, 0, 0)
    half = wo.shape[0] // 2
    return pl.pallas_call(
        functools.partial(_post_kernel, final=final),
        grid=(n // tm,),
        in_specs=[
            pl.BlockSpec((tm, d), row),
            pl.BlockSpec((tm, half), lambda i: (i, col_a)),
            pl.BlockSpec((tm, half), lambda i: (i, col_b)),
            pl.BlockSpec((1, 1, d), bat),
            _const_spec((1, d)),
            pl.BlockSpec((1, 1, d), bat),
            pl.BlockSpec((1, 1, d), bat),
            pl.BlockSpec((1, 1, d), bat),
            _const_spec(wo.shape),
            _const_spec(w1.shape),
            _const_spec(w2.shape),
            _const_spec((1, d)),
        ],
        out_specs=pl.BlockSpec((tm, d), row),
        out_shape=jax.ShapeDtypeStruct((n, d), F32),
        compiler_params=_cparams(("arbitrary",)),
        name="post_mlp",
    )(x, mix_a, mix_b, gm, g2, sh, sc, gf, wo, w1, w2, fg)


def _log_decay(g):
    return (jnp.minimum(g, 0.0) - jnp.log1p(jnp.exp(-jnp.abs(g)))) * (1.0 / GLA_TAU)


def _chunk_cumsum(tri, x):
    hi = x.astype(BF16)
    r1 = x - hi.astype(F32)
    mid = r1.astype(BF16)
    lo = (r1 - mid.astype(F32)).astype(BF16)
    return (jnp.dot(tri, hi, preferred_element_type=F32) + jnp.dot(tri, mid, preferred_element_type=F32)
            + jnp.dot(tri, lo, preferred_element_type=F32))


def _odd_in_kernel(x_ref, g1_ref, sh_ref, sc_ref, wall_ref, wgf_ref, bgf_ref, wgb_ref, bgb_ref,
                   k_ref, v_ref, q_ref, r_ref, cf_ref, cb_ref, *, tm):
    rr = lax.broadcasted_iota(jnp.int32, (GLA_PAIR, GLA_PAIR), 0)
    cc = lax.broadcasted_iota(jnp.int32, (GLA_PAIR, GLA_PAIR), 1)
    same = (rr // GLA_CHUNK) == (cc // GLA_CHUNK)
    tri_f = (same & (rr >= cc)).astype(BF16)
    tri_b = (same & (rr <= cc)).astype(BF16)
    h = _rms(x_ref[...], g1_ref[...]) * (1.0 + sc_ref[0]) + sh_ref[0]
    p = _bdot(h, wall_ref[...])
    k_ref[...] = p[:, OA_K:OA_V]
    v_ref[...] = p[:, OA_V:OA_END].astype(BF16)
    q_ref[...] = p[:, OA_Q:OA_K] * (GLA_DK ** -0.5)
    r = p[:, OA_R:OA_Q]
    r_ref[...] = r * jax.nn.sigmoid(r)
    z = p[:, OA_Z:OA_R].astype(BF16)
    df = _log_decay(jnp.dot(z, wgf_ref[...], preferred_element_type=F32) + bgf_ref[...])
    db = _log_decay(jnp.dot(z, wgb_ref[...], preferred_element_type=F32) + bgb_ref[...])
    for c in range(tm // GLA_PAIR):
        rs = slice(c * GLA_PAIR, (c + 1) * GLA_PAIR)
        cf_ref[rs, :] = _chunk_cumsum(tri_f, df[rs])
        cb_ref[rs, :] = _chunk_cumsum(tri_b, db[rs])


def _odd_in_call(x, g1, sh, sc, w, *, seq, tm):
    n, d = x.shape
    tpb = seq // tm
    row = lambda i: (i, 0)
    bat = lambda i: (i // tpb, 0, 0)
    kd = GLA_HEADS * GLA_DK
    vd = GLA_HEADS * GLA_DV
    return pl.pallas_call(
        functools.partial(_odd_in_kernel, tm=tm),
        grid=(n // tm,),
        in_specs=[
            pl.BlockSpec((tm, d), row),
            _const_spec((1, d)),
            pl.BlockSpec((1, 1, d), bat),
            pl.BlockSpec((1, 1, d), bat),
            _const_spec((d, OA_END)),
            _const_spec((LANES, kd)),
            _const_spec((1, kd)),
            _const_spec((LANES, kd)),
            _const_spec((1, kd)),
        ],
        out_specs=[
            pl.BlockSpec((tm, kd), row),
            pl.BlockSpec((tm, vd), row),
            pl.BlockSpec((tm, kd), row),
            pl.BlockSpec((tm, vd), row),
            pl.BlockSpec((tm, kd), row),
            pl.BlockSpec((tm, kd), row),
        ],
        out_shape=[
            jax.ShapeDtypeStruct((n, kd), F32),
            jax.ShapeDtypeStruct((n, vd), BF16),
            jax.ShapeDtypeStruct((n, kd), F32),
            jax.ShapeDtypeStruct((n, vd), F32),
            jax.ShapeDtypeStruct((n, kd), F32),
            jax.ShapeDtypeStruct((n, kd), F32),
        ],
        compiler_params=_cparams(("arbitrary",)),
        name="odd_in",
    )(x, g1, sh, sc, w["wall"], w["wgf"], w["bgf"], w["wgb"], w["bgb"])


def _dot_tn(a, b):
    return lax.dot_general(a, b, (((0,), (0,)), ((), ())), preferred_element_type=F32)


def _gla_kernel(*refs, reverse, npair, combine):
    if combine:
        q_ref, k_ref, c_ref, v_ref, s0_ref, of_ref, r_ref, on_ref, o_ref, sfin_ref, s_scr = refs
    else:
        q_ref, k_ref, c_ref, v_ref, s0_ref, o_ref, sfin_ref, s_scr = refs
    i = pl.program_id(1)

    @pl.when(i == 0)
    def _():
        s_scr[...] = s0_ref[...]

    ch, pair = GLA_CHUNK, GLA_PAIR
    rr = lax.broadcasted_iota(jnp.int32, (pair, pair), 0)
    cc = lax.broadcasted_iota(jnp.int32, (pair, pair), 1)
    mask = ((rr // ch) == (cc // ch)) & ((rr <= cc) if reverse else (rr >= cc))
    lo = lax.broadcasted_iota(jnp.int32, (pair, GLA_DK), 0) < ch
    for pr in (range(npair - 1, -1, -1) if reverse else range(npair)):
        rs = slice(pr * pair, (pr + 1) * pair)
        for h in range(GLA_HEADS):
            ks = slice(h * GLA_DK, (h + 1) * GLA_DK)
            vs = slice(h * GLA_DV, (h + 1) * GLA_DV)
            bc = c_ref[rs, ks]
            k = k_ref[rs, ks]
            v = v_ref[rs, vs]
            if reverse:
                t_lo, t_hi = bc[0:1, :], bc[ch:ch + 1, :]
            else:
                t_lo, t_hi = bc[ch - 1:ch, :], bc[pair - 1:pair, :]
            q_t = (q_ref[rs, ks] * jnp.exp(bc)).astype(BF16)
            k_t = (k * jnp.exp(-bc)).astype(BF16)
            k_end = k * jnp.exp(jnp.where(lo, t_lo, t_hi) - bc)
            a = jnp.where(mask, _dot_nt(q_t, k_t), 0.0).astype(BF16)
            o = jnp.dot(a, v, preferred_element_type=F32)
            st = s_scr[h]
            halves = [(slice(0, ch), lo, t_lo), (slice(ch, pair), ~lo, t_hi)]
            inter = {}
            for hs, hm, tot in (halves[::-1] if reverse else halves):
                inter[hs.start] = _dot_nt(q_t[hs], st.astype(BF16))
                st = jnp.exp(tot) * st + _dot_tn(v, jnp.where(hm, k_end, 0.0).astype(BF16))
            s_scr[h] = st
            o = o + jnp.concatenate([inter[0], inter[ch]], axis=0)
            if combine:
                o = _rms(o + of_ref[rs, vs], on_ref[...]) * r_ref[rs, vs]
            o_ref[rs, vs] = o.astype(o_ref.dtype)

    @pl.when(i == pl.num_programs(1) - 1)
    def _():
        sfin_ref[...] = s_scr[...]


def _gla_call(q, k, c, v, s0, *, reverse, tb, o_fwd=None, r=None, o_norm=None):
    b, l, _ = q.shape
    nb = l // tb
    combine = o_fwd is not None
    blk = (lambda bi, i: (bi, nb - 1 - i, 0)) if reverse else (lambda bi, i: (bi, i, 0))
    kspec = pl.BlockSpec((None, tb, GLA_HEADS * GLA_DK), blk)
    vspec = pl.BlockSpec((None, tb, GLA_HEADS * GLA_DV), blk)
    sspec = pl.BlockSpec((None, GLA_HEADS, GLA_DV, GLA_DK), lambda bi, i: (bi, 0, 0, 0))
    in_specs = [kspec, kspec, kspec, vspec, sspec]
    args = [q, k, c, v, s0]
    if combine:
        in_specs += [vspec, vspec, pl.BlockSpec((1, GLA_DV), lambda bi, i: (0, 0))]
        args += [o_fwd, r, o_norm]
    return pl.pallas_call(
        functools.partial(_gla_kernel, reverse=reverse, npair=tb // GLA_PAIR, combine=combine),
        grid=(b, nb),
        in_specs=in_specs,
        out_specs=[vspec, sspec],
        out_shape=[
            jax.ShapeDtypeStruct((b, l, GLA_HEADS * GLA_DV), BF16 if combine else F32),
            jax.ShapeDtypeStruct((b, GLA_HEADS, GLA_DV, GLA_DK), F32),
        ],
        scratch_shapes=[pltpu.VMEM((GLA_HEADS, GLA_DV, GLA_DK), F32)],
        compiler_params=_cparams(("arbitrary", "arbitrary")),
        name="gla_bwd" if reverse else "gla_fwd",
    )(*args)


def _rot_cols(w):
    q = MLA_ROPE // 4
    return jnp.concatenate([-w[..., q:2 * q], w[..., 0:q], -w[..., 3 * q:4 * q], w[..., 2 * q:3 * q]], axis=-1)


def _prep_even(w_in, q_norm, w_uq, kv_norm, w_ukv, cm_norm, cm_ws, cm_bs):
    d = w_in.shape[0]
    e_q, e_kv = MLA_Q_RANK, MLA_Q_RANK + MLA_KV_RANK
    e_r = e_kv + MLA_ROPE
    e_u = e_r + CM_WIDTH
    wkr = w_in[:, e_kv:e_r]
    wall = jnp.concatenate([w_in[:, :e_kv], wkr, _rot_cols(wkr), jnp.zeros((d, LANES - 2 * MLA_ROPE), F32),
                            w_in[:, e_r:e_u], w_in[:, e_u:]], axis=1).astype(BF16)
    uq = w_uq.reshape(MLA_Q_RANK, MLA_HEADS, MLA_NOPE + MLA_ROPE)
    uq_r = uq[..., MLA_NOPE:]
    wuqt = jnp.concatenate([uq, _rot_cols(uq_r)], axis=-1).reshape(MLA_Q_RANK, MLA_HEADS * HEAD_SLOT).T.astype(BF16)
    ukv = w_ukv.reshape(MLA_KV_RANK, MLA_HEADS, MLA_NOPE + MLA_V)
    wuk = jnp.concatenate([ukv[..., :MLA_NOPE], jnp.zeros((MLA_KV_RANK, MLA_HEADS, HEAD_SLOT - MLA_NOPE), F32)],
                          axis=-1).reshape(MLA_KV_RANK, MLA_HEADS * HEAD_SLOT).astype(BF16)
    wuvt = ukv[..., MLA_NOPE:].reshape(MLA_KV_RANK, MLA_HEADS * MLA_V).T.astype(BF16)
    bias = jnp.repeat(cm_bs.T, CM_GROUP_DIM, axis=1)
    return dict(wall=wall, qn=q_norm[None], wuqt=wuqt, kvn=kv_norm[None], wuk=wuk, wuvt=wuvt, cmn=cm_norm[None],
                ws=cm_ws.astype(BF16), bias=bias)


def _prep_odd(w_in, w_gf, b_gf, w_gb, b_gb):
    d = w_in.shape[0]
    o_k = GLA_HEADS * GLA_DK
    o_v = o_k + GLA_HEADS * GLA_DV
    o_zb = o_v + 2 * GLA_GATE_RANK
    o_q = o_zb + GLA_HEADS * GLA_DK
    wall = jnp.concatenate([w_in[:, o_v:o_zb], jnp.zeros((d, LANES - 2 * GLA_GATE_RANK), F32), w_in[:, o_q:],
                            w_in[:, o_zb:o_q], w_in[:, :o_v]], axis=1).astype(BF16)
    zr = GLA_GATE_RANK
    wgf = jnp.zeros((LANES, o_k), F32).at[:zr].set(w_gf).astype(BF16)
    wgb = jnp.zeros((LANES, o_k), F32).at[zr:2 * zr].set(w_gb).astype(BF16)
    return dict(wall=wall, wgf=wgf, bgf=b_gf[None], wgb=wgb, bgb=b_gb[None])


def _rope_tables(length):
    rows = length // GRID_W
    r = jnp.repeat(jnp.arange(rows, dtype=F32), GRID_W)
    col = jnp.tile(jnp.arange(GRID_W, dtype=F32), rows)
    half = MLA_ROPE // 2
    inv = ROPE_BASE ** (-jnp.arange(0, half, 2, dtype=F32) / half)
    ang_r = r[:, None] * inv
    ang_c = col[:, None] * inv
    ang = jnp.concatenate([ang_r, ang_r, ang_c, ang_c], axis=-1)
    one = jnp.ones((length, MLA_NOPE), F32)
    pad = jnp.zeros((length, HEAD_SLOT - MLA_NOPE - MLA_ROPE), F32)
    ta = jnp.concatenate([one, jnp.cos(ang), pad], axis=1)
    tb = jnp.concatenate([0.0 * one, jnp.sin(ang), pad], axis=1)
    return ta, tb, jnp.cos(ang).T, jnp.sin(ang).T


def _flat_tables(length):
    ta = jnp.concatenate([jnp.ones((length, MLA_NOPE + MLA_ROPE), F32),
                          jnp.zeros((length, HEAD_SLOT - MLA_NOPE - MLA_ROPE), F32)], axis=1)
    return ta, jnp.zeros_like(ta), jnp.ones((MLA_ROPE, length), F32), jnp.zeros((MLA_ROPE, length), F32)


def _row_tile(seq, want):
    t = min(seq, want)
    while seq % t:
        t //= 2
    return t


def kernel(x, c, ctx, c_ctx, ada_w, ada_b, norm1_g, norm2_g, mlp_w1, mlp_w2, ev_w_in, ev_q_norm, ev_w_uq, ev_kv_norm,
           ev_w_ukv, ev_cm_norm, ev_cm_ws, ev_cm_bs, ev_w_out, od_w_in, od_w_gf, od_b_gf, od_w_gb, od_b_gb, od_o_norm,
           od_w_out, final_g):
    batch, seq, d = x.shape
    lc = ctx.shape[1]
    depth = ada_w.shape[0]
    tm_l = _row_tile(seq, 512)
    tm_c = _row_tile(lc, 256)
    tm_e = tm_l
    tk_l = min(MAX_KEY_TILE, lc)
    assert seq % tk_l == 0 and lc % tk_l == 0
    tq_l = _row_tile(seq, 512)
    tq_c = _row_tile(lc, 256)
    tb_l = _row_tile(seq, 256)
    tb_c = _row_tile(lc, 256)

    cvec = jnp.concatenate([c, c_ctx[None], jnp.zeros((8 - batch - 1, d), F32)], axis=0)
    mods = _ada_call(cvec, ada_w, ada_b).reshape(depth, 8, 6, d)

    tabs_l = _rope_tables(seq)
    tabs_c = _flat_tables(lc)

    xl = x.reshape(batch * seq, d)
    xc = ctx.reshape(batch * lc, d)
    for i in range(depth):
        need_ctx = i < depth - 1
        j = i // 2
        ml = [mods[i, :batch, t][:, None, :] for t in range(6)]
        mc = [mods[i, batch:batch + 1, t][:, None, :] for t in range(6)]
        g1 = norm1_g[i][None]
        g2 = norm2_g[i][None]
        w1 = mlp_w1[i].astype(BF16)
        w2 = mlp_w2[i].astype(BF16)
        final = i == depth - 1
        if i % 2 == 0:
            w = _prep_even(ev_w_in[j], ev_q_norm[j], ev_w_uq[j], ev_kv_norm[j], ev_w_ukv[j], ev_cm_norm[j],
                           ev_cm_ws[j], ev_cm_bs[j])
            wo = ev_w_out[j].astype(BF16)
            qtl, kl, vtl, mll = _even_in_call(xl, g1, ml[0], ml[1], tabs_l, w, seq=seq, tm=tm_e)
            qtc, kc, vtc, mlc = _even_in_call(xc, g1, mc[0], mc[1], tabs_c, w, seq=batch * lc, tm=tm_c)
            al = _attn_call(qtl, [(kl, vtl), (kc, vtc)], batch=batch, tq=tq_l, tk=tk_l)
            xl = _post_call(xl, al, mll, 0, 0, ml[2], g2, ml[3], ml[4], ml[5], wo, w1, w2, final_g[None],
                            seq=seq, tm=tm_l, final=final)
            if need_ctx:
                ac = _attn_call(qtc, [(kc, vtc)], batch=batch, tq=tq_c, tk=_key_tile(lc))
                xc = _post_call(xc, ac, mlc, 0, 0, mc[2], g2, mc[3], mc[4], mc[5], wo, w1, w2, final_g[None],
                                seq=batch * lc, tm=tm_c, final=False)
        else:
            w = _prep_odd(od_w_in[j], od_w_gf[j], od_b_gf[j], od_w_gb[j], od_b_gb[j])
            wo = od_w_out[j].astype(BF16)
            on = od_o_norm[j][None]
            kc, vc, qc, rc, dfc, dbc = _odd_in_call(xc, g1, mc[0], mc[1], w, seq=batch * lc, tm=tm_c)
            kl, vl, ql, rl, dfl, dbl = _odd_in_call(xl, g1, ml[0], ml[1], w, seq=seq, tm=tm_l)
            r3 = lambda t, n: t.reshape(batch, n, t.shape[-1])
            s0 = jnp.zeros((batch, GLA_HEADS, GLA_DV, GLA_DK), F32)
            ocf, s_f = _gla_call(r3(qc, lc), r3(kc, lc), r3(dfc, lc), r3(vc, lc), s0, reverse=False, tb=tb_c)
            mixc, s_b = _gla_call(r3(qc, lc), r3(kc, lc), r3(dbc, lc), r3(vc, lc), s0, reverse=True, tb=tb_c,
                                  o_fwd=ocf, r=r3(rc, lc), o_norm=on)
            olf, _ = _gla_call(r3(ql, seq), r3(kl, seq), r3(dfl, seq), r3(vl, seq), s_f, reverse=False, tb=tb_l)
            mixl, _ = _gla_call(r3(ql, seq), r3(kl, seq), r3(dbl, seq), r3(vl, seq), s_b, reverse=True, tb=tb_l,
                                o_fwd=olf, r=r3(rl, seq), o_norm=on)
            mixl = mixl.reshape(batch * seq, -1)
            xl = _post_call(xl, mixl, mixl, 0, 1, ml[2], g2, ml[3], ml[4], ml[5], wo, w1, w2, final_g[None],
                            seq=seq, tm=tm_l, final=final)
            if need_ctx:
                mixc = mixc.reshape(batch * lc, -1)
                xc = _post_call(xc, mixc, mixc, 0, 1, mc[2], g2, mc[3], mc[4], mc[5], wo, w1, w2, final_g[None],
                                seq=batch * lc, tm=tm_c, final=False)
    return xl.reshape(batch, seq, d)
```

```python
import functools

import jax
import jax.numpy as jnp
from jax import lax
from jax.experimental import pallas as pl
from jax.experimental.pallas import tpu as pltpu

F32 = jnp.float32
BF16 = jnp.bfloat16

D_MODEL = 1024
DEPTH = 4
GRID_W = 64
EPS = 1e-6
MLA_HEADS = 8
MLA_NOPE = 64
MLA_ROPE = 32
MLA_V = 64
MLA_Q_RANK = 384
MLA_KV_RANK = 256
MLA_SCALE = (MLA_NOPE + MLA_ROPE) ** -0.5
ROPE_BASE = 10000.0
CM_CHUNK = 128
CM_GROUPS = 4
CM_GROUP_DIM = 128
CM_WIDTH = CM_GROUPS * CM_GROUP_DIM
GLA_HEADS = 4
GLA_DK = 128
GLA_DV = 256
GLA_GATE_RANK = 16
GLA_TAU = 16.0
GLA_CHUNK = 64
GLA_PAIR = 2 * GLA_CHUNK
D_FF = 4 * D_MODEL

LANES = 128
HEAD_SLOT = LANES
BF16_ROWS = 16
V_SLOT = MLA_V + BF16_ROWS
VMEM_LIMIT = 56 * 1024 * 1024
MAX_KEY_TILE = 256
ATTN_UNROLL = 63
ATTN_SLOTS = 3
Q_PRESCALE = MLA_SCALE * 1.4426950408889634

EA_Q = 0
EA_KV = EA_Q + MLA_Q_RANK
EA_KR = EA_KV + MLA_KV_RANK
EA_U = EA_KR + LANES
EA_V = EA_U + CM_WIDTH
EA_END = EA_V + CM_WIDTH
OA_Z = 0
OA_R = OA_Z + LANES
OA_Q = OA_R + GLA_HEADS * GLA_DV
OA_K = OA_Q + GLA_HEADS * GLA_DK
OA_V = OA_K + GLA_HEADS * GLA_DK
OA_END = OA_V + GLA_HEADS * GLA_DV


def _cparams(sem):
    return pltpu.CompilerParams(dimension_semantics=sem, vmem_limit_bytes=VMEM_LIMIT)


def _rms(x, g):
    return x * lax.rsqrt(jnp.mean(x * x, axis=-1, keepdims=True) + EPS) * g


def _bdot(a, b):
    return jnp.dot(a.astype(BF16), b.astype(BF16), preferred_element_type=F32)


def _const_spec(shape):
    nd = len(shape)
    return pl.BlockSpec(shape, lambda *_: (0,) * nd, pipeline_mode=pl.Buffered(1))


def _ada_kernel(c_ref, w_ref, b_ref, o_ref):
    s = c_ref[...]
    s = s * jax.nn.sigmoid(s)
    o_ref[0] = _bdot(s, w_ref[0]) + b_ref[0]


def _ada_call(cvec, ada_w, ada_b):
    depth, d, n = ada_w.shape
    tn = 1536
    return pl.pallas_call(
        _ada_kernel,
        grid=(depth, n // tn),
        in_specs=[
            pl.BlockSpec((8, d), lambda l, j: (0, 0)),
            pl.BlockSpec((1, d, tn), lambda l, j: (l, 0, j)),
            pl.BlockSpec((1, 1, tn), lambda l, j: (l, 0, j)),
        ],
        out_specs=pl.BlockSpec((1, 8, tn), lambda l, j: (l, 0, j)),
        out_shape=jax.ShapeDtypeStruct((depth, 8, n), F32),
        compiler_params=_cparams(("arbitrary", "arbitrary")),
        name="ada_mod",
    )(cvec, ada_w, ada_b.reshape(depth, 1, n))


def _rope_slot(t, a, b):
    return t * a + pltpu.roll(t, HEAD_SLOT - MLA_ROPE, 1) * b


def _dot_nt(a, b):
    return lax.dot_general(a, b, (((1,), (1,)), ((), ())), preferred_element_type=F32)


def _even_in_kernel(x_ref, g1_ref, sh_ref, sc_ref, ta_ref, tb_ref, ct_ref, st_ref, wall_ref, qn_ref, wuqt_ref,
                    kvn_ref, wuk_ref, wuvt_ref, cmn_ref, ws_ref, bias_ref, qt_ref, k_ref, vt_ref, ml_ref, *, tm):
    h = _rms(x_ref[...], g1_ref[...]) * (1.0 + sc_ref[0]) + sh_ref[0]
    p = _bdot(h, wall_ref[...])

    cq = _rms(p[:, EA_Q:EA_KV], qn_ref[...]).astype(BF16)
    qt = _dot_nt(wuqt_ref[...], cq)
    cos_t = ct_ref[...]
    sin_t = st_ref[...]
    r0, r1 = MLA_NOPE, MLA_NOPE + MLA_ROPE
    for hd in range(MLA_HEADS):
        b = hd * HEAD_SLOT
        qt_ref[b:b + r0, :] = (qt[b:b + r0, :] * Q_PRESCALE).astype(BF16)
        roped = qt[b + r0:b + r1, :] * cos_t + qt[b + r1:b + HEAD_SLOT, :] * sin_t
        qt_ref[b + r0:b + r1, :] = (roped * Q_PRESCALE).astype(BF16)
        qt_ref[b + r1:b + HEAD_SLOT, :] = jnp.zeros((HEAD_SLOT - r1, tm), BF16)

    ckv = _rms(p[:, EA_KV:EA_KR], kvn_ref[...]).astype(BF16)
    kn = jnp.dot(ckv, wuk_ref[...], preferred_element_type=F32)
    kr = _rope_slot(pltpu.roll(p[:, EA_KR:EA_U], MLA_NOPE, 1), ta_ref[...], tb_ref[...])
    for hd in range(MLA_HEADS):
        sl = slice(hd * HEAD_SLOT, (hd + 1) * HEAD_SLOT)
        k_ref[:, sl] = (kn[:, sl] + kr).astype(BF16)
    vt = _dot_nt(wuvt_ref[...], ckv)
    for hd in range(MLA_HEADS):
        b = hd * V_SLOT
        vt_ref[b:b + MLA_V, :] = vt[hd * MLA_V:(hd + 1) * MLA_V, :].astype(BF16)
        vt_ref[b + MLA_V:b + V_SLOT, :] = jnp.ones((V_SLOT - MLA_V, tm), BF16)

    u = jax.nn.gelu(p[:, EA_U:EA_V])
    vv = jax.nn.gelu(p[:, EA_V:EA_END])
    cmn = cmn_ref[...]
    for g in range(CM_GROUPS):
        gl = slice(g * CM_GROUP_DIM, (g + 1) * CM_GROUP_DIM)
        vn = _rms(vv[:, gl], cmn).astype(BF16)
        w = ws_ref[g]
        for c in range(tm // CM_CHUNK):
            rs = slice(c * CM_CHUNK, (c + 1) * CM_CHUNK)
            y = jnp.dot(w, vn[rs], preferred_element_type=F32) + bias_ref[:, gl]
            ml_ref[rs, gl] = (u[rs, gl] * y).astype(BF16)


def _even_in_call(x, g1, sh, sc, tabs, w, *, seq, tm):
    ta, tb, cos_t, sin_t = tabs
    n, d = x.shape
    tpb = seq // tm
    tpt = ta.shape[0] // tm
    row = lambda i: (i, 0)
    col = lambda i: (0, i)
    bat = lambda i: (i // tpb, 0, 0)
    tab = lambda i: (i % tpt, 0)
    tab_t = lambda i: (0, i % tpt)
    hq = MLA_HEADS * HEAD_SLOT
    hv = MLA_HEADS * MLA_V
    in_specs = [
        pl.BlockSpec((tm, d), row),
        _const_spec((1, d)),
        pl.BlockSpec((1, 1, d), bat),
        pl.BlockSpec((1, 1, d), bat),
        pl.BlockSpec((tm, HEAD_SLOT), tab),
        pl.BlockSpec((tm, HEAD_SLOT), tab),
        pl.BlockSpec((MLA_ROPE, tm), tab_t),
        pl.BlockSpec((MLA_ROPE, tm), tab_t),
        _const_spec((d, EA_END)),
        _const_spec((1, MLA_Q_RANK)),
        _const_spec((hq, MLA_Q_RANK)),
        _const_spec((1, MLA_KV_RANK)),
        _const_spec((MLA_KV_RANK, hq)),
        _const_spec((hv, MLA_KV_RANK)),
        _const_spec((1, CM_GROUP_DIM)),
        _const_spec((CM_GROUPS, CM_CHUNK, CM_CHUNK)),
        _const_spec((CM_CHUNK, CM_WIDTH)),
    ]
    args = [x, g1, sh, sc, ta, tb, cos_t, sin_t, w["wall"], w["qn"], w["wuqt"], w["kvn"], w["wuk"], w["wuvt"],
            w["cmn"], w["ws"], w["bias"]]
    return pl.pallas_call(
        functools.partial(_even_in_kernel, tm=tm),
        grid=(n // tm,),
        in_specs=in_specs,
        out_specs=[
            pl.BlockSpec((hq, tm), col),
            pl.BlockSpec((tm, hq), row),
            pl.BlockSpec((MLA_HEADS * V_SLOT, tm), col),
            pl.BlockSpec((tm, CM_WIDTH), row),
        ],
        out_shape=[
            jax.ShapeDtypeStruct((hq, n), BF16),
            jax.ShapeDtypeStruct((n, hq), BF16),
            jax.ShapeDtypeStruct((MLA_HEADS * V_SLOT, n), BF16),
            jax.ShapeDtypeStruct((n, CM_WIDTH), BF16),
        ],
        compiler_params=_cparams(("arbitrary",)),
        name="even_in",
    )(*args)


def _attn_kernel(*refs, tq, tk, seg_tiles, unroll):
    nseg = len(seg_tiles)
    qt_ref = refs[0]
    kv_refs = [(refs[1 + 2 * i], refs[2 + 2 * i]) for i in range(nseg)]
    o_ref, s_scr, p_scr, acc_scr = refs[1 + 2 * nseg:]
    nsl = s_scr.shape[1]
    nk = sum(seg_tiles)

    def tile(t):
        if not isinstance(t, int):
            return kv_refs[0] + (pl.multiple_of(t * tk, LANES),)
        for (k_ref, vt_ref), n in zip(kv_refs, seg_tiles):
            if t < n:
                return k_ref, vt_ref, t * tk
            t -= n
        raise ValueError("key tile out of range")

    def step(t, ph, h, state, scores=True, softmax=True, weighted=True):
        m, alpha, tile_max = state
        new_alpha, new_tile_max = alpha, tile_max
        if scores:
            k_ref, _, r0 = tile(t)
            ks = k_ref[pl.ds(r0, tk), h * HEAD_SLOT:(h + 1) * HEAD_SLOT]
            s = jnp.dot(ks, qt_ref[h * HEAD_SLOT:(h + 1) * HEAD_SLOT, :], preferred_element_type=F32)
            s_scr[h, ph] = s
            new_tile_max = jnp.max(s, axis=0, keepdims=True)
        if weighted:
            _, vt_ref, r0 = tile(t - 2)
            vs = vt_ref[h * V_SLOT:(h + 1) * V_SLOT, pl.ds(r0, tk)]
            acc_scr[h] = alpha * acc_scr[h] + jnp.dot(vs, p_scr[h, (ph - 2) % nsl], preferred_element_type=F32)
        if softmax:
            m_new = jnp.maximum(m, tile_max)
            new_alpha = jnp.exp2(m - m_new)
            p_scr[h, (ph - 1) % nsl] = jnp.exp2(s_scr[h, (ph - 1) % nsl] - m_new).astype(BF16)
            m = m_new
        return m, new_alpha, new_tile_max

    carry = []
    for h in range(2):
        acc_scr[h] = jnp.zeros((V_SLOT, tq), F32)
        neg = jnp.full((1, tq), -jnp.inf, F32)
        st = step(0, 0, h, (neg, jnp.zeros((1, tq), F32), neg), softmax=False, weighted=False)
        carry.append(step(1, 1 % nsl, h, st, weighted=False))

    assert unroll % nsl == 0
    groups = max(seg_tiles[0] - 2, 0) // unroll

    def body(g, carry):
        t0 = 2 + unroll * g
        out = []
        for h in range(2):
            st = carry[h]
            for u in range(unroll):
                st = step(t0 + u, (2 + u) % nsl, h, st)
            out.append(st)
        return tuple(out)

    carry = lax.fori_loop(0, groups, body, tuple(carry))
    outs = []
    for h in range(2):
        st = carry[h]
        for t in range(2 + unroll * groups, nk):
            st = step(t, t % nsl, h, st)
        st = step(nk, nk % nsl, h, st, scores=False)
        step(nk + 1, (nk + 1) % nsl, h, st, scores=False, softmax=False)
        acc = acc_scr[h]
        outs.append(acc[0:MLA_V, :] * (1.0 / acc[MLA_V:MLA_V + 1, :]))
    o_ref[...] = jnp.concatenate(outs, axis=0).T.astype(BF16)


def _key_tile(t_len):
    tk = (min(t_len // 2, MAX_KEY_TILE) // LANES) * LANES
    while t_len % tk:
        tk -= LANES
    return tk


def _attn_call(qt, kvs, *, batch, tq, tk):
    n = qt.shape[1]
    nq = n // batch // tq
    in_specs = [pl.BlockSpec((2 * HEAD_SLOT, tq), lambda b, hp, i: (hp, b * nq + i))]
    args = [qt]
    seg_tiles = []
    for k, vt in kvs:
        t_len = k.shape[0] // batch
        assert t_len % tk == 0
        seg_tiles.append(t_len // tk)
        in_specs.append(pl.BlockSpec((t_len, 2 * HEAD_SLOT), lambda b, hp, i: (b, hp)))
        in_specs.append(pl.BlockSpec((2 * V_SLOT, t_len), lambda b, hp, i: (hp, b)))
        args += [k, vt]
    assert sum(seg_tiles) >= 2
    return pl.pallas_call(
        functools.partial(_attn_kernel, tq=tq, tk=tk, seg_tiles=tuple(seg_tiles), unroll=ATTN_UNROLL),
        grid=(batch, MLA_HEADS // 2, nq),
        in_specs=in_specs,
        out_specs=pl.BlockSpec((tq, 2 * MLA_V), lambda b, hp, i: (b * nq + i, hp)),
        out_shape=jax.ShapeDtypeStruct((n, MLA_HEADS * MLA_V), BF16),
        scratch_shapes=[
            pltpu.VMEM((2, ATTN_SLOTS, tk, tq), F32),
            pltpu.VMEM((2, ATTN_SLOTS, tk, tq), BF16),
            pltpu.VMEM((2, V_SLOT, tq), F32),
        ],
        compiler_params=_cparams(("arbitrary", "arbitrary", "arbitrary")),
        name="mla_attn",
    )(*args)


def _post_kernel(x_ref, a_ref, b_ref, gm_ref, g2_ref, sh_ref, sc_ref, gf_ref, wo_ref, w1_ref, w2_ref, fg_ref,
                 o_ref, *, final):
    half = wo_ref.shape[0] // 2
    y = (jnp.dot(a_ref[...], wo_ref[:half, :], preferred_element_type=F32)
         + jnp.dot(b_ref[...], wo_ref[half:, :], preferred_element_type=F32))
    x1 = x_ref[...] + gm_ref[0] * y
    h2 = (_rms(x1, g2_ref[...]) * (1.0 + sc_ref[0]) + sh_ref[0]).astype(BF16)
    acc = jnp.zeros_like(x1)
    fc = 1024
    for c in range(w1_ref.shape[1] // fc):
        hc = jnp.dot(h2, w1_ref[:, c * fc:(c + 1) * fc], preferred_element_type=F32)
        hc = jnp.square(jnp.maximum(hc, 0.0)).astype(BF16)
        acc = acc + jnp.dot(hc, w2_ref[c * fc:(c + 1) * fc, :], preferred_element_type=F32)
    x2 = x1 + gf_ref[0] * acc
    if final:
        x2 = _rms(x2, fg_ref[...])
    o_ref[...] = x2


def _post_call(x, mix_a, mix_b, col_a, col_b, gm, g2, sh, sc, gf, wo, w1, w2, fg, *, seq, tm, final):
    n, d = x.shape
    tpb = seq // tm
    row = lambda i: (i, 0)
    bat = lambda i: (i // tpb, 0, 0)
    half = wo.shape[0] // 2
    return pl.pallas_call(
        functools.partial(_post_kernel, final=final),
        grid=(n // tm,),
        in_specs=[
            pl.BlockSpec((tm, d), row),
            pl.BlockSpec((tm, half), lambda i: (i, col_a)),
            pl.BlockSpec((tm, half), lambda i: (i, col_b)),
            pl.BlockSpec((1, 1, d), bat),
            _const_spec((1, d)),
            pl.BlockSpec((1, 1, d), bat),
            pl.BlockSpec((1, 1, d), bat),
            pl.BlockSpec((1, 1, d), bat),
            _const_spec(wo.shape),
            _const_spec(w1.shape),
            _const_spec(w2.shape),
            _const_spec((1, d)),
        ],
        out_specs=pl.BlockSpec((tm, d), row),
        out_shape=jax.ShapeDtypeStruct((n, d), F32),
        compiler_params=_cparams(("arbitrary",)),
        name="post_mlp",
    )(x, mix_a, mix_b, gm, g2, sh, sc, gf, wo, w1, w2, fg)


def _log_decay(g):
    return (jnp.minimum(g, 0.0) - jnp.log1p(jnp.exp(-jnp.abs(g)))) * (1.0 / GLA_TAU)


def _chunk_cumsum(tri, x):
    hi = x.astype(BF16)
    r1 = x - hi.astype(F32)
    mid = r1.astype(BF16)
    lo = (r1 - mid.astype(F32)).astype(BF16)
    return (jnp.dot(tri, hi, preferred_element_type=F32) + jnp.dot(tri, mid, preferred_element_type=F32)
            + jnp.dot(tri, lo, preferred_element_type=F32))


def _odd_in_kernel(x_ref, g1_ref, sh_ref, sc_ref, wall_ref, wgf_ref, bgf_ref, wgb_ref, bgb_ref,
                   k_ref, v_ref, q_ref, r_ref, cf_ref, cb_ref, *, tm):
    rr = lax.broadcasted_iota(jnp.int32, (GLA_PAIR, GLA_PAIR), 0)
    cc = lax.broadcasted_iota(jnp.int32, (GLA_PAIR, GLA_PAIR), 1)
    same = (rr // GLA_CHUNK) == (cc // GLA_CHUNK)
    tri_f = (same & (rr >= cc)).astype(BF16)
    tri_b = (same & (rr <= cc)).astype(BF16)
    h = _rms(x_ref[...], g1_ref[...]) * (1.0 + sc_ref[0]) + sh_ref[0]
    p = _bdot(h, wall_ref[...])
    k_ref[...] = p[:, OA_K:OA_V]
    v_ref[...] = p[:, OA_V:OA_END].astype(BF16)
    q_ref[...] = p[:, OA_Q:OA_K] * (GLA_DK ** -0.5)
    r = p[:, OA_R:OA_Q]
    r_ref[...] = r * jax.nn.sigmoid(r)
    z = p[:, OA_Z:OA_R].astype(BF16)
    df = _log_decay(jnp.dot(z, wgf_ref[...], preferred_element_type=F32) + bgf_ref[...])
    db = _log_decay(jnp.dot(z, wgb_ref[...], preferred_element_type=F32) + bgb_ref[...])
    for c in range(tm // GLA_PAIR):
        rs = slice(c * GLA_PAIR, (c + 1) * GLA_PAIR)
        cf_ref[rs, :] = _chunk_cumsum(tri_f, df[rs])
        cb_ref[rs, :] = _chunk_cumsum(tri_b, db[rs])


def _odd_in_call(x, g1, sh, sc, w, *, seq, tm):
    n, d = x.shape
    tpb = seq // tm
    row = lambda i: (i, 0)
    bat = lambda i: (i // tpb, 0, 0)
    kd = GLA_HEADS * GLA_DK
    vd = GLA_HEADS * GLA_DV
    return pl.pallas_call(
        functools.partial(_odd_in_kernel, tm=tm),
        grid=(n // tm,),
        in_specs=[
            pl.BlockSpec((tm, d), row),
            _const_spec((1, d)),
            pl.BlockSpec((1, 1, d), bat),
            pl.BlockSpec((1, 1, d), bat),
            _const_spec((d, OA_END)),
            _const_spec((LANES, kd)),
            _const_spec((1, kd)),
            _const_spec((LANES, kd)),
            _const_spec((1, kd)),
        ],
        out_specs=[
            pl.BlockSpec((tm, kd), row),
            pl.BlockSpec((tm, vd), row),
            pl.BlockSpec((tm, kd), row),
            pl.BlockSpec((tm, vd), row),
            pl.BlockSpec((tm, kd), row),
            pl.BlockSpec((tm, kd), row),
        ],
        out_shape=[
            jax.ShapeDtypeStruct((n, kd), F32),
            jax.ShapeDtypeStruct((n, vd), BF16),
            jax.ShapeDtypeStruct((n, kd), F32),
            jax.ShapeDtypeStruct((n, vd), F32),
            jax.ShapeDtypeStruct((n, kd), F32),
            jax.ShapeDtypeStruct((n, kd), F32),
        ],
        compiler_params=_cparams(("arbitrary",)),
        name="odd_in",
    )(x, g1, sh, sc, w["wall"], w["wgf"], w["bgf"], w["wgb"], w["bgb"])


def _dot_tn(a, b):
    return lax.dot_general(a, b, (((0,), (0,)), ((), ())), preferred_element_type=F32)


def _gla_kernel(*refs, reverse, npair, combine):
    if combine:
        q_ref, k_ref, c_ref, v_ref, s0_ref, of_ref, r_ref, on_ref, o_ref, sfin_ref, s_scr = refs
    else:
        q_ref, k_ref, c_ref, v_ref, s0_ref, o_ref, sfin_ref, s_scr = refs
    i = pl.program_id(1)

    @pl.when(i == 0)
    def _():
        s_scr[...] = s0_ref[...]

    ch, pair = GLA_CHUNK, GLA_PAIR
    rr = lax.broadcasted_iota(jnp.int32, (pair, pair), 0)
    cc = lax.broadcasted_iota(jnp.int32, (pair, pair), 1)
    mask = ((rr // ch) == (cc // ch)) & ((rr <= cc) if reverse else (rr >= cc))
    lo = lax.broadcasted_iota(jnp.int32, (pair, GLA_DK), 0) < ch
    for pr in (range(npair - 1, -1, -1) if reverse else range(npair)):
        rs = slice(pr * pair, (pr + 1) * pair)
        for h in range(GLA_HEADS):
            ks = slice(h * GLA_DK, (h + 1) * GLA_DK)
            vs = slice(h * GLA_DV, (h + 1) * GLA_DV)
            bc = c_ref[rs, ks]
            k = k_ref[rs, ks]
            v = v_ref[rs, vs]
            if reverse:
                t_lo, t_hi = bc[0:1, :], bc[ch:ch + 1, :]
            else:
                t_lo, t_hi = bc[ch - 1:ch, :], bc[pair - 1:pair, :]
            q_t = (q_ref[rs, ks] * jnp.exp(bc)).astype(BF16)
            k_t = (k * jnp.exp(-bc)).astype(BF16)
            k_end = k * jnp.exp(jnp.where(lo, t_lo, t_hi) - bc)
            a = jnp.where(mask, _dot_nt(q_t, k_t), 0.0).astype(BF16)
            o = jnp.dot(a, v, preferred_element_type=F32)
            st = s_scr[h]
            halves = [(slice(0, ch), lo, t_lo), (slice(ch, pair), ~lo, t_hi)]
            inter = {}
            for hs, hm, tot in (halves[::-1] if reverse else halves):
                inter[hs.start] = _dot_nt(q_t[hs], st.astype(BF16))
                st = jnp.exp(tot) * st + _dot_tn(v, jnp.where(hm, k_end, 0.0).astype(BF16))
            s_scr[h] = st
            o = o + jnp.concatenate([inter[0], inter[ch]], axis=0)
            if combine:
                o = _rms(o + of_ref[rs, vs], on_ref[...]) * r_ref[rs, vs]
            o_ref[rs, vs] = o.astype(o_ref.dtype)

    @pl.when(i == pl.num_programs(1) - 1)
    def _():
        sfin_ref[...] = s_scr[...]


def _gla_call(q, k, c, v, s0, *, reverse, tb, o_fwd=None, r=None, o_norm=None):
    b, l, _ = q.shape
    nb = l // tb
    combine = o_fwd is not None
    blk = (lambda bi, i: (bi, nb - 1 - i, 0)) if reverse else (lambda bi, i: (bi, i, 0))
    kspec = pl.BlockSpec((None, tb, GLA_HEADS * GLA_DK), blk)
    vspec = pl.BlockSpec((None, tb, GLA_HEADS * GLA_DV), blk)
    sspec = pl.BlockSpec((None, GLA_HEADS, GLA_DV, GLA_DK), lambda bi, i: (bi, 0, 0, 0))
    in_specs = [kspec, kspec, kspec, vspec, sspec]
    args = [q, k, c, v, s0]
    if combine:
        in_specs += [vspec, vspec, pl.BlockSpec((1, GLA_DV), lambda bi, i: (0, 0))]
        args += [o_fwd, r, o_norm]
    return pl.pallas_call(
        functools.partial(_gla_kernel, reverse=reverse, npair=tb // GLA_PAIR, combine=combine),
        grid=(b, nb),
        in_specs=in_specs,
        out_specs=[vspec, sspec],
        out_shape=[
            jax.ShapeDtypeStruct((b, l, GLA_HEADS * GLA_DV), BF16 if combine else F32),
            jax.ShapeDtypeStruct((b, GLA_HEADS, GLA_DV, GLA_DK), F32),
        ],
        scratch_shapes=[pltpu.VMEM((GLA_HEADS, GLA_DV, GLA_DK), F32)],
        compiler_params=_cparams(("arbitrary", "arbitrary")),
        name="gla_bwd" if reverse else "gla_fwd",
    )(*args)


def _rot_cols(w):
    q = MLA_ROPE // 4
    return jnp.concatenate([-w[..., q:2 * q], w[..., 0:q], -w[..., 3 * q:4 * q], w[..., 2 * q:3 * q]], axis=-1)


def _prep_even(w_in, q_norm, w_uq, kv_norm, w_ukv, cm_norm, cm_ws, cm_bs):
    d = w_in.shape[0]
    e_q, e_kv = MLA_Q_RANK, MLA_Q_RANK + MLA_KV_RANK
    e_r = e_kv + MLA_ROPE
    e_u = e_r + CM_WIDTH
    wkr = w_in[:, e_kv:e_r]
    wall = jnp.concatenate([w_in[:, :e_kv], wkr, _rot_cols(wkr), jnp.zeros((d, LANES - 2 * MLA_ROPE), F32),
                            w_in[:, e_r:e_u], w_in[:, e_u:]], axis=1).astype(BF16)
    uq = w_uq.reshape(MLA_Q_RANK, MLA_HEADS, MLA_NOPE + MLA_ROPE)
    uq_r = uq[..., MLA_NOPE:]
    wuqt = jnp.concatenate([uq, _rot_cols(uq_r)], axis=-1).reshape(MLA_Q_RANK, MLA_HEADS * HEAD_SLOT).T.astype(BF16)
    ukv = w_ukv.reshape(MLA_KV_RANK, MLA_HEADS, MLA_NOPE + MLA_V)
    wuk = jnp.concatenate([ukv[..., :MLA_NOPE], jnp.zeros((MLA_KV_RANK, MLA_HEADS, HEAD_SLOT - MLA_NOPE), F32)],
                          axis=-1).reshape(MLA_KV_RANK, MLA_HEADS * HEAD_SLOT).astype(BF16)
    wuvt = ukv[..., MLA_NOPE:].reshape(MLA_KV_RANK, MLA_HEADS * MLA_V).T.astype(BF16)
    bias = jnp.repeat(cm_bs.T, CM_GROUP_DIM, axis=1)
    return dict(wall=wall, qn=q_norm[None], wuqt=wuqt, kvn=kv_norm[None], wuk=wuk, wuvt=wuvt, cmn=cm_norm[None],
                ws=cm_ws.astype(BF16), bias=bias)


def _prep_odd(w_in, w_gf, b_gf, w_gb, b_gb):
    d = w_in.shape[0]
    o_k = GLA_HEADS * GLA_DK
    o_v = o_k + GLA_HEADS * GLA_DV
    o_zb = o_v + 2 * GLA_GATE_RANK
    o_q = o_zb + GLA_HEADS * GLA_DK
    wall = jnp.concatenate([w_in[:, o_v:o_zb], jnp.zeros((d, LANES - 2 * GLA_GATE_RANK), F32), w_in[:, o_q:],
                            w_in[:, o_zb:o_q], w_in[:, :o_v]], axis=1).astype(BF16)
    zr = GLA_GATE_RANK
    wgf = jnp.zeros((LANES, o_k), F32).at[:zr].set(w_gf).astype(BF16)
    wgb = jnp.zeros((LANES, o_k), F32).at[zr:2 * zr].set(w_gb).astype(BF16)
    return dict(wall=wall, wgf=wgf, bgf=b_gf[None], wgb=wgb, bgb=b_gb[None])


def _rope_tables(length):
    rows = length // GRID_W
    r = jnp.repeat(jnp.arange(rows, dtype=F32), GRID_W)
    col = jnp.tile(jnp.arange(GRID_W, dtype=F32), rows)
    half = MLA_ROPE // 2
    inv = ROPE_BASE ** (-jnp.arange(0, half, 2, dtype=F32) / half)
    ang_r = r[:, None] * inv
    ang_c = col[:, None] * inv
    ang = jnp.concatenate([ang_r, ang_r, ang_c, ang_c], axis=-1)
    one = jnp.ones((length, MLA_NOPE), F32)
    pad = jnp.zeros((length, HEAD_SLOT - MLA_NOPE - MLA_ROPE), F32)
    ta = jnp.concatenate([one, jnp.cos(ang), pad], axis=1)
    tb = jnp.concatenate([0.0 * one, jnp.sin(ang), pad], axis=1)
    return ta, tb, jnp.cos(ang).T, jnp.sin(ang).T


def _flat_tables(length):
    ta = jnp.concatenate([jnp.ones((length, MLA_NOPE + MLA_ROPE), F32),
                          jnp.zeros((length, HEAD_SLOT - MLA_NOPE - MLA_ROPE), F32)], axis=1)
    return ta, jnp.zeros_like(ta), jnp.ones((MLA_ROPE, length), F32), jnp.zeros((MLA_ROPE, length), F32)


def _row_tile(seq, want):
    t = min(seq, want)
    while seq % t:
        t //= 2
    return t


def kernel(x, c, ctx, c_ctx, ada_w, ada_b, norm1_g, norm2_g, mlp_w1, mlp_w2, ev_w_in, ev_q_norm, ev_w_uq, ev_kv_norm,
           ev_w_ukv, ev_cm_norm, ev_cm_ws, ev_cm_bs, ev_w_out, od_w_in, od_w_gf, od_b_gf, od_w_gb, od_b_gb, od_o_norm,
           od_w_out, final_g):
    batch, seq, d = x.shape
    lc = ctx.shape[1]
    depth = ada_w.shape[0]
    tm_l = _row_tile(seq, 512)
    tm_c = _row_tile(lc, 256)
    tm_e = tm_l
    tk_l = min(MAX_KEY_TILE, lc)
    assert seq % tk_l == 0 and lc % tk_l == 0
    tq_l = _row_tile(seq, 512)
    tq_c = _row_tile(lc, 256)
    tb_l = _row_tile(seq, 512)
    tb_c = _row_tile(lc, 256)

    cvec = jnp.concatenate([c, c_ctx[None], jnp.zeros((8 - batch - 1, d), F32)], axis=0)
    mods = _ada_call(cvec, ada_w, ada_b).reshape(depth, 8, 6, d)

    tabs_l = _rope_tables(seq)
    tabs_c = _flat_tables(lc)

    xl = x.reshape(batch * seq, d)
    xc = ctx.reshape(batch * lc, d)
    for i in range(depth):
        need_ctx = i < depth - 1
        j = i // 2
        ml = [mods[i, :batch, t][:, None, :] for t in range(6)]
        mc = [mods[i, batch:batch + 1, t][:, None, :] for t in range(6)]
        g1 = norm1_g[i][None]
        g2 = norm2_g[i][None]
        w1 = mlp_w1[i].astype(BF16)
        w2 = mlp_w2[i].astype(BF16)
        final = i == depth - 1
        if i % 2 == 0:
            w = _prep_even(ev_w_in[j], ev_q_norm[j], ev_w_uq[j], ev_kv_norm[j], ev_w_ukv[j], ev_cm_norm[j],
                           ev_cm_ws[j], ev_cm_bs[j])
            wo = ev_w_out[j].astype(BF16)
            qtl, kl, vtl, mll = _even_in_call(xl, g1, ml[0], ml[1], tabs_l, w, seq=seq, tm=tm_e)
            qtc, kc, vtc, mlc = _even_in_call(xc, g1, mc[0], mc[1], tabs_c, w, seq=batch * lc, tm=tm_c)
            al = _attn_call(qtl, [(kl, vtl), (kc, vtc)], batch=batch, tq=tq_l, tk=tk_l)
            xl = _post_call(xl, al, mll, 0, 0, ml[2], g2, ml[3], ml[4], ml[5], wo, w1, w2, final_g[None],
                            seq=seq, tm=tm_l, final=final)
            if need_ctx:
                ac = _attn_call(qtc, [(kc, vtc)], batch=batch, tq=tq_c, tk=_key_tile(lc))
                xc = _post_call(xc, ac, mlc, 0, 0, mc[2], g2, mc[3], mc[4], mc[5], wo, w1, w2, final_g[None],
                                seq=batch * lc, tm=tm_c, final=False)
        else:
            w = _prep_odd(od_w_in[j], od_w_gf[j], od_b_gf[j], od_w_gb[j], od_b_gb[j])
            wo = od_w_out[j].astype(BF16)
            on = od_o_norm[j][None]
            kc, vc, qc, rc, dfc, dbc = _odd_in_call(xc, g1, mc[0], mc[1], w, seq=batch * lc, tm=tm_c)
            kl, vl, ql, rl, dfl, dbl = _odd_in_call(xl, g1, ml[0], ml[1], w, seq=seq, tm=tm_l)
            r3 = lambda t, n: t.reshape(batch, n, t.shape[-1])
            s0 = jnp.zeros((batch, GLA_HEADS, GLA_DV, GLA_DK), F32)
            ocf, s_f = _gla_call(r3(qc, lc), r3(kc, lc), r3(dfc, lc), r3(vc, lc), s0, reverse=False, tb=tb_c)
            mixc, s_b = _gla_call(r3(qc, lc), r3(kc, lc), r3(dbc, lc), r3(vc, lc), s0, reverse=True, tb=tb_c,
                                  o_fwd=ocf, r=r3(rc, lc), o_norm=on)
            olf, _ = _gla_call(r3(ql, seq), r3(kl, seq), r3(dfl, seq), r3(vl, seq), s_f, reverse=False, tb=tb_l)
            mixl, _ = _gla_call(r3(ql, seq), r3(kl, seq), r3(dbl, seq), r3(vl, seq), s_b, reverse=True, tb=tb_l,
                                o_fwd=olf, r=r3(rl, seq), o_norm=on)
            mixl = mixl.reshape(batch * seq, -1)
            xl = _post_call(xl, mixl, mixl, 0, 1, ml[2], g2, ml[3], ml[4], ml[5], wo, w1, w2, final_g[None],
                            seq=seq, tm=tm_l, final=final)
            if need_ctx:
                mixc = mixc.reshape(batch * lc, -1)
                xc = _post_call(xc, mixc, mixc, 0, 1, mc[2], g2, mc[3], mc[4], mc[5], wo, w1, w2, final_g[None],
                                seq=batch * lc, tm=tm_c, final=False)
    return xl.reshape(batch, seq, d)
```

```python
import functools

import jax
import jax.numpy as jnp
from jax import lax
from jax.experimental import pallas as pl
from jax.experimental.pallas import tpu as pltpu

F32 = jnp.float32
BF16 = jnp.bfloat16

D_MODEL = 1024
DEPTH = 4
GRID_W = 64
EPS = 1e-6
MLA_HEADS = 8
MLA_NOPE = 64
MLA_ROPE = 32
MLA_V = 64
MLA_Q_RANK = 384
MLA_KV_RANK = 256
MLA_SCALE = (MLA_NOPE + MLA_ROPE) ** -0.5
ROPE_BASE = 10000.0
CM_CHUNK = 128
CM_GROUPS = 4
CM_GROUP_DIM = 128
CM_WIDTH = CM_GROUPS * CM_GROUP_DIM
GLA_HEADS = 4
GLA_DK = 128
GLA_DV = 256
GLA_GATE_RANK = 16
GLA_TAU = 16.0
GLA_CHUNK = 64
GLA_PAIR = 2 * GLA_CHUNK
D_FF = 4 * D_MODEL

LANES = 128
HEAD_SLOT = LANES
BF16_ROWS = 16
V_SLOT = MLA_V + BF16_ROWS
VMEM_LIMIT = 56 * 1024 * 1024
MAX_KEY_TILE = 256
ATTN_UNROLL = 63
ATTN_SLOTS = 3
Q_PRESCALE = MLA_SCALE * 1.4426950408889634

EA_Q = 0
EA_KV = EA_Q + MLA_Q_RANK
EA_KR = EA_KV + MLA_KV_RANK
EA_U = EA_KR + LANES
EA_V = EA_U + CM_WIDTH
EA_END = EA_V + CM_WIDTH
OA_Z = 0
OA_R = OA_Z + LANES
OA_Q = OA_R + GLA_HEADS * GLA_DV
OA_K = OA_Q + GLA_HEADS * GLA_DK
OA_V = OA_K + GLA_HEADS * GLA_DK
OA_END = OA_V + GLA_HEADS * GLA_DV


def _cparams(sem):
    return pltpu.CompilerParams(dimension_semantics=sem, vmem_limit_bytes=VMEM_LIMIT)


def _rms(x, g):
    return x * lax.rsqrt(jnp.mean(x * x, axis=-1, keepdims=True) + EPS) * g


def _bdot(a, b):
    return jnp.dot(a.astype(BF16), b.astype(BF16), preferred_element_type=F32)


def _const_spec(shape):
    nd = len(shape)
    return pl.BlockSpec(shape, lambda *_: (0,) * nd, pipeline_mode=pl.Buffered(1))


def _ada_kernel(c_ref, w_ref, b_ref, o_ref):
    s = c_ref[...]
    s = s * jax.nn.sigmoid(s)
    o_ref[0] = _bdot(s, w_ref[0]) + b_ref[0]


def _ada_call(cvec, ada_w, ada_b):
    depth, d, n = ada_w.shape
    tn = 1536
    return pl.pallas_call(
        _ada_kernel,
        grid=(depth, n // tn),
        in_specs=[
            pl.BlockSpec((8, d), lambda l, j: (0, 0)),
            pl.BlockSpec((1, d, tn), lambda l, j: (l, 0, j)),
            pl.BlockSpec((1, 1, tn), lambda l, j: (l, 0, j)),
        ],
        out_specs=pl.BlockSpec((1, 8, tn), lambda l, j: (l, 0, j)),
        out_shape=jax.ShapeDtypeStruct((depth, 8, n), F32),
        compiler_params=_cparams(("arbitrary", "arbitrary")),
        name="ada_mod",
    )(cvec, ada_w, ada_b.reshape(depth, 1, n))


def _rope_slot(t, a, b):
    return t * a + pltpu.roll(t, HEAD_SLOT - MLA_ROPE, 1) * b


def _dot_nt(a, b):
    return lax.dot_general(a, b, (((1,), (1,)), ((), ())), preferred_element_type=F32)


def _even_in_kernel(x_ref, g1_ref, sh_ref, sc_ref, ta_ref, tb_ref, ct_ref, st_ref, wall_ref, qn_ref, wuqt_ref,
                    kvn_ref, wuk_ref, wuvt_ref, cmn_ref, ws_ref, bias_ref, qt_ref, k_ref, vt_ref, ml_ref, *, tm):
    h = _rms(x_ref[...], g1_ref[...]) * (1.0 + sc_ref[0]) + sh_ref[0]
    p = _bdot(h, wall_ref[...])

    cq = _rms(p[:, EA_Q:EA_KV], qn_ref[...]).astype(BF16)
    qt = _dot_nt(wuqt_ref[...], cq)
    cos_t = ct_ref[...]
    sin_t = st_ref[...]
    r0, r1 = MLA_NOPE, MLA_NOPE + MLA_ROPE
    for hd in range(MLA_HEADS):
        b = hd * HEAD_SLOT
        qt_ref[b:b + r0, :] = (qt[b:b + r0, :] * Q_PRESCALE).astype(BF16)
        roped = qt[b + r0:b + r1, :] * cos_t + qt[b + r1:b + HEAD_SLOT, :] * sin_t
        qt_ref[b + r0:b + r1, :] = (roped * Q_PRESCALE).astype(BF16)
        qt_ref[b + r1:b + HEAD_SLOT, :] = jnp.zeros((HEAD_SLOT - r1, tm), BF16)

    ckv = _rms(p[:, EA_KV:EA_KR], kvn_ref[...]).astype(BF16)
    kn = jnp.dot(ckv, wuk_ref[...], preferred_element_type=F32)
    kr = _rope_slot(pltpu.roll(p[:, EA_KR:EA_U], MLA_NOPE, 1), ta_ref[...], tb_ref[...])
    for hd in range(MLA_HEADS):
        sl = slice(hd * HEAD_SLOT, (hd + 1) * HEAD_SLOT)
        k_ref[:, sl] = (kn[:, sl] + kr).astype(BF16)
    vt = _dot_nt(wuvt_ref[...], ckv)
    for hd in range(MLA_HEADS):
        b = hd * V_SLOT
        vt_ref[b:b + MLA_V, :] = vt[hd * MLA_V:(hd + 1) * MLA_V, :].astype(BF16)
        vt_ref[b + MLA_V:b + V_SLOT, :] = jnp.ones((V_SLOT - MLA_V, tm), BF16)

    u = jax.nn.gelu(p[:, EA_U:EA_V])
    vv = jax.nn.gelu(p[:, EA_V:EA_END])
    cmn = cmn_ref[...]
    for g in range(CM_GROUPS):
        gl = slice(g * CM_GROUP_DIM, (g + 1) * CM_GROUP_DIM)
        vn = _rms(vv[:, gl], cmn).astype(BF16)
        w = ws_ref[g]
        for c in range(tm // CM_CHUNK):
            rs = slice(c * CM_CHUNK, (c + 1) * CM_CHUNK)
            y = jnp.dot(w, vn[rs], preferred_element_type=F32) + bias_ref[:, gl]
            ml_ref[rs, gl] = (u[rs, gl] * y).astype(BF16)


def _even_in_call(x, g1, sh, sc, tabs, w, *, seq, tm):
    ta, tb, cos_t, sin_t = tabs
    n, d = x.shape
    tpb = seq // tm
    tpt = ta.shape[0] // tm
    row = lambda i: (i, 0)
    col = lambda i: (0, i)
    bat = lambda i: (i // tpb, 0, 0)
    tab = lambda i: (i % tpt, 0)
    tab_t = lambda i: (0, i % tpt)
    hq = MLA_HEADS * HEAD_SLOT
    hv = MLA_HEADS * MLA_V
    in_specs = [
        pl.BlockSpec((tm, d), row),
        _const_spec((1, d)),
        pl.BlockSpec((1, 1, d), bat),
        pl.BlockSpec((1, 1, d), bat),
        pl.BlockSpec((tm, HEAD_SLOT), tab),
        pl.BlockSpec((tm, HEAD_SLOT), tab),
        pl.BlockSpec((MLA_ROPE, tm), tab_t),
        pl.BlockSpec((MLA_ROPE, tm), tab_t),
        _const_spec((d, EA_END)),
        _const_spec((1, MLA_Q_RANK)),
        _const_spec((hq, MLA_Q_RANK)),
        _const_spec((1, MLA_KV_RANK)),
        _const_spec((MLA_KV_RANK, hq)),
        _const_spec((hv, MLA_KV_RANK)),
        _const_spec((1, CM_GROUP_DIM)),
        _const_spec((CM_GROUPS, CM_CHUNK, CM_CHUNK)),
        _const_spec((CM_CHUNK, CM_WIDTH)),
    ]
    args = [x, g1, sh, sc, ta, tb, cos_t, sin_t, w["wall"], w["qn"], w["wuqt"], w["kvn"], w["wuk"], w["wuvt"],
            w["cmn"], w["ws"], w["bias"]]
    return pl.pallas_call(
        functools.partial(_even_in_kernel, tm=tm),
        grid=(n // tm,),
        in_specs=in_specs,
        out_specs=[
            pl.BlockSpec((hq, tm), col),
            pl.BlockSpec((tm, hq), row),
            pl.BlockSpec((MLA_HEADS * V_SLOT, tm), col),
            pl.BlockSpec((tm, CM_WIDTH), row),
        ],
        out_shape=[
            jax.ShapeDtypeStruct((hq, n), BF16),
            jax.ShapeDtypeStruct((n, hq), BF16),
            jax.ShapeDtypeStruct((MLA_HEADS * V_SLOT, n), BF16),
            jax.ShapeDtypeStruct((n, CM_WIDTH), BF16),
        ],
        compiler_params=_cparams(("arbitrary",)),
        name="even_in",
    )(*args)


def _attn_kernel(*refs, tq, tk, seg_tiles, unroll):
    nseg = len(seg_tiles)
    qt_ref = refs[0]
    kv_refs = [(refs[1 + 2 * i], refs[2 + 2 * i]) for i in range(nseg)]
    o_ref, s_scr, p_scr, acc_scr = refs[1 + 2 * nseg:]
    nsl = s_scr.shape[1]
    nk = sum(seg_tiles)

    def tile(t):
        if not isinstance(t, int):
            return kv_refs[0] + (pl.multiple_of(t * tk, LANES),)
        for (k_ref, vt_ref), n in zip(kv_refs, seg_tiles):
            if t < n:
                return k_ref, vt_ref, t * tk
            t -= n
        raise ValueError("key tile out of range")

    def step(t, ph, h, state, scores=True, softmax=True, weighted=True):
        m, alpha, tile_max = state
        new_alpha, new_tile_max = alpha, tile_max
        if scores:
            k_ref, _, r0 = tile(t)
            ks = k_ref[pl.ds(r0, tk), h * HEAD_SLOT:(h + 1) * HEAD_SLOT]
            s = jnp.dot(ks, qt_ref[h * HEAD_SLOT:(h + 1) * HEAD_SLOT, :], preferred_element_type=F32)
            s_scr[h, ph] = s
            new_tile_max = jnp.max(s, axis=0, keepdims=True)
        if weighted:
            _, vt_ref, r0 = tile(t - 2)
            vs = vt_ref[h * V_SLOT:(h + 1) * V_SLOT, pl.ds(r0, tk)]
            acc_scr[h] = alpha * acc_scr[h] + jnp.dot(vs, p_scr[h, (ph - 2) % nsl], preferred_element_type=F32)
        if softmax:
            m_new = jnp.maximum(m, tile_max)
            new_alpha = jnp.exp2(m - m_new)
            p_scr[h, (ph - 1) % nsl] = jnp.exp2(s_scr[h, (ph - 1) % nsl] - m_new).astype(BF16)
            m = m_new
        return m, new_alpha, new_tile_max

    carry = []
    for h in range(2):
        acc_scr[h] = jnp.zeros((V_SLOT, tq), F32)
        neg = jnp.full((1, tq), -jnp.inf, F32)
        st = step(0, 0, h, (neg, jnp.zeros((1, tq), F32), neg), softmax=False, weighted=False)
        carry.append(step(1, 1 % nsl, h, st, weighted=False))

    assert unroll % nsl == 0
    full = max(seg_tiles[0] - 2, 0)
    groups = 0 if full <= unroll else full // unroll

    def body(g, carry):
        t0 = 2 + unroll * g
        out = []
        for h in range(2):
            st = carry[h]
            for u in range(unroll):
                st = step(t0 + u, (2 + u) % nsl, h, st)
            out.append(st)
        return tuple(out)

    carry = lax.fori_loop(0, groups, body, tuple(carry))
    outs = []
    for h in range(2):
        st = carry[h]
        for t in range(2 + unroll * groups, nk):
            st = step(t, t % nsl, h, st)
        st = step(nk, nk % nsl, h, st, scores=False)
        step(nk + 1, (nk + 1) % nsl, h, st, scores=False, softmax=False)
        acc = acc_scr[h]
        outs.append(acc[0:MLA_V, :] * (1.0 / acc[MLA_V:MLA_V + 1, :]))
    o_ref[...] = jnp.concatenate(outs, axis=0).T.astype(BF16)


def _key_tile(t_len):
    tk = (min(t_len // 2, MAX_KEY_TILE) // LANES) * LANES
    while t_len % tk:
        tk -= LANES
    return tk


def _attn_call(qt, kvs, *, batch, tq, tk):
    n = qt.shape[1]
    nq = n // batch // tq
    in_specs = [pl.BlockSpec((2 * HEAD_SLOT, tq), lambda b, hp, i: (hp, b * nq + i))]
    args = [qt]
    seg_tiles = []
    for k, vt in kvs:
        t_len = k.shape[0] // batch
        assert t_len % tk == 0
        seg_tiles.append(t_len // tk)
        in_specs.append(pl.BlockSpec((t_len, 2 * HEAD_SLOT), lambda b, hp, i: (b, hp)))
        in_specs.append(pl.BlockSpec((2 * V_SLOT, t_len), lambda b, hp, i: (hp, b)))
        args += [k, vt]
    assert sum(seg_tiles) >= 2
    return pl.pallas_call(
        functools.partial(_attn_kernel, tq=tq, tk=tk, seg_tiles=tuple(seg_tiles), unroll=ATTN_UNROLL),
        grid=(batch, MLA_HEADS // 2, nq),
        in_specs=in_specs,
        out_specs=pl.BlockSpec((tq, 2 * MLA_V), lambda b, hp, i: (b * nq + i, hp)),
        out_shape=jax.ShapeDtypeStruct((n, MLA_HEADS * MLA_V), BF16),
        scratch_shapes=[
            pltpu.VMEM((2, ATTN_SLOTS, tk, tq), F32),
            pltpu.VMEM((2, ATTN_SLOTS, tk, tq), BF16),
            pltpu.VMEM((2, V_SLOT, tq), F32),
        ],
        compiler_params=_cparams(("arbitrary", "arbitrary", "arbitrary")),
        name="mla_attn",
    )(*args)


def _post_kernel(x_ref, a_ref, b_ref, gm_ref, g2_ref, sh_ref, sc_ref, gf_ref, wo_ref, w1_ref, w2_ref, fg_ref,
                 o_ref, *, final):
    half = wo_ref.shape[0] // 2
    y = (jnp.dot(a_ref[...], wo_ref[:half, :], preferred_element_type=F32)
         + jnp.dot(b_ref[...], wo_ref[half:, :], preferred_element_type=F32))
    x1 = x_ref[...] + gm_ref[0] * y
    h2 = (_rms(x1, g2_ref[...]) * (1.0 + sc_ref[0]) + sh_ref[0]).astype(BF16)
    acc = jnp.zeros_like(x1)
    fc = 1024
    for c in range(w1_ref.shape[1] // fc):
        hc = jnp.dot(h2, w1_ref[:, c * fc:(c + 1) * fc], preferred_element_type=F32)
        hc = jnp.square(jnp.maximum(hc, 0.0)).astype(BF16)
        acc = acc + jnp.dot(hc, w2_ref[c * fc:(c + 1) * fc, :], preferred_element_type=F32)
    x2 = x1 + gf_ref[0] * acc
    if final:
        x2 = _rms(x2, fg_ref[...])
    o_ref[...] = x2


def _post_call(x, mix_a, mix_b, col_a, col_b, gm, g2, sh, sc, gf, wo, w1, w2, fg, *, seq, tm, final):
    n, d = x.shape
    tpb = seq // tm
    row = lambda i: (i, 0)
    bat = lambda i: (i // tpb, 0, 0)
    half = wo.shape[0] // 2
    return pl.pallas_call(
        functools.partial(_post_kernel, final=final),
        grid=(n // tm,),
        in_specs=[
            pl.BlockSpec((tm, d), row),
            pl.BlockSpec((tm, half), lambda i: (i, col_a)),
            pl.BlockSpec((tm, half), lambda i: (i, col_b)),
            pl.BlockSpec((1, 1, d), bat),
            _const_spec((1, d)),
            pl.BlockSpec((1, 1, d), bat),
            pl.BlockSpec((1, 1, d), bat),
            pl.BlockSpec((1, 1, d), bat),
            _const_spec(wo.shape),
            _const_spec(w1.shape),
            _const_spec(w2.shape),
            _const_spec((1, d)),
        ],
        out_specs=pl.BlockSpec((tm, d), row),
        out_shape=jax.ShapeDtypeStruct((n, d), F32),
        compiler_params=_cparams(("arbitrary",)),
        name="post_mlp",
    )(x, mix_a, mix_b, gm, g2, sh, sc, gf, wo, w1, w2, fg)


def _log_decay(g):
    return (jnp.minimum(g, 0.0) - jnp.log1p(jnp.exp(-jnp.abs(g)))) * (1.0 / GLA_TAU)


def _chunk_cumsum(tri, x):
    hi = x.astype(BF16)
    r1 = x - hi.astype(F32)
    mid = r1.astype(BF16)
    lo = (r1 - mid.astype(F32)).astype(BF16)
    return (jnp.dot(tri, hi, preferred_element_type=F32) + jnp.dot(tri, mid, preferred_element_type=F32)
            + jnp.dot(tri, lo, preferred_element_type=F32))


def _odd_in_kernel(x_ref, g1_ref, sh_ref, sc_ref, wall_ref, wgf_ref, bgf_ref, wgb_ref, bgb_ref,
                   k_ref, v_ref, q_ref, r_ref, cf_ref, cb_ref, *, tm):
    rr = lax.broadcasted_iota(jnp.int32, (GLA_PAIR, GLA_PAIR), 0)
    cc = lax.broadcasted_iota(jnp.int32, (GLA_PAIR, GLA_PAIR), 1)
    same = (rr // GLA_CHUNK) == (cc // GLA_CHUNK)
    tri_f = (same & (rr >= cc)).astype(BF16)
    tri_b = (same & (rr <= cc)).astype(BF16)
    h = _rms(x_ref[...], g1_ref[...]) * (1.0 + sc_ref[0]) + sh_ref[0]
    p = _bdot(h, wall_ref[...])
    k_ref[...] = p[:, OA_K:OA_V]
    v_ref[...] = p[:, OA_V:OA_END].astype(BF16)
    q_ref[...] = p[:, OA_Q:OA_K] * (GLA_DK ** -0.5)
    r = p[:, OA_R:OA_Q]
    r_ref[...] = r * jax.nn.sigmoid(r)
    z = p[:, OA_Z:OA_R].astype(BF16)
    df = _log_decay(jnp.dot(z, wgf_ref[...], preferred_element_type=F32) + bgf_ref[...])
    db = _log_decay(jnp.dot(z, wgb_ref[...], preferred_element_type=F32) + bgb_ref[...])
    for c in range(tm // GLA_PAIR):
        rs = slice(c * GLA_PAIR, (c + 1) * GLA_PAIR)
        cf_ref[rs, :] = _chunk_cumsum(tri_f, df[rs])
        cb_ref[rs, :] = _chunk_cumsum(tri_b, db[rs])


def _odd_in_call(x, g1, sh, sc, w, *, seq, tm):
    n, d = x.shape
    tpb = seq // tm
    row = lambda i: (i, 0)
    bat = lambda i: (i // tpb, 0, 0)
    kd = GLA_HEADS * GLA_DK
    vd = GLA_HEADS * GLA_DV
    return pl.pallas_call(
        functools.partial(_odd_in_kernel, tm=tm),
        grid=(n // tm,),
        in_specs=[
            pl.BlockSpec((tm, d), row),
            _const_spec((1, d)),
            pl.BlockSpec((1, 1, d), bat),
            pl.BlockSpec((1, 1, d), bat),
            _const_spec((d, OA_END)),
            _const_spec((LANES, kd)),
            _const_spec((1, kd)),
            _const_spec((LANES, kd)),
            _const_spec((1, kd)),
        ],
        out_specs=[
            pl.BlockSpec((tm, kd), row),
            pl.BlockSpec((tm, vd), row),
            pl.BlockSpec((tm, kd), row),
            pl.BlockSpec((tm, vd), row),
            pl.BlockSpec((tm, kd), row),
            pl.BlockSpec((tm, kd), row),
        ],
        out_shape=[
            jax.ShapeDtypeStruct((n, kd), F32),
            jax.ShapeDtypeStruct((n, vd), BF16),
            jax.ShapeDtypeStruct((n, kd), F32),
            jax.ShapeDtypeStruct((n, vd), F32),
            jax.ShapeDtypeStruct((n, kd), F32),
            jax.ShapeDtypeStruct((n, kd), F32),
        ],
        compiler_params=_cparams(("arbitrary",)),
        name="odd_in",
    )(x, g1, sh, sc, w["wall"], w["wgf"], w["bgf"], w["wgb"], w["bgb"])


def _dot_tn(a, b):
    return lax.dot_general(a, b, (((0,), (0,)), ((), ())), preferred_element_type=F32)


def _gla_kernel(*refs, reverse, npair, combine):
    if combine:
        q_ref, k_ref, c_ref, v_ref, s0_ref, of_ref, r_ref, on_ref, o_ref, sfin_ref, s_scr = refs
    else:
        q_ref, k_ref, c_ref, v_ref, s0_ref, o_ref, sfin_ref, s_scr = refs
    i = pl.program_id(1)

    @pl.when(i == 0)
    def _():
        s_scr[...] = s0_ref[...]

    ch, pair = GLA_CHUNK, GLA_PAIR
    rr = lax.broadcasted_iota(jnp.int32, (pair, pair), 0)
    cc = lax.broadcasted_iota(jnp.int32, (pair, pair), 1)
    mask = ((rr // ch) == (cc // ch)) & ((rr <= cc) if reverse else (rr >= cc))
    lo = lax.broadcasted_iota(jnp.int32, (pair, GLA_DK), 0) < ch
    for pr in (range(npair - 1, -1, -1) if reverse else range(npair)):
        rs = slice(pr * pair, (pr + 1) * pair)
        for h in range(GLA_HEADS):
            ks = slice(h * GLA_DK, (h + 1) * GLA_DK)
            vs = slice(h * GLA_DV, (h + 1) * GLA_DV)
            bc = c_ref[rs, ks]
            k = k_ref[rs, ks]
            v = v_ref[rs, vs]
            if reverse:
                t_lo, t_hi = bc[0:1, :], bc[ch:ch + 1, :]
            else:
                t_lo, t_hi = bc[ch - 1:ch, :], bc[pair - 1:pair, :]
            q_t = (q_ref[rs, ks] * jnp.exp(bc)).astype(BF16)
            k_t = (k * jnp.exp(-bc)).astype(BF16)
            k_end = k * jnp.exp(jnp.where(lo, t_lo, t_hi) - bc)
            a = jnp.where(mask, _dot_nt(q_t, k_t), 0.0).astype(BF16)
            o = jnp.dot(a, v, preferred_element_type=F32)
            st = s_scr[h]
            halves = [(slice(0, ch), lo, t_lo), (slice(ch, pair), ~lo, t_hi)]
            inter = {}
            for hs, hm, tot in (halves[::-1] if reverse else halves):
                inter[hs.start] = _dot_nt(q_t[hs], st.astype(BF16))
                st = jnp.exp(tot) * st + _dot_tn(v, jnp.where(hm, k_end, 0.0).astype(BF16))
            s_scr[h] = st
            o = o + jnp.concatenate([inter[0], inter[ch]], axis=0)
            if combine:
                o = _rms(o + of_ref[rs, vs], on_ref[...]) * r_ref[rs, vs]
            o_ref[rs, vs] = o.astype(o_ref.dtype)

    @pl.when(i == pl.num_programs(1) - 1)
    def _():
        sfin_ref[...] = s_scr[...]


def _gla_call(q, k, c, v, s0, *, reverse, tb, o_fwd=None, r=None, o_norm=None):
    b, l, _ = q.shape
    nb = l // tb
    combine = o_fwd is not None
    blk = (lambda bi, i: (bi, nb - 1 - i, 0)) if reverse else (lambda bi, i: (bi, i, 0))
    kspec = pl.BlockSpec((None, tb, GLA_HEADS * GLA_DK), blk)
    vspec = pl.BlockSpec((None, tb, GLA_HEADS * GLA_DV), blk)
    sspec = pl.BlockSpec((None, GLA_HEADS, GLA_DV, GLA_DK), lambda bi, i: (bi, 0, 0, 0))
    in_specs = [kspec, kspec, kspec, vspec, sspec]
    args = [q, k, c, v, s0]
    if combine:
        in_specs += [vspec, vspec, pl.BlockSpec((1, GLA_DV), lambda bi, i: (0, 0))]
        args += [o_fwd, r, o_norm]
    return pl.pallas_call(
        functools.partial(_gla_kernel, reverse=reverse, npair=tb // GLA_PAIR, combine=combine),
        grid=(b, nb),
        in_specs=in_specs,
        out_specs=[vspec, sspec],
        out_shape=[
            jax.ShapeDtypeStruct((b, l, GLA_HEADS * GLA_DV), BF16 if combine else F32),
            jax.ShapeDtypeStruct((b, GLA_HEADS, GLA_DV, GLA_DK), F32),
        ],
        scratch_shapes=[pltpu.VMEM((GLA_HEADS, GLA_DV, GLA_DK), F32)],
        compiler_params=_cparams(("arbitrary", "arbitrary")),
        name="gla_bwd" if reverse else "gla_fwd",
    )(*args)


def _rot_cols(w):
    q = MLA_ROPE // 4
    return jnp.concatenate([-w[..., q:2 * q], w[..., 0:q], -w[..., 3 * q:4 * q], w[..., 2 * q:3 * q]], axis=-1)


def _prep_even(w_in, q_norm, w_uq, kv_norm, w_ukv, cm_norm, cm_ws, cm_bs):
    d = w_in.shape[0]
    e_q, e_kv = MLA_Q_RANK, MLA_Q_RANK + MLA_KV_RANK
    e_r = e_kv + MLA_ROPE
    e_u = e_r + CM_WIDTH
    wkr = w_in[:, e_kv:e_r]
    wall = jnp.concatenate([w_in[:, :e_kv], wkr, _rot_cols(wkr), jnp.zeros((d, LANES - 2 * MLA_ROPE), F32),
                            w_in[:, e_r:e_u], w_in[:, e_u:]], axis=1).astype(BF16)
    uq = w_uq.reshape(MLA_Q_RANK, MLA_HEADS, MLA_NOPE + MLA_ROPE)
    uq_r = uq[..., MLA_NOPE:]
    wuqt = jnp.concatenate([uq, _rot_cols(uq_r)], axis=-1).reshape(MLA_Q_RANK, MLA_HEADS * HEAD_SLOT).T.astype(BF16)
    ukv = w_ukv.reshape(MLA_KV_RANK, MLA_HEADS, MLA_NOPE + MLA_V)
    wuk = jnp.concatenate([ukv[..., :MLA_NOPE], jnp.zeros((MLA_KV_RANK, MLA_HEADS, HEAD_SLOT - MLA_NOPE), F32)],
                          axis=-1).reshape(MLA_KV_RANK, MLA_HEADS * HEAD_SLOT).astype(BF16)
    wuvt = ukv[..., MLA_NOPE:].reshape(MLA_KV_RANK, MLA_HEADS * MLA_V).T.astype(BF16)
    bias = jnp.repeat(cm_bs.T, CM_GROUP_DIM, axis=1)
    return dict(wall=wall, qn=q_norm[None], wuqt=wuqt, kvn=kv_norm[None], wuk=wuk, wuvt=wuvt, cmn=cm_norm[None],
                ws=cm_ws.astype(BF16), bias=bias)


def _prep_odd(w_in, w_gf, b_gf, w_gb, b_gb):
    d = w_in.shape[0]
    o_k = GLA_HEADS * GLA_DK
    o_v = o_k + GLA_HEADS * GLA_DV
    o_zb = o_v + 2 * GLA_GATE_RANK
    o_q = o_zb + GLA_HEADS * GLA_DK
    wall = jnp.concatenate([w_in[:, o_v:o_zb], jnp.zeros((d, LANES - 2 * GLA_GATE_RANK), F32), w_in[:, o_q:],
                            w_in[:, o_zb:o_q], w_in[:, :o_v]], axis=1).astype(BF16)
    zr = GLA_GATE_RANK
    wgf = jnp.zeros((LANES, o_k), F32).at[:zr].set(w_gf).astype(BF16)
    wgb = jnp.zeros((LANES, o_k), F32).at[zr:2 * zr].set(w_gb).astype(BF16)
    return dict(wall=wall, wgf=wgf, bgf=b_gf[None], wgb=wgb, bgb=b_gb[None])


def _rope_tables(length):
    rows = length // GRID_W
    r = jnp.repeat(jnp.arange(rows, dtype=F32), GRID_W)
    col = jnp.tile(jnp.arange(GRID_W, dtype=F32), rows)
    half = MLA_ROPE // 2
    inv = ROPE_BASE ** (-jnp.arange(0, half, 2, dtype=F32) / half)
    ang_r = r[:, None] * inv
    ang_c = col[:, None] * inv
    ang = jnp.concatenate([ang_r, ang_r, ang_c, ang_c], axis=-1)
    one = jnp.ones((length, MLA_NOPE), F32)
    pad = jnp.zeros((length, HEAD_SLOT - MLA_NOPE - MLA_ROPE), F32)
    ta = jnp.concatenate([one, jnp.cos(ang), pad], axis=1)
    tb = jnp.concatenate([0.0 * one, jnp.sin(ang), pad], axis=1)
    return ta, tb, jnp.cos(ang).T, jnp.sin(ang).T


def _flat_tables(length):
    ta = jnp.concatenate([jnp.ones((length, MLA_NOPE + MLA_ROPE), F32),
                          jnp.zeros((length, HEAD_SLOT - MLA_NOPE - MLA_ROPE), F32)], axis=1)
    return ta, jnp.zeros_like(ta), jnp.ones((MLA_ROPE, length), F32), jnp.zeros((MLA_ROPE, length), F32)


def _row_tile(seq, want):
    t = min(seq, want)
    while seq % t:
        t //= 2
    return t


def kernel(x, c, ctx, c_ctx, ada_w, ada_b, norm1_g, norm2_g, mlp_w1, mlp_w2, ev_w_in, ev_q_norm, ev_w_uq, ev_kv_norm,
           ev_w_ukv, ev_cm_norm, ev_cm_ws, ev_cm_bs, ev_w_out, od_w_in, od_w_gf, od_b_gf, od_w_gb, od_b_gb, od_o_norm,
           od_w_out, final_g):
    batch, seq, d = x.shape
    lc = ctx.shape[1]
    depth = ada_w.shape[0]
    tm_l = _row_tile(seq, 512)
    tm_c = _row_tile(lc, 256)
    tm_e = tm_l
    tk_l = min(MAX_KEY_TILE, lc)
    assert seq % tk_l == 0 and lc % tk_l == 0
    tq_l = _row_tile(seq, 512)
    tq_c = _row_tile(lc, 256)
    tb_l = _row_tile(seq, 1024)
    tb_c = _row_tile(lc, 256)

    cvec = jnp.concatenate([c, c_ctx[None], jnp.zeros((8 - batch - 1, d), F32)], axis=0)
    mods = _ada_call(cvec, ada_w, ada_b).reshape(depth, 8, 6, d)

    tabs_l = _rope_tables(seq)
    tabs_c = _flat_tables(lc)

    xl = x.reshape(batch * seq, d)
    xc = ctx.reshape(batch * lc, d)
    for i in range(depth):
        need_ctx = i < depth - 1
        j = i // 2
        ml = [mods[i, :batch, t][:, None, :] for t in range(6)]
        mc = [mods[i, batch:batch + 1, t][:, None, :] for t in range(6)]
        g1 = norm1_g[i][None]
        g2 = norm2_g[i][None]
        w1 = mlp_w1[i].astype(BF16)
        w2 = mlp_w2[i].astype(BF16)
        final = i == depth - 1
        if i % 2 == 0:
            w = _prep_even(ev_w_in[j], ev_q_norm[j], ev_w_uq[j], ev_kv_norm[j], ev_w_ukv[j], ev_cm_norm[j],
                           ev_cm_ws[j], ev_cm_bs[j])
            wo = ev_w_out[j].astype(BF16)
            qtl, kl, vtl, mll = _even_in_call(xl, g1, ml[0], ml[1], tabs_l, w, seq=seq, tm=tm_e)
            qtc, kc, vtc, mlc = _even_in_call(xc, g1, mc[0], mc[1], tabs_c, w, seq=batch * lc, tm=tm_c)
            al = _attn_call(qtl, [(kl, vtl), (kc, vtc)], batch=batch, tq=tq_l, tk=tk_l)
            xl = _post_call(xl, al, mll, 0, 0, ml[2], g2, ml[3], ml[4], ml[5], wo, w1, w2, final_g[None],
                            seq=seq, tm=tm_l, final=final)
            if need_ctx:
                ac = _attn_call(qtc, [(kc, vtc)], batch=batch, tq=tq_c, tk=_key_tile(lc))
                xc = _post_call(xc, ac, mlc, 0, 0, mc[2], g2, mc[3], mc[4], mc[5], wo, w1, w2, final_g[None],
                                seq=batch * lc, tm=tm_c, final=False)
        else:
            w = _prep_odd(od_w_in[j], od_w_gf[j], od_b_gf[j], od_w_gb[j], od_b_gb[j])
            wo = od_w_out[j].astype(BF16)
            on = od_o_norm[j][None]
            kc, vc, qc, rc, dfc, dbc = _odd_in_call(xc, g1, mc[0], mc[1], w, seq=batch * lc, tm=tm_c)
            kl, vl, ql, rl, dfl, dbl = _odd_in_call(xl, g1, ml[0], ml[1], w, seq=seq, tm=tm_l)
            r3 = lambda t, n: t.reshape(batch, n, t.shape[-1])
            s0 = jnp.zeros((batch, GLA_HEADS, GLA_DV, GLA_DK), F32)
            ocf, s_f = _gla_call(r3(qc, lc), r3(kc, lc), r3(dfc, lc), r3(vc, lc), s0, reverse=False, tb=tb_c)
            mixc, s_b = _gla_call(r3(qc, lc), r3(kc, lc), r3(dbc, lc), r3(vc, lc), s0, reverse=True, tb=tb_c,
                                  o_fwd=ocf, r=r3(rc, lc), o_norm=on)
            olf, _ = _gla_call(r3(ql, seq), r3(kl, seq), r3(dfl, seq), r3(vl, seq), s_f, reverse=False, tb=tb_l)
            mixl, _ = _gla_call(r3(ql, seq), r3(kl, seq), r3(dbl, seq), r3(vl, seq), s_b, reverse=True, tb=tb_l,
                                o_fwd=olf, r=r3(rl, seq), o_norm=on)
            mixl = mixl.reshape(batch * seq, -1)
            xl = _post_call(xl, mixl, mixl, 0, 1, ml[2], g2, ml[3], ml[4], ml[5], wo, w1, w2, final_g[None],
                            seq=seq, tm=tm_l, final=final)
            if need_ctx:
                mixc = mixc.reshape(batch * lc, -1)
                xc = _post_call(xc, mixc, mixc, 0, 1, mc[2], g2, mc[3], mc[4], mc[5], wo, w1, w2, final_g[None],
                                seq=batch * lc, tm=tm_c, final=False)
    return xl.reshape(batch, seq, d)
```

```python
import functools

import jax
import jax.numpy as jnp
from jax import lax
from jax.experimental import pallas as pl
from jax.experimental.pallas import tpu as pltpu

F32 = jnp.float32
BF16 = jnp.bfloat16

D_MODEL = 1024
DEPTH = 4
GRID_W = 64
EPS = 1e-6
MLA_HEADS = 8
MLA_NOPE = 64
MLA_ROPE = 32
MLA_V = 64
MLA_Q_RANK = 384
MLA_KV_RANK = 256
MLA_SCALE = (MLA_NOPE + MLA_ROPE) ** -0.5
ROPE_BASE = 10000.0
CM_CHUNK = 128
CM_GROUPS = 4
CM_GROUP_DIM = 128
CM_WIDTH = CM_GROUPS * CM_GROUP_DIM
GLA_HEADS = 4
GLA_DK = 128
GLA_DV = 256
GLA_GATE_RANK = 16
GLA_TAU = 16.0
GLA_CHUNK = 64
GLA_PAIR = 2 * GLA_CHUNK
D_FF = 4 * D_MODEL

LANES = 128
HEAD_SLOT = LANES
BF16_ROWS = 16
V_SLOT = MLA_V + BF16_ROWS
VMEM_LIMIT = 56 * 1024 * 1024
MAX_KEY_TILE = 256
ATTN_UNROLL = 63
ATTN_SLOTS = 3
Q_PRESCALE = MLA_SCALE * 1.4426950408889634

EA_Q = 0
EA_KV = EA_Q + MLA_Q_RANK
EA_KR = EA_KV + MLA_KV_RANK
EA_U = EA_KR + LANES
EA_V = EA_U + CM_WIDTH
EA_END = EA_V + CM_WIDTH
OA_Z = 0
OA_R = OA_Z + LANES
OA_Q = OA_R + GLA_HEADS * GLA_DV
OA_K = OA_Q + GLA_HEADS * GLA_DK
OA_V = OA_K + GLA_HEADS * GLA_DK
OA_END = OA_V + GLA_HEADS * GLA_DV


def _cparams(sem):
    return pltpu.CompilerParams(dimension_semantics=sem, vmem_limit_bytes=VMEM_LIMIT)


def _rms(x, g):
    return x * lax.rsqrt(jnp.mean(x * x, axis=-1, keepdims=True) + EPS) * g


def _bdot(a, b):
    return jnp.dot(a.astype(BF16), b.astype(BF16), preferred_element_type=F32)


def _const_spec(shape):
    nd = len(shape)
    return pl.BlockSpec(shape, lambda *_: (0,) * nd, pipeline_mode=pl.Buffered(1))


def _ada_kernel(c_ref, w_ref, b_ref, o_ref):
    s = c_ref[...]
    s = s * jax.nn.sigmoid(s)
    o_ref[0] = _bdot(s, w_ref[0]) + b_ref[0]


def _ada_call(cvec, ada_w, ada_b):
    depth, d, n = ada_w.shape
    tn = 1536
    return pl.pallas_call(
        _ada_kernel,
        grid=(depth, n // tn),
        in_specs=[
            pl.BlockSpec((8, d), lambda l, j: (0, 0)),
            pl.BlockSpec((1, d, tn), lambda l, j: (l, 0, j)),
            pl.BlockSpec((1, 1, tn), lambda l, j: (l, 0, j)),
        ],
        out_specs=pl.BlockSpec((1, 8, tn), lambda l, j: (l, 0, j)),
        out_shape=jax.ShapeDtypeStruct((depth, 8, n), F32),
        compiler_params=_cparams(("arbitrary", "arbitrary")),
        name="ada_mod",
    )(cvec, ada_w, ada_b.reshape(depth, 1, n))


def _rope_slot(t, a, b):
    return t * a + pltpu.roll(t, HEAD_SLOT - MLA_ROPE, 1) * b


def _dot_nt(a, b):
    return lax.dot_general(a, b, (((1,), (1,)), ((), ())), preferred_element_type=F32)


def _even_in_kernel(x_ref, g1_ref, sh_ref, sc_ref, ta_ref, tb_ref, ct_ref, st_ref, wall_ref, qn_ref, wuqt_ref,
                    kvn_ref, wuk_ref, wuvt_ref, cmn_ref, ws_ref, bias_ref, qt_ref, k_ref, vt_ref, ml_ref, *, tm):
    h = _rms(x_ref[...], g1_ref[...]) * (1.0 + sc_ref[0]) + sh_ref[0]
    p = _bdot(h, wall_ref[...])

    cq = _rms(p[:, EA_Q:EA_KV], qn_ref[...]).astype(BF16)
    qt = _dot_nt(wuqt_ref[...], cq)
    cos_t = ct_ref[...]
    sin_t = st_ref[...]
    r0, r1 = MLA_NOPE, MLA_NOPE + MLA_ROPE
    for hd in range(MLA_HEADS):
        b = hd * HEAD_SLOT
        qt_ref[b:b + r0, :] = (qt[b:b + r0, :] * Q_PRESCALE).astype(BF16)
        roped = qt[b + r0:b + r1, :] * cos_t + qt[b + r1:b + HEAD_SLOT, :] * sin_t
        qt_ref[b + r0:b + r1, :] = (roped * Q_PRESCALE).astype(BF16)
        qt_ref[b + r1:b + HEAD_SLOT, :] = jnp.zeros((HEAD_SLOT - r1, tm), BF16)

    ckv = _rms(p[:, EA_KV:EA_KR], kvn_ref[...]).astype(BF16)
    kn = jnp.dot(ckv, wuk_ref[...], preferred_element_type=F32)
    kr = _rope_slot(pltpu.roll(p[:, EA_KR:EA_U], MLA_NOPE, 1), ta_ref[...], tb_ref[...])
    for hd in range(MLA_HEADS):
        sl = slice(hd * HEAD_SLOT, (hd + 1) * HEAD_SLOT)
        k_ref[:, sl] = (kn[:, sl] + kr).astype(BF16)
    vt = _dot_nt(wuvt_ref[...], ckv)
    for hd in range(MLA_HEADS):
        b = hd * V_SLOT
        vt_ref[b:b + MLA_V, :] = vt[hd * MLA_V:(hd + 1) * MLA_V, :].astype(BF16)
        vt_ref[b + MLA_V:b + V_SLOT, :] = jnp.ones((V_SLOT - MLA_V, tm), BF16)

    u = jax.nn.gelu(p[:, EA_U:EA_V])
    vv = jax.nn.gelu(p[:, EA_V:EA_END])
    cmn = cmn_ref[...]
    for g in range(CM_GROUPS):
        gl = slice(g * CM_GROUP_DIM, (g + 1) * CM_GROUP_DIM)
        vn = _rms(vv[:, gl], cmn).astype(BF16)
        w = ws_ref[g]
        for c in range(tm // CM_CHUNK):
            rs = slice(c * CM_CHUNK, (c + 1) * CM_CHUNK)
            y = jnp.dot(w, vn[rs], preferred_element_type=F32) + bias_ref[:, gl]
            ml_ref[rs, gl] = (u[rs, gl] * y).astype(BF16)


def _even_in_call(x, g1, sh, sc, tabs, w, *, seq, tm):
    ta, tb, cos_t, sin_t = tabs
    n, d = x.shape
    tpb = seq // tm
    tpt = ta.shape[0] // tm
    row = lambda i: (i, 0)
    col = lambda i: (0, i)
    bat = lambda i: (i // tpb, 0, 0)
    tab = lambda i: (i % tpt, 0)
    tab_t = lambda i: (0, i % tpt)
    hq = MLA_HEADS * HEAD_SLOT
    hv = MLA_HEADS * MLA_V
    in_specs = [
        pl.BlockSpec((tm, d), row),
        _const_spec((1, d)),
        pl.BlockSpec((1, 1, d), bat),
        pl.BlockSpec((1, 1, d), bat),
        pl.BlockSpec((tm, HEAD_SLOT), tab),
        pl.BlockSpec((tm, HEAD_SLOT), tab),
        pl.BlockSpec((MLA_ROPE, tm), tab_t),
        pl.BlockSpec((MLA_ROPE, tm), tab_t),
        _const_spec((d, EA_END)),
        _const_spec((1, MLA_Q_RANK)),
        _const_spec((hq, MLA_Q_RANK)),
        _const_spec((1, MLA_KV_RANK)),
        _const_spec((MLA_KV_RANK, hq)),
        _const_spec((hv, MLA_KV_RANK)),
        _const_spec((1, CM_GROUP_DIM)),
        _const_spec((CM_GROUPS, CM_CHUNK, CM_CHUNK)),
        _const_spec((CM_CHUNK, CM_WIDTH)),
    ]
    args = [x, g1, sh, sc, ta, tb, cos_t, sin_t, w["wall"], w["qn"], w["wuqt"], w["kvn"], w["wuk"], w["wuvt"],
            w["cmn"], w["ws"], w["bias"]]
    return pl.pallas_call(
        functools.partial(_even_in_kernel, tm=tm),
        grid=(n // tm,),
        in_specs=in_specs,
        out_specs=[
            pl.BlockSpec((hq, tm), col),
            pl.BlockSpec((tm, hq), row),
            pl.BlockSpec((MLA_HEADS * V_SLOT, tm), col),
            pl.BlockSpec((tm, CM_WIDTH), row),
        ],
        out_shape=[
            jax.ShapeDtypeStruct((hq, n), BF16),
            jax.ShapeDtypeStruct((n, hq), BF16),
            jax.ShapeDtypeStruct((MLA_HEADS * V_SLOT, n), BF16),
            jax.ShapeDtypeStruct((n, CM_WIDTH), BF16),
        ],
        compiler_params=_cparams(("arbitrary",)),
        name="even_in",
    )(*args)


def _attn_kernel(*refs, tq, tk, seg_tiles, unroll):
    nseg = len(seg_tiles)
    qt_ref = refs[0]
    kv_refs = [(refs[1 + 2 * i], refs[2 + 2 * i]) for i in range(nseg)]
    o_ref, s_scr, p_scr, acc_scr = refs[1 + 2 * nseg:]
    nsl = s_scr.shape[1]
    nk = sum(seg_tiles)

    def tile(t):
        if not isinstance(t, int):
            return kv_refs[0] + (pl.multiple_of(t * tk, LANES),)
        for (k_ref, vt_ref), n in zip(kv_refs, seg_tiles):
            if t < n:
                return k_ref, vt_ref, t * tk
            t -= n
        raise ValueError("key tile out of range")

    def step(t, ph, h, state, scores=True, softmax=True, weighted=True):
        m, alpha, tile_max = state
        new_alpha, new_tile_max = alpha, tile_max
        if scores:
            k_ref, _, r0 = tile(t)
            ks = k_ref[pl.ds(r0, tk), h * HEAD_SLOT:(h + 1) * HEAD_SLOT]
            s = jnp.dot(ks, qt_ref[h * HEAD_SLOT:(h + 1) * HEAD_SLOT, :], preferred_element_type=F32)
            s_scr[h, ph] = s
            new_tile_max = jnp.max(s, axis=0, keepdims=True)
        if weighted:
            _, vt_ref, r0 = tile(t - 2)
            vs = vt_ref[h * V_SLOT:(h + 1) * V_SLOT, pl.ds(r0, tk)]
            acc_scr[h] = alpha * acc_scr[h] + jnp.dot(vs, p_scr[h, (ph - 2) % nsl], preferred_element_type=F32)
        if softmax:
            m_new = jnp.maximum(m, tile_max)
            new_alpha = jnp.exp2(m - m_new)
            p_scr[h, (ph - 1) % nsl] = jnp.exp2(s_scr[h, (ph - 1) % nsl] - m_new).astype(BF16)
            m = m_new
        return m, new_alpha, new_tile_max

    carry = []
    for h in range(2):
        acc_scr[h] = jnp.zeros((V_SLOT, tq), F32)
        neg = jnp.full((1, tq), -jnp.inf, F32)
        st = step(0, 0, h, (neg, jnp.zeros((1, tq), F32), neg), softmax=False, weighted=False)
        carry.append(step(1, 1 % nsl, h, st, weighted=False))

    assert unroll % nsl == 0
    full = max(seg_tiles[0] - 2, 0)
    groups = 0 if full <= unroll else full // unroll

    def body(g, carry):
        t0 = 2 + unroll * g
        out = []
        for h in range(2):
            st = carry[h]
            for u in range(unroll):
                st = step(t0 + u, (2 + u) % nsl, h, st)
            out.append(st)
        return tuple(out)

    carry = lax.fori_loop(0, groups, body, tuple(carry))
    outs = []
    for h in range(2):
        st = carry[h]
        for t in range(2 + unroll * groups, nk):
            st = step(t, t % nsl, h, st)
        st = step(nk, nk % nsl, h, st, scores=False)
        step(nk + 1, (nk + 1) % nsl, h, st, scores=False, softmax=False)
        acc = acc_scr[h]
        outs.append(acc[0:MLA_V, :] * (1.0 / acc[MLA_V:MLA_V + 1, :]))
    o_ref[...] = jnp.concatenate(outs, axis=0).T.astype(BF16)


def _key_tile(t_len):
    tk = (min(t_len // 2, MAX_KEY_TILE) // LANES) * LANES
    while t_len % tk:
        tk -= LANES
    return tk


def _attn_call(qt, kvs, *, batch, tq, tk):
    n = qt.shape[1]
    nq = n // batch // tq
    in_specs = [pl.BlockSpec((2 * HEAD_SLOT, tq), lambda b, hp, i: (hp, b * nq + i))]
    args = [qt]
    seg_tiles = []
    for k, vt in kvs:
        t_len = k.shape[0] // batch
        assert t_len % tk == 0
        seg_tiles.append(t_len // tk)
        in_specs.append(pl.BlockSpec((t_len, 2 * HEAD_SLOT), lambda b, hp, i: (b, hp)))
        in_specs.append(pl.BlockSpec((2 * V_SLOT, t_len), lambda b, hp, i: (hp, b)))
        args += [k, vt]
    assert sum(seg_tiles) >= 2
    return pl.pallas_call(
        functools.partial(_attn_kernel, tq=tq, tk=tk, seg_tiles=tuple(seg_tiles), unroll=ATTN_UNROLL),
        grid=(batch, MLA_HEADS // 2, nq),
        in_specs=in_specs,
        out_specs=pl.BlockSpec((tq, 2 * MLA_V), lambda b, hp, i: (b * nq + i, hp)),
        out_shape=jax.ShapeDtypeStruct((n, MLA_HEADS * MLA_V), BF16),
        scratch_shapes=[
            pltpu.VMEM((2, ATTN_SLOTS, tk, tq), F32),
            pltpu.VMEM((2, ATTN_SLOTS, tk, tq), BF16),
            pltpu.VMEM((2, V_SLOT, tq), F32),
        ],
        compiler_params=_cparams(("arbitrary", "arbitrary", "arbitrary")),
        name="mla_attn",
    )(*args)


def _post_kernel(*refs, final, gla):
    x_ref = refs[0]
    gm_ref, g2_ref, sh_ref, sc_ref, gf_ref, wo_ref, w1_ref, w2_ref, fg_ref, o_ref = refs[-10:]
    if gla:
        of_ref, ob_ref, r_ref, on_ref = refs[1:5]
        y = None
        for h in range(GLA_HEADS):
            vs = slice(h * GLA_DV, (h + 1) * GLA_DV)
            mix = (_rms(ob_ref[:, vs] + of_ref[:, vs], on_ref[...]) * r_ref[:, vs]).astype(BF16)
            part = jnp.dot(mix, wo_ref[vs, :], preferred_element_type=F32)
            y = part if y is None else y + part
    else:
        a_ref, b_ref = refs[1:3]
        half = wo_ref.shape[0] // 2
        y = (jnp.dot(a_ref[...], wo_ref[:half, :], preferred_element_type=F32)
             + jnp.dot(b_ref[...], wo_ref[half:, :], preferred_element_type=F32))
    x1 = x_ref[...] + gm_ref[0] * y
    h2 = (_rms(x1, g2_ref[...]) * (1.0 + sc_ref[0]) + sh_ref[0]).astype(BF16)
    acc = jnp.zeros_like(x1)
    fc = 1024
    for c in range(w1_ref.shape[1] // fc):
        hc = jnp.dot(h2, w1_ref[:, c * fc:(c + 1) * fc], preferred_element_type=F32)
        hc = jnp.square(jnp.maximum(hc, 0.0)).astype(BF16)
        acc = acc + jnp.dot(hc, w2_ref[c * fc:(c + 1) * fc, :], preferred_element_type=F32)
    x2 = x1 + gf_ref[0] * acc
    if final:
        x2 = _rms(x2, fg_ref[...])
    o_ref[...] = x2


def _post_call(x, mixer, gm, g2, sh, sc, gf, wo, w1, w2, fg, *, seq, tm, final, gla=False):
    n, d = x.shape
    tpb = seq // tm
    row = lambda i: (i, 0)
    bat = lambda i: (i // tpb, 0, 0)
    if gla:
        wide = mixer[0].shape[1]
        mix_specs = [pl.BlockSpec((tm, wide), row)] * 3 + [_const_spec(mixer[3].shape)]
    else:
        mix_specs = [pl.BlockSpec((tm, m.shape[1]), row) for m in mixer]
    return pl.pallas_call(
        functools.partial(_post_kernel, final=final, gla=gla),
        grid=(n // tm,),
        in_specs=[
            pl.BlockSpec((tm, d), row),
            *mix_specs,
            pl.BlockSpec((1, 1, d), bat),
            _const_spec((1, d)),
            pl.BlockSpec((1, 1, d), bat),
            pl.BlockSpec((1, 1, d), bat),
            pl.BlockSpec((1, 1, d), bat),
            _const_spec(wo.shape),
            _const_spec(w1.shape),
            _const_spec(w2.shape),
            _const_spec((1, d)),
        ],
        out_specs=pl.BlockSpec((tm, d), row),
        out_shape=jax.ShapeDtypeStruct((n, d), F32),
        compiler_params=_cparams(("arbitrary",)),
        name="post_mlp",
    )(x, *mixer, gm, g2, sh, sc, gf, wo, w1, w2, fg)


def _log_decay(g):
    return (jnp.minimum(g, 0.0) - jnp.log1p(jnp.exp(-jnp.abs(g)))) * (1.0 / GLA_TAU)


def _chunk_cumsum(tri, x):
    hi = x.astype(BF16)
    r1 = x - hi.astype(F32)
    mid = r1.astype(BF16)
    lo = (r1 - mid.astype(F32)).astype(BF16)
    return (jnp.dot(tri, hi, preferred_element_type=F32) + jnp.dot(tri, mid, preferred_element_type=F32)
            + jnp.dot(tri, lo, preferred_element_type=F32))


def _odd_in_kernel(x_ref, g1_ref, sh_ref, sc_ref, wall_ref, wgf_ref, bgf_ref, wgb_ref, bgb_ref,
                   k_ref, v_ref, q_ref, r_ref, cf_ref, cb_ref, *, tm):
    rr = lax.broadcasted_iota(jnp.int32, (GLA_PAIR, GLA_PAIR), 0)
    cc = lax.broadcasted_iota(jnp.int32, (GLA_PAIR, GLA_PAIR), 1)
    same = (rr // GLA_CHUNK) == (cc // GLA_CHUNK)
    tri_f = (same & (rr >= cc)).astype(BF16)
    tri_b = (same & (rr <= cc)).astype(BF16)
    h = _rms(x_ref[...], g1_ref[...]) * (1.0 + sc_ref[0]) + sh_ref[0]
    p = _bdot(h, wall_ref[...])
    k_ref[...] = p[:, OA_K:OA_V]
    v_ref[...] = p[:, OA_V:OA_END].astype(BF16)
    q_ref[...] = p[:, OA_Q:OA_K] * (GLA_DK ** -0.5)
    r = p[:, OA_R:OA_Q]
    r_ref[...] = r * jax.nn.sigmoid(r)
    z = p[:, OA_Z:OA_R].astype(BF16)
    df = _log_decay(jnp.dot(z, wgf_ref[...], preferred_element_type=F32) + bgf_ref[...])
    db = _log_decay(jnp.dot(z, wgb_ref[...], preferred_element_type=F32) + bgb_ref[...])
    for c in range(tm // GLA_PAIR):
        rs = slice(c * GLA_PAIR, (c + 1) * GLA_PAIR)
        cf_ref[rs, :] = _chunk_cumsum(tri_f, df[rs])
        cb_ref[rs, :] = _chunk_cumsum(tri_b, db[rs])


def _odd_in_call(x, g1, sh, sc, w, *, seq, tm):
    n, d = x.shape
    tpb = seq // tm
    row = lambda i: (i, 0)
    bat = lambda i: (i // tpb, 0, 0)
    kd = GLA_HEADS * GLA_DK
    vd = GLA_HEADS * GLA_DV
    return pl.pallas_call(
        functools.partial(_odd_in_kernel, tm=tm),
        grid=(n // tm,),
        in_specs=[
            pl.BlockSpec((tm, d), row),
            _const_spec((1, d)),
            pl.BlockSpec((1, 1, d), bat),
            pl.BlockSpec((1, 1, d), bat),
            _const_spec((d, OA_END)),
            _const_spec((LANES, kd)),
            _const_spec((1, kd)),
            _const_spec((LANES, kd)),
            _const_spec((1, kd)),
        ],
        out_specs=[
            pl.BlockSpec((tm, kd), row),
            pl.BlockSpec((tm, vd), row),
            pl.BlockSpec((tm, kd), row),
            pl.BlockSpec((tm, vd), row),
            pl.BlockSpec((tm, kd), row),
            pl.BlockSpec((tm, kd), row),
        ],
        out_shape=[
            jax.ShapeDtypeStruct((n, kd), F32),
            jax.ShapeDtypeStruct((n, vd), BF16),
            jax.ShapeDtypeStruct((n, kd), F32),
            jax.ShapeDtypeStruct((n, vd), F32),
            jax.ShapeDtypeStruct((n, kd), F32),
            jax.ShapeDtypeStruct((n, kd), F32),
        ],
        compiler_params=_cparams(("arbitrary",)),
        name="odd_in",
    )(x, g1, sh, sc, w["wall"], w["wgf"], w["bgf"], w["wgb"], w["bgb"])


def _dot_tn(a, b):
    return lax.dot_general(a, b, (((0,), (0,)), ((), ())), preferred_element_type=F32)


def _gla_kernel(q_ref, k_ref, c_ref, v_ref, s0_ref, o_ref, sfin_ref, s_scr, *, reverse, npair):
    i = pl.program_id(1)

    @pl.when(i == 0)
    def _():
        s_scr[...] = s0_ref[...]

    ch, pair = GLA_CHUNK, GLA_PAIR
    rr = lax.broadcasted_iota(jnp.int32, (pair, pair), 0)
    cc = lax.broadcasted_iota(jnp.int32, (pair, pair), 1)
    mask = ((rr // ch) == (cc // ch)) & ((rr <= cc) if reverse else (rr >= cc))
    lo = lax.broadcasted_iota(jnp.int32, (pair, GLA_DK), 0) < ch
    for pr in (range(npair - 1, -1, -1) if reverse else range(npair)):
        rs = slice(pr * pair, (pr + 1) * pair)
        for h in range(GLA_HEADS):
            ks = slice(h * GLA_DK, (h + 1) * GLA_DK)
            vs = slice(h * GLA_DV, (h + 1) * GLA_DV)
            bc = c_ref[rs, ks]
            k = k_ref[rs, ks]
            v = v_ref[rs, vs]
            if reverse:
                t_lo, t_hi = bc[0:1, :], bc[ch:ch + 1, :]
            else:
                t_lo, t_hi = bc[ch - 1:ch, :], bc[pair - 1:pair, :]
            q_t = (q_ref[rs, ks] * jnp.exp(bc)).astype(BF16)
            k_t = (k * jnp.exp(-bc)).astype(BF16)
            k_end = k * jnp.exp(jnp.where(lo, t_lo, t_hi) - bc)
            a = jnp.where(mask, _dot_nt(q_t, k_t), 0.0).astype(BF16)
            o = jnp.dot(a, v, preferred_element_type=F32)
            st = s_scr[h]
            halves = [(slice(0, ch), lo, t_lo), (slice(ch, pair), ~lo, t_hi)]
            inter = {}
            for hs, hm, tot in (halves[::-1] if reverse else halves):
                inter[hs.start] = _dot_nt(q_t[hs], st.astype(BF16))
                st = jnp.exp(tot) * st + _dot_tn(v, jnp.where(hm, k_end, 0.0).astype(BF16))
            s_scr[h] = st
            o_ref[rs, vs] = o + jnp.concatenate([inter[0], inter[ch]], axis=0)

    @pl.when(i == pl.num_programs(1) - 1)
    def _():
        sfin_ref[...] = s_scr[...]


def _gla_call(q, k, c, v, s0, *, reverse, tb):
    b, l, _ = q.shape
    nb = l // tb
    blk = (lambda bi, i: (bi, nb - 1 - i, 0)) if reverse else (lambda bi, i: (bi, i, 0))
    kspec = pl.BlockSpec((None, tb, GLA_HEADS * GLA_DK), blk)
    vspec = pl.BlockSpec((None, tb, GLA_HEADS * GLA_DV), blk)
    sspec = pl.BlockSpec((None, GLA_HEADS, GLA_DV, GLA_DK), lambda bi, i: (bi, 0, 0, 0))
    return pl.pallas_call(
        functools.partial(_gla_kernel, reverse=reverse, npair=tb // GLA_PAIR),
        grid=(b, nb),
        in_specs=[kspec, kspec, kspec, vspec, sspec],
        out_specs=[vspec, sspec],
        out_shape=[
            jax.ShapeDtypeStruct((b, l, GLA_HEADS * GLA_DV), F32),
            jax.ShapeDtypeStruct((b, GLA_HEADS, GLA_DV, GLA_DK), F32),
        ],
        scratch_shapes=[pltpu.VMEM((GLA_HEADS, GLA_DV, GLA_DK), F32)],
        compiler_params=_cparams(("arbitrary", "arbitrary")),
        name="gla_bwd" if reverse else "gla_fwd",
    )(q, k, c, v, s0)


def _rot_cols(w):
    q = MLA_ROPE // 4
    return jnp.concatenate([-w[..., q:2 * q], w[..., 0:q], -w[..., 3 * q:4 * q], w[..., 2 * q:3 * q]], axis=-1)


def _prep_even(w_in, q_norm, w_uq, kv_norm, w_ukv, cm_norm, cm_ws, cm_bs):
    d = w_in.shape[0]
    e_q, e_kv = MLA_Q_RANK, MLA_Q_RANK + MLA_KV_RANK
    e_r = e_kv + MLA_ROPE
    e_u = e_r + CM_WIDTH
    wkr = w_in[:, e_kv:e_r]
    wall = jnp.concatenate([w_in[:, :e_kv], wkr, _rot_cols(wkr), jnp.zeros((d, LANES - 2 * MLA_ROPE), F32),
                            w_in[:, e_r:e_u], w_in[:, e_u:]], axis=1).astype(BF16)
    uq = w_uq.reshape(MLA_Q_RANK, MLA_HEADS, MLA_NOPE + MLA_ROPE)
    uq_r = uq[..., MLA_NOPE:]
    wuqt = jnp.concatenate([uq, _rot_cols(uq_r)], axis=-1).reshape(MLA_Q_RANK, MLA_HEADS * HEAD_SLOT).T.astype(BF16)
    ukv = w_ukv.reshape(MLA_KV_RANK, MLA_HEADS, MLA_NOPE + MLA_V)
    wuk = jnp.concatenate([ukv[..., :MLA_NOPE], jnp.zeros((MLA_KV_RANK, MLA_HEADS, HEAD_SLOT - MLA_NOPE), F32)],
                          axis=-1).reshape(MLA_KV_RANK, MLA_HEADS * HEAD_SLOT).astype(BF16)
    wuvt = ukv[..., MLA_NOPE:].reshape(MLA_KV_RANK, MLA_HEADS * MLA_V).T.astype(BF16)
    bias = jnp.repeat(cm_bs.T, CM_GROUP_DIM, axis=1)
    return dict(wall=wall, qn=q_norm[None], wuqt=wuqt, kvn=kv_norm[None], wuk=wuk, wuvt=wuvt, cmn=cm_norm[None],
                ws=cm_ws.astype(BF16), bias=bias)


def _prep_odd(w_in, w_gf, b_gf, w_gb, b_gb):
    d = w_in.shape[0]
    o_k = GLA_HEADS * GLA_DK
    o_v = o_k + GLA_HEADS * GLA_DV
    o_zb = o_v + 2 * GLA_GATE_RANK
    o_q = o_zb + GLA_HEADS * GLA_DK
    wall = jnp.concatenate([w_in[:, o_v:o_zb], jnp.zeros((d, LANES - 2 * GLA_GATE_RANK), F32), w_in[:, o_q:],
                            w_in[:, o_zb:o_q], w_in[:, :o_v]], axis=1).astype(BF16)
    zr = GLA_GATE_RANK
    wgf = jnp.zeros((LANES, o_k), F32).at[:zr].set(w_gf).astype(BF16)
    wgb = jnp.zeros((LANES, o_k), F32).at[zr:2 * zr].set(w_gb).astype(BF16)
    return dict(wall=wall, wgf=wgf, bgf=b_gf[None], wgb=wgb, bgb=b_gb[None])


def _rope_tables(length):
    rows = length // GRID_W
    r = jnp.repeat(jnp.arange(rows, dtype=F32), GRID_W)
    col = jnp.tile(jnp.arange(GRID_W, dtype=F32), rows)
    half = MLA_ROPE // 2
    inv = ROPE_BASE ** (-jnp.arange(0, half, 2, dtype=F32) / half)
    ang_r = r[:, None] * inv
    ang_c = col[:, None] * inv
    ang = jnp.concatenate([ang_r, ang_r, ang_c, ang_c], axis=-1)
    one = jnp.ones((length, MLA_NOPE), F32)
    pad = jnp.zeros((length, HEAD_SLOT - MLA_NOPE - MLA_ROPE), F32)
    ta = jnp.concatenate([one, jnp.cos(ang), pad], axis=1)
    tb = jnp.concatenate([0.0 * one, jnp.sin(ang), pad], axis=1)
    return ta, tb, jnp.cos(ang).T, jnp.sin(ang).T


def _flat_tables(length):
    ta = jnp.concatenate([jnp.ones((length, MLA_NOPE + MLA_ROPE), F32),
                          jnp.zeros((length, HEAD_SLOT - MLA_NOPE - MLA_ROPE), F32)], axis=1)
    return ta, jnp.zeros_like(ta), jnp.ones((MLA_ROPE, length), F32), jnp.zeros((MLA_ROPE, length), F32)


def _row_tile(seq, want):
    t = min(seq, want)
    while seq % t:
        t //= 2
    return t


def kernel(x, c, ctx, c_ctx, ada_w, ada_b, norm1_g, norm2_g, mlp_w1, mlp_w2, ev_w_in, ev_q_norm, ev_w_uq, ev_kv_norm,
           ev_w_ukv, ev_cm_norm, ev_cm_ws, ev_cm_bs, ev_w_out, od_w_in, od_w_gf, od_b_gf, od_w_gb, od_b_gb, od_o_norm,
           od_w_out, final_g):
    batch, seq, d = x.shape
    lc = ctx.shape[1]
    depth = ada_w.shape[0]
    tm_l = _row_tile(seq, 512)
    tm_c = _row_tile(lc, 256)
    tm_e = tm_l
    tk_l = min(MAX_KEY_TILE, lc)
    assert seq % tk_l == 0 and lc % tk_l == 0
    tq_l = _row_tile(seq, 512)
    tq_c = _row_tile(lc, 256)
    tb_l = _row_tile(seq, 1024)
    tb_c = _row_tile(lc, 256)

    cvec = jnp.concatenate([c, c_ctx[None], jnp.zeros((8 - batch - 1, d), F32)], axis=0)
    mods = _ada_call(cvec, ada_w, ada_b).reshape(depth, 8, 6, d)

    tabs_l = _rope_tables(seq)
    tabs_c = _flat_tables(lc)

    xl = x.reshape(batch * seq, d)
    xc = ctx.reshape(batch * lc, d)
    for i in range(depth):
        need_ctx = i < depth - 1
        j = i // 2
        ml = [mods[i, :batch, t][:, None, :] for t in range(6)]
        mc = [mods[i, batch:batch + 1, t][:, None, :] for t in range(6)]
        g1 = norm1_g[i][None]
        g2 = norm2_g[i][None]
        w1 = mlp_w1[i].astype(BF16)
        w2 = mlp_w2[i].astype(BF16)
        final = i == depth - 1
        if i % 2 == 0:
            w = _prep_even(ev_w_in[j], ev_q_norm[j], ev_w_uq[j], ev_kv_norm[j], ev_w_ukv[j], ev_cm_norm[j],
                           ev_cm_ws[j], ev_cm_bs[j])
            wo = ev_w_out[j].astype(BF16)
            qtl, kl, vtl, mll = _even_in_call(xl, g1, ml[0], ml[1], tabs_l, w, seq=seq, tm=tm_e)
            qtc, kc, vtc, mlc = _even_in_call(xc, g1, mc[0], mc[1], tabs_c, w, seq=batch * lc, tm=tm_c)
            al = _attn_call(qtl, [(kl, vtl), (kc, vtc)], batch=batch, tq=tq_l, tk=tk_l)
            xl = _post_call(xl, (al, mll), ml[2], g2, ml[3], ml[4], ml[5], wo, w1, w2, final_g[None],
                            seq=seq, tm=tm_l, final=final)
            if need_ctx:
                ac = _attn_call(qtc, [(kc, vtc)], batch=batch, tq=tq_c, tk=_key_tile(lc))
                xc = _post_call(xc, (ac, mlc), mc[2], g2, mc[3], mc[4], mc[5], wo, w1, w2, final_g[None],
                                seq=batch * lc, tm=tm_c, final=False)
        else:
            w = _prep_odd(od_w_in[j], od_w_gf[j], od_b_gf[j], od_w_gb[j], od_b_gb[j])
            wo = od_w_out[j].astype(BF16)
            on = od_o_norm[j][None]
            kc, vc, qc, rc, dfc, dbc = _odd_in_call(xc, g1, mc[0], mc[1], w, seq=batch * lc, tm=tm_c)
            kl, vl, ql, rl, dfl, dbl = _odd_in_call(xl, g1, ml[0], ml[1], w, seq=seq, tm=tm_l)
            r3 = lambda t, n: t.reshape(batch, n, t.shape[-1])
            s0 = jnp.zeros((batch, GLA_HEADS, GLA_DV, GLA_DK), F32)
            ocf, s_f = _gla_call(r3(qc, lc), r3(kc, lc), r3(dfc, lc), r3(vc, lc), s0, reverse=False, tb=tb_c)
            ocb, s_b = _gla_call(r3(qc, lc), r3(kc, lc), r3(dbc, lc), r3(vc, lc), s0, reverse=True, tb=tb_c)
            olf, _ = _gla_call(r3(ql, seq), r3(kl, seq), r3(dfl, seq), r3(vl, seq), s_f, reverse=False, tb=tb_l)
            olb, _ = _gla_call(r3(ql, seq), r3(kl, seq), r3(dbl, seq), r3(vl, seq), s_b, reverse=True, tb=tb_l)
            flat = lambda t: t.reshape(-1, t.shape[-1])
            xl = _post_call(xl, (flat(olf), flat(olb), rl, on), ml[2], g2, ml[3], ml[4], ml[5], wo, w1, w2,
                            final_g[None], seq=seq, tm=tm_l, final=final, gla=True)
            if need_ctx:
                xc = _post_call(xc, (flat(ocf), flat(ocb), rc, on), mc[2], g2, mc[3], mc[4], mc[5], wo, w1, w2,
                                final_g[None], seq=batch * lc, tm=tm_c, final=False, gla=True)
    return xl.reshape(batch, seq, d)
```

```python
import functools

import jax
import jax.numpy as jnp
from jax import lax
from jax.experimental import pallas as pl
from jax.experimental.pallas import tpu as pltpu

F32 = jnp.float32
BF16 = jnp.bfloat16

D_MODEL = 1024
DEPTH = 4
GRID_W = 64
EPS = 1e-6
MLA_HEADS = 8
MLA_NOPE = 64
MLA_ROPE = 32
MLA_V = 64
MLA_Q_RANK = 384
MLA_KV_RANK = 256
MLA_SCALE = (MLA_NOPE + MLA_ROPE) ** -0.5
ROPE_BASE = 10000.0
CM_CHUNK = 128
CM_GROUPS = 4
CM_GROUP_DIM = 128
CM_WIDTH = CM_GROUPS * CM_GROUP_DIM
GLA_HEADS = 4
GLA_DK = 128
GLA_DV = 256
GLA_GATE_RANK = 16
GLA_TAU = 16.0
GLA_CHUNK = 64
GLA_PAIR = 2 * GLA_CHUNK
D_FF = 4 * D_MODEL

LANES = 128
HEAD_SLOT = LANES
BF16_ROWS = 16
V_SLOT = MLA_V + BF16_ROWS
VMEM_LIMIT = 56 * 1024 * 1024
MAX_KEY_TILE = 256
ATTN_UNROLL = 63
ATTN_SLOTS = 3
Q_PRESCALE = MLA_SCALE * 1.4426950408889634

EA_Q = 0
EA_KV = EA_Q + MLA_Q_RANK
EA_KR = EA_KV + MLA_KV_RANK
EA_U = EA_KR + LANES
EA_V = EA_U + CM_WIDTH
EA_END = EA_V + CM_WIDTH
OA_Z = 0
OA_R = OA_Z + LANES
OA_Q = OA_R + GLA_HEADS * GLA_DV
OA_K = OA_Q + GLA_HEADS * GLA_DK
OA_V = OA_K + GLA_HEADS * GLA_DK
OA_END = OA_V + GLA_HEADS * GLA_DV


def _cparams(sem):
    return pltpu.CompilerParams(dimension_semantics=sem, vmem_limit_bytes=VMEM_LIMIT)


def _rms(x, g):
    return x * lax.rsqrt(jnp.mean(x * x, axis=-1, keepdims=True) + EPS) * g


def _bdot(a, b):
    return jnp.dot(a.astype(BF16), b.astype(BF16), preferred_element_type=F32)


def _const_spec(shape):
    nd = len(shape)
    return pl.BlockSpec(shape, lambda *_: (0,) * nd, pipeline_mode=pl.Buffered(1))


def _ada_kernel(c_ref, w_ref, b_ref, o_ref):
    s = c_ref[...]
    s = s * jax.nn.sigmoid(s)
    o_ref[0] = _bdot(s, w_ref[0]) + b_ref[0]


def _ada_call(cvec, ada_w, ada_b):
    depth, d, n = ada_w.shape
    tn = 1536
    return pl.pallas_call(
        _ada_kernel,
        grid=(depth, n // tn),
        in_specs=[
            pl.BlockSpec((8, d), lambda l, j: (0, 0)),
            pl.BlockSpec((1, d, tn), lambda l, j: (l, 0, j)),
            pl.BlockSpec((1, 1, tn), lambda l, j: (l, 0, j)),
        ],
        out_specs=pl.BlockSpec((1, 8, tn), lambda l, j: (l, 0, j)),
        out_shape=jax.ShapeDtypeStruct((depth, 8, n), F32),
        compiler_params=_cparams(("arbitrary", "arbitrary")),
        name="ada_mod",
    )(cvec, ada_w, ada_b.reshape(depth, 1, n))


def _rope_slot(t, a, b):
    return t * a + pltpu.roll(t, HEAD_SLOT - MLA_ROPE, 1) * b


def _dot_nt(a, b):
    return lax.dot_general(a, b, (((1,), (1,)), ((), ())), preferred_element_type=F32)


def _even_in_kernel(x_ref, g1_ref, sh_ref, sc_ref, ta_ref, tb_ref, ct_ref, st_ref, wall_ref, qn_ref, wuqt_ref,
                    kvn_ref, wuk_ref, wuvt_ref, cmn_ref, ws_ref, bias_ref, qt_ref, k_ref, vt_ref, ml_ref, *, tm):
    h = _rms(x_ref[...], g1_ref[...]) * (1.0 + sc_ref[0]) + sh_ref[0]
    p = _bdot(h, wall_ref[...])

    cq = _rms(p[:, EA_Q:EA_KV], qn_ref[...]).astype(BF16)
    qt = _dot_nt(wuqt_ref[...], cq)
    cos_t = ct_ref[...]
    sin_t = st_ref[...]
    r0, r1 = MLA_NOPE, MLA_NOPE + MLA_ROPE
    for hd in range(MLA_HEADS):
        b = hd * HEAD_SLOT
        qt_ref[b:b + r0, :] = (qt[b:b + r0, :] * Q_PRESCALE).astype(BF16)
        roped = qt[b + r0:b + r1, :] * cos_t + qt[b + r1:b + HEAD_SLOT, :] * sin_t
        qt_ref[b + r0:b + r1, :] = (roped * Q_PRESCALE).astype(BF16)
        qt_ref[b + r1:b + HEAD_SLOT, :] = jnp.zeros((HEAD_SLOT - r1, tm), BF16)

    ckv = _rms(p[:, EA_KV:EA_KR], kvn_ref[...]).astype(BF16)
    kn = jnp.dot(ckv, wuk_ref[...], preferred_element_type=F32)
    kr = _rope_slot(pltpu.roll(p[:, EA_KR:EA_U], MLA_NOPE, 1), ta_ref[...], tb_ref[...])
    for hd in range(MLA_HEADS):
        sl = slice(hd * HEAD_SLOT, (hd + 1) * HEAD_SLOT)
        k_ref[:, sl] = (kn[:, sl] + kr).astype(BF16)
    vt = _dot_nt(wuvt_ref[...], ckv)
    for hd in range(MLA_HEADS):
        b = hd * V_SLOT
        vt_ref[b:b + MLA_V, :] = vt[hd * MLA_V:(hd + 1) * MLA_V, :].astype(BF16)
        vt_ref[b + MLA_V:b + V_SLOT, :] = jnp.ones((V_SLOT - MLA_V, tm), BF16)

    u = jax.nn.gelu(p[:, EA_U:EA_V])
    vv = jax.nn.gelu(p[:, EA_V:EA_END])
    cmn = cmn_ref[...]
    for g in range(CM_GROUPS):
        gl = slice(g * CM_GROUP_DIM, (g + 1) * CM_GROUP_DIM)
        vn = _rms(vv[:, gl], cmn).astype(BF16)
        w = ws_ref[g]
        for c in range(tm // CM_CHUNK):
            rs = slice(c * CM_CHUNK, (c + 1) * CM_CHUNK)
            y = jnp.dot(w, vn[rs], preferred_element_type=F32) + bias_ref[:, gl]
            ml_ref[rs, gl] = (u[rs, gl] * y).astype(BF16)


def _even_in_call(x, g1, sh, sc, tabs, w, *, seq, tm):
    ta, tb, cos_t, sin_t = tabs
    n, d = x.shape
    tpb = seq // tm
    tpt = ta.shape[0] // tm
    row = lambda i: (i, 0)
    col = lambda i: (0, i)
    bat = lambda i: (i // tpb, 0, 0)
    tab = lambda i: (i % tpt, 0)
    tab_t = lambda i: (0, i % tpt)
    hq = MLA_HEADS * HEAD_SLOT
    hv = MLA_HEADS * MLA_V
    in_specs = [
        pl.BlockSpec((tm, d), row),
        _const_spec((1, d)),
        pl.BlockSpec((1, 1, d), bat),
        pl.BlockSpec((1, 1, d), bat),
        pl.BlockSpec((tm, HEAD_SLOT), tab),
        pl.BlockSpec((tm, HEAD_SLOT), tab),
        pl.BlockSpec((MLA_ROPE, tm), tab_t),
        pl.BlockSpec((MLA_ROPE, tm), tab_t),
        _const_spec((d, EA_END)),
        _const_spec((1, MLA_Q_RANK)),
        _const_spec((hq, MLA_Q_RANK)),
        _const_spec((1, MLA_KV_RANK)),
        _const_spec((MLA_KV_RANK, hq)),
        _const_spec((hv, MLA_KV_RANK)),
        _const_spec((1, CM_GROUP_DIM)),
        _const_spec((CM_GROUPS, CM_CHUNK, CM_CHUNK)),
        _const_spec((CM_CHUNK, CM_WIDTH)),
    ]
    args = [x, g1, sh, sc, ta, tb, cos_t, sin_t, w["wall"], w["qn"], w["wuqt"], w["kvn"], w["wuk"], w["wuvt"],
            w["cmn"], w["ws"], w["bias"]]
    return pl.pallas_call(
        functools.partial(_even_in_kernel, tm=tm),
        grid=(n // tm,),
        in_specs=in_specs,
        out_specs=[
            pl.BlockSpec((hq, tm), col),
            pl.BlockSpec((tm, hq), row),
            pl.BlockSpec((MLA_HEADS * V_SLOT, tm), col),
            pl.BlockSpec((tm, CM_WIDTH), row),
        ],
        out_shape=[
            jax.ShapeDtypeStruct((hq, n), BF16),
            jax.ShapeDtypeStruct((n, hq), BF16),
            jax.ShapeDtypeStruct((MLA_HEADS * V_SLOT, n), BF16),
            jax.ShapeDtypeStruct((n, CM_WIDTH), BF16),
        ],
        compiler_params=_cparams(("arbitrary",)),
        name="even_in",
    )(*args)


def _attn_kernel(*refs, tq, tk, seg_tiles, unroll):
    nseg = len(seg_tiles)
    qt_ref = refs[0]
    kv_refs = [(refs[1 + 2 * i], refs[2 + 2 * i]) for i in range(nseg)]
    o_ref, s_scr, p_scr, acc_scr = refs[1 + 2 * nseg:]
    nsl = s_scr.shape[1]
    nk = sum(seg_tiles)

    def tile(t):
        if not isinstance(t, int):
            return kv_refs[0] + (pl.multiple_of(t * tk, LANES),)
        for (k_ref, vt_ref), n in zip(kv_refs, seg_tiles):
            if t < n:
                return k_ref, vt_ref, t * tk
            t -= n
        raise ValueError("key tile out of range")

    def step(t, ph, h, state, scores=True, softmax=True, weighted=True):
        m, alpha, tile_max = state
        new_alpha, new_tile_max = alpha, tile_max
        if scores:
            k_ref, _, r0 = tile(t)
            ks = k_ref[pl.ds(r0, tk), h * HEAD_SLOT:(h + 1) * HEAD_SLOT]
            s = jnp.dot(ks, qt_ref[h * HEAD_SLOT:(h + 1) * HEAD_SLOT, :], preferred_element_type=F32)
            s_scr[h, ph] = s
            new_tile_max = jnp.max(s, axis=0, keepdims=True)
        if weighted:
            _, vt_ref, r0 = tile(t - 2)
            vs = vt_ref[h * V_SLOT:(h + 1) * V_SLOT, pl.ds(r0, tk)]
            acc_scr[h] = alpha * acc_scr[h] + jnp.dot(vs, p_scr[h, (ph - 2) % nsl], preferred_element_type=F32)
        if softmax:
            m_new = jnp.maximum(m, tile_max)
            new_alpha = jnp.exp2(m - m_new)
            p_scr[h, (ph - 1) % nsl] = jnp.exp2(s_scr[h, (ph - 1) % nsl] - m_new).astype(BF16)
            m = m_new
        return m, new_alpha, new_tile_max

    carry = []
    for h in range(2):
        acc_scr[h] = jnp.zeros((V_SLOT, tq), F32)
        neg = jnp.full((1, tq), -jnp.inf, F32)
        st = step(0, 0, h, (neg, jnp.zeros((1, tq), F32), neg), softmax=False, weighted=False)
        carry.append(step(1, 1 % nsl, h, st, weighted=False))

    assert unroll % nsl == 0
    full = max(seg_tiles[0] - 2, 0)
    groups = 0 if full <= unroll else full // unroll

    def body(g, carry):
        t0 = 2 + unroll * g
        out = []
        for h in range(2):
            st = carry[h]
            for u in range(unroll):
                st = step(t0 + u, (2 + u) % nsl, h, st)
            out.append(st)
        return tuple(out)

    carry = lax.fori_loop(0, groups, body, tuple(carry))
    outs = []
    for h in range(2):
        st = carry[h]
        for t in range(2 + unroll * groups, nk):
            st = step(t, t % nsl, h, st)
        st = step(nk, nk % nsl, h, st, scores=False)
        step(nk + 1, (nk + 1) % nsl, h, st, scores=False, softmax=False)
        acc = acc_scr[h]
        outs.append(acc[0:MLA_V, :] * (1.0 / acc[MLA_V:MLA_V + 1, :]))
    o_ref[...] = jnp.concatenate(outs, axis=0).T.astype(BF16)


def _key_tile(t_len):
    tk = (min(t_len // 2, MAX_KEY_TILE) // LANES) * LANES
    while t_len % tk:
        tk -= LANES
    return tk


def _attn_call(qt, kvs, *, batch, tq, tk):
    n = qt.shape[1]
    nq = n // batch // tq
    in_specs = [pl.BlockSpec((2 * HEAD_SLOT, tq), lambda b, hp, i: (hp, b * nq + i))]
    args = [qt]
    seg_tiles = []
    for k, vt in kvs:
        t_len = k.shape[0] // batch
        assert t_len % tk == 0
        seg_tiles.append(t_len // tk)
        in_specs.append(pl.BlockSpec((t_len, 2 * HEAD_SLOT), lambda b, hp, i: (b, hp)))
        in_specs.append(pl.BlockSpec((2 * V_SLOT, t_len), lambda b, hp, i: (hp, b)))
        args += [k, vt]
    assert sum(seg_tiles) >= 2
    return pl.pallas_call(
        functools.partial(_attn_kernel, tq=tq, tk=tk, seg_tiles=tuple(seg_tiles), unroll=ATTN_UNROLL),
        grid=(batch, MLA_HEADS // 2, nq),
        in_specs=in_specs,
        out_specs=pl.BlockSpec((tq, 2 * MLA_V), lambda b, hp, i: (b * nq + i, hp)),
        out_shape=jax.ShapeDtypeStruct((n, MLA_HEADS * MLA_V), BF16),
        scratch_shapes=[
            pltpu.VMEM((2, ATTN_SLOTS, tk, tq), F32),
            pltpu.VMEM((2, ATTN_SLOTS, tk, tq), BF16),
            pltpu.VMEM((2, V_SLOT, tq), F32),
        ],
        compiler_params=_cparams(("arbitrary", "arbitrary", "arbitrary")),
        name="mla_attn",
    )(*args)


def _post_kernel(*refs, final, gla):
    x_ref = refs[0]
    gm_ref, g2_ref, sh_ref, sc_ref, gf_ref, wo_ref, w1_ref, w2_ref, fg_ref, o_ref = refs[-10:]
    if gla:
        of_ref, ob_ref, r_ref, on_ref = refs[1:5]
        y = None
        for h in range(GLA_HEADS):
            vs = slice(h * GLA_DV, (h + 1) * GLA_DV)
            r = r_ref[:, vs]
            mix = (_rms(ob_ref[:, vs] + of_ref[:, vs], on_ref[...]) * (r * jax.nn.sigmoid(r))).astype(BF16)
            part = jnp.dot(mix, wo_ref[vs, :], preferred_element_type=F32)
            y = part if y is None else y + part
    else:
        a_ref, b_ref = refs[1:3]
        half = wo_ref.shape[0] // 2
        y = (jnp.dot(a_ref[...], wo_ref[:half, :], preferred_element_type=F32)
             + jnp.dot(b_ref[...], wo_ref[half:, :], preferred_element_type=F32))
    x1 = x_ref[...] + gm_ref[0] * y
    h2 = (_rms(x1, g2_ref[...]) * (1.0 + sc_ref[0]) + sh_ref[0]).astype(BF16)
    acc = jnp.zeros_like(x1)
    fc = 1024
    for c in range(w1_ref.shape[1] // fc):
        hc = jnp.dot(h2, w1_ref[:, c * fc:(c + 1) * fc], preferred_element_type=F32)
        hc = jnp.square(jnp.maximum(hc, 0.0)).astype(BF16)
        acc = acc + jnp.dot(hc, w2_ref[c * fc:(c + 1) * fc, :], preferred_element_type=F32)
    x2 = x1 + gf_ref[0] * acc
    if final:
        x2 = _rms(x2, fg_ref[...])
    o_ref[...] = x2


def _post_call(x, mixer, gm, g2, sh, sc, gf, wo, w1, w2, fg, *, seq, tm, final, gla=False):
    n, d = x.shape
    tpb = seq // tm
    row = lambda i: (i, 0)
    bat = lambda i: (i // tpb, 0, 0)
    if gla:
        wide = mixer[0].shape[1]
        mix_specs = [pl.BlockSpec((tm, wide), row)] * 3 + [_const_spec(mixer[3].shape)]
    else:
        mix_specs = [pl.BlockSpec((tm, m.shape[1]), row) for m in mixer]
    return pl.pallas_call(
        functools.partial(_post_kernel, final=final, gla=gla),
        grid=(n // tm,),
        in_specs=[
            pl.BlockSpec((tm, d), row),
            *mix_specs,
            pl.BlockSpec((1, 1, d), bat),
            _const_spec((1, d)),
            pl.BlockSpec((1, 1, d), bat),
            pl.BlockSpec((1, 1, d), bat),
            pl.BlockSpec((1, 1, d), bat),
            _const_spec(wo.shape),
            _const_spec(w1.shape),
            _const_spec(w2.shape),
            _const_spec((1, d)),
        ],
        out_specs=pl.BlockSpec((tm, d), row),
        out_shape=jax.ShapeDtypeStruct((n, d), F32),
        compiler_params=_cparams(("arbitrary",)),
        name="post_mlp",
    )(x, *mixer, gm, g2, sh, sc, gf, wo, w1, w2, fg)


def _log_decay(g):
    return (jnp.minimum(g, 0.0) - jnp.log1p(jnp.exp(-jnp.abs(g)))) * (1.0 / GLA_TAU)


def _chunk_cumsum(tri, x):
    hi = x.astype(BF16)
    r1 = x - hi.astype(F32)
    mid = r1.astype(BF16)
    lo = (r1 - mid.astype(F32)).astype(BF16)
    return (jnp.dot(tri, hi, preferred_element_type=F32) + jnp.dot(tri, mid, preferred_element_type=F32)
            + jnp.dot(tri, lo, preferred_element_type=F32))


def _odd_in_kernel(x_ref, g1_ref, sh_ref, sc_ref, wall_ref, wgf_ref, bgf_ref, wgb_ref, bgb_ref,
                   k_ref, v_ref, q_ref, r_ref, cf_ref, cb_ref, *, tm):
    rr = lax.broadcasted_iota(jnp.int32, (GLA_PAIR, GLA_PAIR), 0)
    cc = lax.broadcasted_iota(jnp.int32, (GLA_PAIR, GLA_PAIR), 1)
    same = (rr // GLA_CHUNK) == (cc // GLA_CHUNK)
    tri_f = (same & (rr >= cc)).astype(BF16)
    tri_b = (same & (rr <= cc)).astype(BF16)
    h = _rms(x_ref[...], g1_ref[...]) * (1.0 + sc_ref[0]) + sh_ref[0]
    p = _bdot(h, wall_ref[...])
    k_ref[...] = p[:, OA_K:OA_V]
    v_ref[...] = p[:, OA_V:OA_END].astype(BF16)
    q_ref[...] = p[:, OA_Q:OA_K] * (GLA_DK ** -0.5)
    r_ref[...] = p[:, OA_R:OA_Q]
    z = p[:, OA_Z:OA_R].astype(BF16)
    df = _log_decay(jnp.dot(z, wgf_ref[...], preferred_element_type=F32) + bgf_ref[...])
    db = _log_decay(jnp.dot(z, wgb_ref[...], preferred_element_type=F32) + bgb_ref[...])
    for c in range(tm // GLA_PAIR):
        rs = slice(c * GLA_PAIR, (c + 1) * GLA_PAIR)
        cf_ref[rs, :] = _chunk_cumsum(tri_f, df[rs])
        cb_ref[rs, :] = _chunk_cumsum(tri_b, db[rs])


def _odd_in_call(x, g1, sh, sc, w, *, seq, tm):
    n, d = x.shape
    tpb = seq // tm
    row = lambda i: (i, 0)
    bat = lambda i: (i // tpb, 0, 0)
    kd = GLA_HEADS * GLA_DK
    vd = GLA_HEADS * GLA_DV
    return pl.pallas_call(
        functools.partial(_odd_in_kernel, tm=tm),
        grid=(n // tm,),
        in_specs=[
            pl.BlockSpec((tm, d), row),
            _const_spec((1, d)),
            pl.BlockSpec((1, 1, d), bat),
            pl.BlockSpec((1, 1, d), bat),
            _const_spec((d, OA_END)),
            _const_spec((LANES, kd)),
            _const_spec((1, kd)),
            _const_spec((LANES, kd)),
            _const_spec((1, kd)),
        ],
        out_specs=[
            pl.BlockSpec((tm, kd), row),
            pl.BlockSpec((tm, vd), row),
            pl.BlockSpec((tm, kd), row),
            pl.BlockSpec((tm, vd), row),
            pl.BlockSpec((tm, kd), row),
            pl.BlockSpec((tm, kd), row),
        ],
        out_shape=[
            jax.ShapeDtypeStruct((n, kd), F32),
            jax.ShapeDtypeStruct((n, vd), BF16),
            jax.ShapeDtypeStruct((n, kd), F32),
            jax.ShapeDtypeStruct((n, vd), F32),
            jax.ShapeDtypeStruct((n, kd), F32),
            jax.ShapeDtypeStruct((n, kd), F32),
        ],
        compiler_params=_cparams(("arbitrary",)),
        name="odd_in",
    )(x, g1, sh, sc, w["wall"], w["wgf"], w["bgf"], w["wgb"], w["bgb"])


def _dot_tn(a, b):
    return lax.dot_general(a, b, (((0,), (0,)), ((), ())), preferred_element_type=F32)


def _gla_kernel(q_ref, k_ref, c_ref, v_ref, s0_ref, o_ref, sfin_ref, s_scr, *, reverse, npair):
    i = pl.program_id(1)

    @pl.when(i == 0)
    def _():
        s_scr[...] = s0_ref[...]

    ch, pair = GLA_CHUNK, GLA_PAIR
    rr = lax.broadcasted_iota(jnp.int32, (pair, pair), 0)
    cc = lax.broadcasted_iota(jnp.int32, (pair, pair), 1)
    mask = ((rr // ch) == (cc // ch)) & ((rr <= cc) if reverse else (rr >= cc))
    lo = lax.broadcasted_iota(jnp.int32, (pair, GLA_DK), 0) < ch
    for pr in (range(npair - 1, -1, -1) if reverse else range(npair)):
        rs = slice(pr * pair, (pr + 1) * pair)
        for h in range(GLA_HEADS):
            ks = slice(h * GLA_DK, (h + 1) * GLA_DK)
            vs = slice(h * GLA_DV, (h + 1) * GLA_DV)
            bc = c_ref[rs, ks]
            k = k_ref[rs, ks]
            v = v_ref[rs, vs]
            if reverse:
                t_lo, t_hi = bc[0:1, :], bc[ch:ch + 1, :]
            else:
                t_lo, t_hi = bc[ch - 1:ch, :], bc[pair - 1:pair, :]
            q_t = (q_ref[rs, ks] * jnp.exp(bc)).astype(BF16)
            k_t = (k * jnp.exp(-bc)).astype(BF16)
            k_end = k * jnp.exp(jnp.where(lo, t_lo, t_hi) - bc)
            a = jnp.where(mask, _dot_nt(q_t, k_t), 0.0).astype(BF16)
            o = jnp.dot(a, v, preferred_element_type=F32)
            st = s_scr[h]
            halves = [(slice(0, ch), lo, t_lo), (slice(ch, pair), ~lo, t_hi)]
            inter = {}
            for hs, hm, tot in (halves[::-1] if reverse else halves):
                inter[hs.start] = _dot_nt(q_t[hs], st.astype(BF16))
                st = jnp.exp(tot) * st + _dot_tn(v, jnp.where(hm, k_end, 0.0).astype(BF16))
            s_scr[h] = st
            o_ref[rs, vs] = o + jnp.concatenate([inter[0], inter[ch]], axis=0)

    @pl.when(i == pl.num_programs(1) - 1)
    def _():
        sfin_ref[...] = s_scr[...]


def _gla_call(q, k, c, v, s0, *, reverse, tb):
    b, l, _ = q.shape
    nb = l // tb
    blk = (lambda bi, i: (bi, nb - 1 - i, 0)) if reverse else (lambda bi, i: (bi, i, 0))
    kspec = pl.BlockSpec((None, tb, GLA_HEADS * GLA_DK), blk)
    vspec = pl.BlockSpec((None, tb, GLA_HEADS * GLA_DV), blk)
    sspec = pl.BlockSpec((None, GLA_HEADS, GLA_DV, GLA_DK), lambda bi, i: (bi, 0, 0, 0))
    return pl.pallas_call(
        functools.partial(_gla_kernel, reverse=reverse, npair=tb // GLA_PAIR),
        grid=(b, nb),
        in_specs=[kspec, kspec, kspec, vspec, sspec],
        out_specs=[vspec, sspec],
        out_shape=[
            jax.ShapeDtypeStruct((b, l, GLA_HEADS * GLA_DV), F32),
            jax.ShapeDtypeStruct((b, GLA_HEADS, GLA_DV, GLA_DK), F32),
        ],
        scratch_shapes=[pltpu.VMEM((GLA_HEADS, GLA_DV, GLA_DK), F32)],
        compiler_params=_cparams(("arbitrary", "arbitrary")),
        name="gla_bwd" if reverse else "gla_fwd",
    )(q, k, c, v, s0)


def _rot_cols(w):
    q = MLA_ROPE // 4
    return jnp.concatenate([-w[..., q:2 * q], w[..., 0:q], -w[..., 3 * q:4 * q], w[..., 2 * q:3 * q]], axis=-1)


def _prep_even(w_in, q_norm, w_uq, kv_norm, w_ukv, cm_norm, cm_ws, cm_bs):
    d = w_in.shape[0]
    e_q, e_kv = MLA_Q_RANK, MLA_Q_RANK + MLA_KV_RANK
    e_r = e_kv + MLA_ROPE
    e_u = e_r + CM_WIDTH
    wkr = w_in[:, e_kv:e_r]
    wall = jnp.concatenate([w_in[:, :e_kv], wkr, _rot_cols(wkr), jnp.zeros((d, LANES - 2 * MLA_ROPE), F32),
                            w_in[:, e_r:e_u], w_in[:, e_u:]], axis=1).astype(BF16)
    uq = w_uq.reshape(MLA_Q_RANK, MLA_HEADS, MLA_NOPE + MLA_ROPE)
    uq_r = uq[..., MLA_NOPE:]
    wuqt = jnp.concatenate([uq, _rot_cols(uq_r)], axis=-1).reshape(MLA_Q_RANK, MLA_HEADS * HEAD_SLOT).T.astype(BF16)
    ukv = w_ukv.reshape(MLA_KV_RANK, MLA_HEADS, MLA_NOPE + MLA_V)
    wuk = jnp.concatenate([ukv[..., :MLA_NOPE], jnp.zeros((MLA_KV_RANK, MLA_HEADS, HEAD_SLOT - MLA_NOPE), F32)],
                          axis=-1).reshape(MLA_KV_RANK, MLA_HEADS * HEAD_SLOT).astype(BF16)
    wuvt = ukv[..., MLA_NOPE:].reshape(MLA_KV_RANK, MLA_HEADS * MLA_V).T.astype(BF16)
    bias = jnp.repeat(cm_bs.T, CM_GROUP_DIM, axis=1)
    return dict(wall=wall, qn=q_norm[None], wuqt=wuqt, kvn=kv_norm[None], wuk=wuk, wuvt=wuvt, cmn=cm_norm[None],
                ws=cm_ws.astype(BF16), bias=bias)


def _prep_odd(w_in, w_gf, b_gf, w_gb, b_gb):
    d = w_in.shape[0]
    o_k = GLA_HEADS * GLA_DK
    o_v = o_k + GLA_HEADS * GLA_DV
    o_zb = o_v + 2 * GLA_GATE_RANK
    o_q = o_zb + GLA_HEADS * GLA_DK
    wall = jnp.concatenate([w_in[:, o_v:o_zb], jnp.zeros((d, LANES - 2 * GLA_GATE_RANK), F32), w_in[:, o_q:],
                            w_in[:, o_zb:o_q], w_in[:, :o_v]], axis=1).astype(BF16)
    zr = GLA_GATE_RANK
    wgf = jnp.zeros((LANES, o_k), F32).at[:zr].set(w_gf).astype(BF16)
    wgb = jnp.zeros((LANES, o_k), F32).at[zr:2 * zr].set(w_gb).astype(BF16)
    return dict(wall=wall, wgf=wgf, bgf=b_gf[None], wgb=wgb, bgb=b_gb[None])


def _rope_tables(length):
    rows = length // GRID_W
    r = jnp.repeat(jnp.arange(rows, dtype=F32), GRID_W)
    col = jnp.tile(jnp.arange(GRID_W, dtype=F32), rows)
    half = MLA_ROPE // 2
    inv = ROPE_BASE ** (-jnp.arange(0, half, 2, dtype=F32) / half)
    ang_r = r[:, None] * inv
    ang_c = col[:, None] * inv
    ang = jnp.concatenate([ang_r, ang_r, ang_c, ang_c], axis=-1)
    one = jnp.ones((length, MLA_NOPE), F32)
    pad = jnp.zeros((length, HEAD_SLOT - MLA_NOPE - MLA_ROPE), F32)
    ta = jnp.concatenate([one, jnp.cos(ang), pad], axis=1)
    tb = jnp.concatenate([0.0 * one, jnp.sin(ang), pad], axis=1)
    return ta, tb, jnp.cos(ang).T, jnp.sin(ang).T


def _flat_tables(length):
    ta = jnp.concatenate([jnp.ones((length, MLA_NOPE + MLA_ROPE), F32),
                          jnp.zeros((length, HEAD_SLOT - MLA_NOPE - MLA_ROPE), F32)], axis=1)
    return ta, jnp.zeros_like(ta), jnp.ones((MLA_ROPE, length), F32), jnp.zeros((MLA_ROPE, length), F32)


def _row_tile(seq, want):
    t = min(seq, want)
    while seq % t:
        t //= 2
    return t


def kernel(x, c, ctx, c_ctx, ada_w, ada_b, norm1_g, norm2_g, mlp_w1, mlp_w2, ev_w_in, ev_q_norm, ev_w_uq, ev_kv_norm,
           ev_w_ukv, ev_cm_norm, ev_cm_ws, ev_cm_bs, ev_w_out, od_w_in, od_w_gf, od_b_gf, od_w_gb, od_b_gb, od_o_norm,
           od_w_out, final_g):
    batch, seq, d = x.shape
    lc = ctx.shape[1]
    depth = ada_w.shape[0]
    tm_l = _row_tile(seq, 512)
    tm_c = _row_tile(lc, 256)
    tm_e = tm_l
    tk_l = min(MAX_KEY_TILE, lc)
    assert seq % tk_l == 0 and lc % tk_l == 0
    tq_l = _row_tile(seq, 512)
    tq_c = _row_tile(lc, 256)
    tb_l = _row_tile(seq, 1024)
    tb_c = _row_tile(lc, 256)

    cvec = jnp.concatenate([c, c_ctx[None], jnp.zeros((8 - batch - 1, d), F32)], axis=0)
    mods = _ada_call(cvec, ada_w, ada_b).reshape(depth, 8, 6, d)

    tabs_l = _rope_tables(seq)
    tabs_c = _flat_tables(lc)

    xl = x.reshape(batch * seq, d)
    xc = ctx.reshape(batch * lc, d)
    for i in range(depth):
        need_ctx = i < depth - 1
        j = i // 2
        ml = [mods[i, :batch, t][:, None, :] for t in range(6)]
        mc = [mods[i, batch:batch + 1, t][:, None, :] for t in range(6)]
        g1 = norm1_g[i][None]
        g2 = norm2_g[i][None]
        w1 = mlp_w1[i].astype(BF16)
        w2 = mlp_w2[i].astype(BF16)
        final = i == depth - 1
        if i % 2 == 0:
            w = _prep_even(ev_w_in[j], ev_q_norm[j], ev_w_uq[j], ev_kv_norm[j], ev_w_ukv[j], ev_cm_norm[j],
                           ev_cm_ws[j], ev_cm_bs[j])
            wo = ev_w_out[j].astype(BF16)
            qtl, kl, vtl, mll = _even_in_call(xl, g1, ml[0], ml[1], tabs_l, w, seq=seq, tm=tm_e)
            qtc, kc, vtc, mlc = _even_in_call(xc, g1, mc[0], mc[1], tabs_c, w, seq=batch * lc, tm=tm_c)
            al = _attn_call(qtl, [(kl, vtl), (kc, vtc)], batch=batch, tq=tq_l, tk=tk_l)
            xl = _post_call(xl, (al, mll), ml[2], g2, ml[3], ml[4], ml[5], wo, w1, w2, final_g[None],
                            seq=seq, tm=tm_l, final=final)
            if need_ctx:
                ac = _attn_call(qtc, [(kc, vtc)], batch=batch, tq=tq_c, tk=_key_tile(lc))
                xc = _post_call(xc, (ac, mlc), mc[2], g2, mc[3], mc[4], mc[5], wo, w1, w2, final_g[None],
                                seq=batch * lc, tm=tm_c, final=False)
        else:
            w = _prep_odd(od_w_in[j], od_w_gf[j], od_b_gf[j], od_w_gb[j], od_b_gb[j])
            wo = od_w_out[j].astype(BF16)
            on = od_o_norm[j][None]
            kc, vc, qc, rc, dfc, dbc = _odd_in_call(xc, g1, mc[0], mc[1], w, seq=batch * lc, tm=tm_c)
            kl, vl, ql, rl, dfl, dbl = _odd_in_call(xl, g1, ml[0], ml[1], w, seq=seq, tm=tm_l)
            r3 = lambda t, n: t.reshape(batch, n, t.shape[-1])
            s0 = jnp.zeros((batch, GLA_HEADS, GLA_DV, GLA_DK), F32)
            ocf, s_f = _gla_call(r3(qc, lc), r3(kc, lc), r3(dfc, lc), r3(vc, lc), s0, reverse=False, tb=tb_c)
            ocb, s_b = _gla_call(r3(qc, lc), r3(kc, lc), r3(dbc, lc), r3(vc, lc), s0, reverse=True, tb=tb_c)
            olf, _ = _gla_call(r3(ql, seq), r3(kl, seq), r3(dfl, seq), r3(vl, seq), s_f, reverse=False, tb=tb_l)
            olb, _ = _gla_call(r3(ql, seq), r3(kl, seq), r3(dbl, seq), r3(vl, seq), s_b, reverse=True, tb=tb_l)
            flat = lambda t: t.reshape(-1, t.shape[-1])
            xl = _post_call(xl, (flat(olf), flat(olb), rl, on), ml[2], g2, ml[3], ml[4], ml[5], wo, w1, w2,
                            final_g[None], seq=seq, tm=tm_l, final=final, gla=True)
            if need_ctx:
                xc = _post_call(xc, (flat(ocf), flat(ocb), rc, on), mc[2], g2, mc[3], mc[4], mc[5], wo, w1, w2,
                                final_g[None], seq=batch * lc, tm=tm_c, final=False, gla=True)
    return xl.reshape(batch, seq, d)
```
